```python
import math
import jax, jax.numpy as jnp
from jax import lax
import numpy as np

D_MODEL = 1024
BATCH = 8
SEQ = 2048
DEPTH = 1
DEC_BATCH = 16
DEC_SEQ = 32
PAST_LEN = 4096

CHUNK = 64
CONV_DIM = 512
CONV_WIDTH = 31
N_HEADS = 8
HEAD_DIM = 64
N_KV_HEADS = 2
ATTN_DIM = N_HEADS * HEAD_DIM
MIX_DIM = CONV_DIM + ATTN_DIM
KV_DIM = N_KV_HEADS * HEAD_DIM
ROT_DIM = HEAD_DIM // 4
ROPE_THETA = 500000.0
N_IDX_HEADS = 8
IDX_DIM = 32
TOPK_MAX = 256
Q_BLOCK = 128
D_FF = 2816
FFN_CONV_WIDTH = 3
EPS = 1e-6
IN_SIZES = (CONV_DIM, CONV_DIM, ATTN_DIM, KV_DIM, KV_DIM, N_IDX_HEADS * IDX_DIM, IDX_DIM, N_IDX_HEADS)
IN_DIM = sum(IN_SIZES)
IN_SPLITS = tuple(int(s) for s in np.cumsum(IN_SIZES)[:-1])

kernel_name = "hybrid_conv_dsa_streaming_step"


def _rmsnorm(x, g):
    xf = x.astype(jnp.float32)
    y = xf * lax.rsqrt(jnp.mean(xf * xf, axis=-1, keepdims=True) + EPS)
    return (y * g.astype(jnp.float32)).astype(x.dtype)


def _layernorm(x, g, b):
    xf = x.astype(jnp.float32)
    mu = jnp.mean(xf, axis=-1, keepdims=True)
    var = jnp.mean(jnp.square(xf - mu), axis=-1, keepdims=True)
    y = (xf - mu) * lax.rsqrt(var + EPS) * g.astype(jnp.float32) + b.astype(jnp.float32)
    return y.astype(x.dtype)


def _rope(x, pos):
    half = ROT_DIM // 2
    inv = 1.0 / (ROPE_THETA ** (jnp.arange(0, ROT_DIM, 2, dtype=jnp.float32) / ROT_DIM))
    ang = pos.astype(jnp.float32)[:, None] * inv[None, :]
    cos = jnp.cos(ang)[None, :, None, :]
    sin = jnp.sin(ang)[None, :, None, :]
    xr = x[..., :ROT_DIM].astype(jnp.float32)
    x1, x2 = xr[..., :half], xr[..., half:]
    rot = jnp.concatenate([x1 * cos - x2 * sin, x2 * cos + x1 * sin], axis=-1)
    return jnp.concatenate([rot.astype(x.dtype), x[..., ROT_DIM:]], axis=-1)


def _dwconv(x_ext, w, b):
    c = x_ext.shape[-1]
    y = lax.conv_general_dilated(x_ext, w[:, None, :].astype(x_ext.dtype), window_strides=(1,), padding="VALID",
                                 dimension_numbers=("NWC", "WIO", "NWC"), feature_group_count=c)
    return y + b.astype(x_ext.dtype)


def _sparse_attend(q, qi, wi, k_all, v_all, ki_all, limit, k_sel):
    b, tq = q.shape[:2]
    n_keys = k_all.shape[1]
    admissible = jnp.arange(n_keys)[None, :] < limit[:, None]
    dots = jnp.einsum("bthd,bsd->bths", qi.astype(jnp.float32), ki_all.astype(jnp.float32))
    score = jnp.einsum("bth,bths->bts", wi.astype(jnp.float32), jax.nn.relu(dots))
    score = jnp.where(admissible[None], score, -jnp.inf)
    _, idx = lax.top_k(score, k_sel)
    valid = idx < limit[None, :, None]
    gather = jax.vmap(lambda rows, ids: rows[ids])
    k_g = gather(k_all, idx)
    v_g = gather(v_all, idx)
    qg = q.reshape(b, tq, N_KV_HEADS, N_HEADS // N_KV_HEADS, HEAD_DIM)
    logits = jnp.einsum("btgrd,btkgd->btgrk", qg.astype(jnp.float32), k_g.astype(jnp.float32)) * (HEAD_DIM ** -0.5)
    logits = jnp.where(valid[:, :, None, None, :], logits, -jnp.inf)
    p = jax.nn.softmax(logits, axis=-1)
    o = jnp.einsum("btgrk,btkgd->btgrd", p.astype(v_g.dtype), v_g)
    return o.reshape(b, tq, ATTN_DIM)


def _attend_prompt(q, qi, wi, k, v, ki):
    b, s = q.shape[:2]
    nb = s // Q_BLOCK
    k_sel = min(TOPK_MAX, s // 4)
    limit = (jnp.arange(s, dtype=jnp.int32) // CHUNK + 1) * CHUNK

    def blk(a):
        return jnp.moveaxis(a.reshape((b, nb, Q_BLOCK) + a.shape[2:]), 1, 0)

    xs = (blk(q), blk(qi), blk(wi), limit.reshape(nb, Q_BLOCK))
    out = lax.map(lambda a: _sparse_attend(a[0], a[1], a[2], k, v, ki, a[3], k_sel), xs)
    return jnp.moveaxis(out, 0, 1).reshape(b, s, ATTN_DIM)


def _attend_cached(q, qi, wi, k, v, ki, ck, cv, cki):
    k_all = jnp.concatenate([ck.astype(k.dtype), k], axis=1)
    v_all = jnp.concatenate([cv.astype(v.dtype), v], axis=1)
    ki_all = jnp.concatenate([cki.astype(ki.dtype), ki], axis=1)
    n_keys = k_all.shape[1]
    limit = jnp.full((q.shape[1],), n_keys, dtype=jnp.int32)
    return _sparse_attend(q, qi, wi, k_all, v_all, ki_all, limit, min(TOPK_MAX, n_keys // 4))


def _layer(x, c, pos, conv_hist, ffn_hist, past, w):
    (w_ada, b_ada, g1, w_in, dw_w, dw_b, ln_g, ln_b, w_out, g2, w_up, fdw_w, fdw_b, w_down) = w
    b, t, _ = x.shape
    mod = jnp.einsum("bd,de->be", jax.nn.silu(c), w_ada) + b_ada
    sh1, sc1, gt1, sh2, sc2, gt2 = [m[:, None, :] for m in jnp.split(mod, 6, axis=-1)]

    h = _rmsnorm(x, g1) * (1 + sc1) + sh1
    z = jnp.einsum("btd,de->bte", h, w_in)
    u, ug, q, k, v, qi, ki, wi = jnp.split(z, IN_SPLITS, axis=-1)

    glu = u * jax.nn.sigmoid(ug)
    conv_ext = jnp.concatenate([conv_hist.astype(glu.dtype), glu], axis=1)
    conv_out = jax.nn.silu(_layernorm(_dwconv(conv_ext, dw_w, dw_b), ln_g, ln_b))
    new_conv = conv_ext[:, -(CONV_WIDTH - 1):]

    q = _rope(q.reshape(b, t, N_HEADS, HEAD_DIM), pos)
    k = _rope(k.reshape(b, t, N_KV_HEADS, HEAD_DIM), pos)
    v = v.reshape(b, t, N_KV_HEADS, HEAD_DIM)
    qi = qi.reshape(b, t, N_IDX_HEADS, IDX_DIM)
    if past is None:
        attn_out = _attend_prompt(q, qi, wi, k, v, ki)
    else:
        attn_out = _attend_cached(q, qi, wi, k, v, ki, *past)

    mix = jnp.einsum("btm,md->btd", jnp.concatenate([conv_out, attn_out], axis=-1), w_out)
    x = x + gt1 * mix

    h2 = _rmsnorm(x, g2) * (1 + sc2) + sh2
    up = jnp.einsum("btd,df->btf", h2, w_up)
    ffn_ext = jnp.concatenate([ffn_hist.astype(up.dtype), up], axis=1)
    a, g = jnp.split(_dwconv(ffn_ext, fdw_w, fdw_b), 2, axis=-1)
    x = x + gt2 * jnp.einsum("btf,fd->btd", a * jax.nn.silu(g), w_down)
    new_ffn = ffn_ext[:, -(FFN_CONV_WIDTH - 1):]
    return x, (k, v, ki, new_conv, new_ffn)


def setup_inputs(seed: int = 0) -> dict:
    key = jax.random.key(seed)
    ks = jax.random.split(key, 32)

    def nrm(k, shape, s=1.0):
        return jax.random.normal(k, shape, jnp.float32) * s

    return {
        "x_prompt": nrm(ks[0], (BATCH, SEQ, D_MODEL)),
        "x_sample": nrm(ks[1], (DEC_BATCH, DEC_SEQ, D_MODEL)),
        "cache_k": nrm(ks[2], (DEPTH, DEC_BATCH, PAST_LEN, N_KV_HEADS, HEAD_DIM)),
        "cache_v": nrm(ks[3], (DEPTH, DEC_BATCH, PAST_LEN, N_KV_HEADS, HEAD_DIM)),
        "cache_kidx": nrm(ks[4], (DEPTH, DEC_BATCH, PAST_LEN, IDX_DIM)),
        "state_conv": nrm(ks[5], (DEPTH, DEC_BATCH, CONV_WIDTH - 1, CONV_DIM), 0.5),
        "state_ffn_conv": nrm(ks[6], (DEPTH, DEC_BATCH, FFN_CONV_WIDTH - 1, 2 * D_FF)),
        "c_prompt": nrm(ks[7], (BATCH, D_MODEL)),
        "c_sample": nrm(ks[8], (DEC_BATCH, D_MODEL)),
        "w_ada": nrm(ks[9], (DEPTH, D_MODEL, 6 * D_MODEL), 0.5 * D_MODEL ** -0.5),
        "b_ada": nrm(ks[10], (DEPTH, 6 * D_MODEL), 0.01),
        "norm1_g": 1.0 + nrm(ks[11], (DEPTH, D_MODEL), 0.01),
        "w_in": nrm(ks[12], (DEPTH, D_MODEL, IN_DIM), D_MODEL ** -0.5),
        "conv_dw_w": nrm(ks[13], (DEPTH, CONV_WIDTH, CONV_DIM), CONV_WIDTH ** -0.5),
        "conv_dw_b": nrm(ks[14], (DEPTH, CONV_DIM), 0.01),
        "conv_ln_g": 1.0 + nrm(ks[15], (DEPTH, CONV_DIM), 0.01),
        "conv_ln_b": nrm(ks[16], (DEPTH, CONV_DIM), 0.01),
        "w_out": nrm(ks[17], (DEPTH, MIX_DIM, D_MODEL), MIX_DIM ** -0.5),
        "norm2_g": 1.0 + nrm(ks[18], (DEPTH, D_MODEL), 0.01),
        "w_up": nrm(ks[19], (DEPTH, D_MODEL, 2 * D_FF), D_MODEL ** -0.5),
        "ffn_dw_w": nrm(ks[20], (DEPTH, FFN_CONV_WIDTH, 2 * D_FF), FFN_CONV_WIDTH ** -0.5),
        "ffn_dw_b": nrm(ks[21], (DEPTH, 2 * D_FF), 0.01),
        "w_down": nrm(ks[22], (DEPTH, D_FF, D_MODEL), D_FF ** -0.5),
        "final_norm_g": 1.0 + nrm(ks[23], (D_MODEL,), 0.01),
    }


def reference(x_prompt, x_sample, cache_k, cache_v, cache_kidx, state_conv, state_ffn_conv, c_prompt, c_sample,
              w_ada, b_ada, norm1_g, w_in, conv_dw_w, conv_dw_b, conv_ln_g, conv_ln_b, w_out, norm2_g,
              w_up, ffn_dw_w, ffn_dw_b, w_down, final_norm_g):
    bp, sp = x_prompt.shape[:2]
    ts = x_sample.shape[1]
    past_len = cache_k.shape[2]
    pos_p = jnp.arange(sp, dtype=jnp.int32)
    pos_s = past_len + jnp.arange(ts, dtype=jnp.int32)
    hp, hs = x_prompt, x_sample
    outs_p, outs_s = [], []
    for l in range(DEPTH):
        w = (w_ada[l], b_ada[l], norm1_g[l], w_in[l], conv_dw_w[l], conv_dw_b[l], conv_ln_g[l], conv_ln_b[l],
             w_out[l], norm2_g[l], w_up[l], ffn_dw_w[l], ffn_dw_b[l], w_down[l])
        conv0 = jnp.zeros((bp, CONV_WIDTH - 1, CONV_DIM), hp.dtype)
        ffn0 = jnp.zeros((bp, FFN_CONV_WIDTH - 1, 2 * D_FF), hp.dtype)
        hp, st_p = _layer(hp, c_prompt, pos_p, conv0, ffn0, None, w)
        hs, st_s = _layer(hs, c_sample, pos_s, state_conv[l], state_ffn_conv[l],
                          (cache_k[l], cache_v[l], cache_kidx[l]), w)
        outs_p.append(st_p)
        outs_s.append(st_s)
    y_prompt = _rmsnorm(hp, final_norm_g)
    y_sample = _rmsnorm(hs, final_norm_g)
    k_p, v_p, ki_p, conv_p, ffn_p = [jnp.stack(z) for z in zip(*outs_p)]
    k_s, v_s, ki_s, conv_s, ffn_s = [jnp.stack(z) for z in zip(*outs_s)]
    return (y_prompt, y_sample, k_p, v_p, ki_p, conv_p, ffn_p, k_s, v_s, ki_s, conv_s, ffn_s)
```

```python
import functools

import jax
import jax.numpy as jnp
from jax import lax
from jax.experimental import pallas as pl
from jax.experimental.pallas import tpu as pltpu

F32 = jnp.float32
BF16 = jnp.bfloat16
I32 = jnp.int32

D_MODEL = 1024
CHUNK = 64
CONV_DIM = 512
CONV_WIDTH = 31
N_HEADS = 8
HEAD_DIM = 64
N_KV_HEADS = 2
ATTN_DIM = N_HEADS * HEAD_DIM
KV_DIM = N_KV_HEADS * HEAD_DIM
ROT_DIM = HEAD_DIM // 4
ROPE_THETA = 500000.0
N_IDX_HEADS = 8
IDX_DIM = 32
TOPK_MAX = 256
D_FF = 2816
FFN_CONV_WIDTH = 3
EPS = 1e-6

LANES = 128
MXU_N = 256
VMEM_LIMIT = 52 * 1024 * 1024

COL_U, COL_UG, COL_Q, COL_K, COL_V, COL_QI = 0, 512, 1024, 1536, 1664, 1792
COL_KI = 2048
COL_WI = COL_KI + LANES
IN_PAD = COL_WI + LANES

INT_MIN = -2147483648
KEY_NEG_INF = -2139095041
IDX_BITS = 13


def _cparams(sem):
    return pltpu.CompilerParams(dimension_semantics=sem, vmem_limit_bytes=VMEM_LIMIT)


def _silu(x):
    return x * jax.nn.sigmoid(x)


def _mod_kernel(c_ref, w_ref, b_ref, o_ref):
    s = _silu(c_ref[...]).astype(BF16)
    o_ref[...] = jnp.dot(s, w_ref[...].astype(BF16), preferred_element_type=F32) + b_ref[...]


def _mod_call(c_all, w_ada, b_ada):
    nb, d = c_all.shape
    n = w_ada.shape[1]
    tn = 1536
    return pl.pallas_call(
        _mod_kernel,
        grid=(n // tn,),
        in_specs=[pl.BlockSpec((nb, d), lambda j: (0, 0)),
                  pl.BlockSpec((d, tn), lambda j: (0, j)),
                  pl.BlockSpec((1, tn), lambda j: (0, j))],
        out_specs=pl.BlockSpec((nb, tn), lambda j: (0, j)),
        out_shape=jax.ShapeDtypeStruct((nb, n), F32),
        compiler_params=_cparams(("arbitrary",)),
        name="mod",
    )(c_all, w_ada, b_ada.reshape(1, n))


def _rmsnorm_mod(x, g, sc, sh):
    ms = jnp.mean(x * x, axis=-1, keepdims=True)
    return (x * lax.rsqrt(ms + EPS) * g) * (1.0 + sc) + sh


def _inproj_kernel(x_ref, sc_ref, sh_ref, g_ref, w_ref, cos_ref, sa_ref, sb_ref,
                   glu_ref, q_ref, k_ref, v_ref, qi_ref, ki_ref, wi_ref):
    h = _rmsnorm_mod(x_ref[...], g_ref[...], sc_ref[...], sh_ref[...])
    z = jnp.dot(h.astype(BF16), w_ref[...], preferred_element_type=F32)
    glu_ref[...] = z[:, COL_U:COL_U + CONV_DIM] * jax.nn.sigmoid(z[:, COL_UG:COL_UG + CONV_DIM])
    cos, sa, sb = cos_ref[...], sa_ref[...], sb_ref[...]

    def rope(xs):
        return (xs * cos + pltpu.roll(xs, ROT_DIM // 2, axis=1) * sa
                + pltpu.roll(xs, LANES - ROT_DIM // 2, axis=1) * sb)

    for j in range(ATTN_DIM // LANES):
        c0 = COL_Q + LANES * j
        q_ref[:, LANES * j:LANES * (j + 1)] = (rope(z[:, c0:c0 + LANES]) * (HEAD_DIM ** -0.5)).astype(BF16)
    k_ref[...] = rope(z[:, COL_K:COL_K + KV_DIM])
    v_ref[...] = z[:, COL_V:COL_V + KV_DIM]
    qi_ref[...] = z[:, COL_QI:COL_QI + N_IDX_HEADS * IDX_DIM].astype(BF16)
    ki_ref[...] = z[:, COL_KI:COL_KI + IDX_DIM]
    wi_ref[...] = z[:, COL_WI:COL_WI + N_IDX_HEADS]


def _inproj_call(x2, sc, sh, g1, w_in_p, tabs, tm, name):
    r, d = x2.shape
    nt = r // tm
    if sc.ndim == 3:
        tpb = nt // sc.shape[0]
        mod_spec = pl.BlockSpec((None, 1, d), lambda i: (i // tpb, 0, 0))
    else:
        mod_spec = pl.BlockSpec((tm, d), lambda i: (i, 0))
    ntab = tabs[0].shape[0] // tm
    tab_spec = pl.BlockSpec((tm, LANES), lambda i: (i % ntab, 0))

    def rows(width):
        return pl.BlockSpec((tm, width), lambda i: (i, 0))

    out_shapes = (jax.ShapeDtypeStruct((r, CONV_DIM), F32), jax.ShapeDtypeStruct((r, ATTN_DIM), BF16),
                  jax.ShapeDtypeStruct((r, KV_DIM), F32), jax.ShapeDtypeStruct((r, KV_DIM), F32),
                  jax.ShapeDtypeStruct((r, N_IDX_HEADS * IDX_DIM), BF16),
                  jax.ShapeDtypeStruct((r, IDX_DIM), F32), jax.ShapeDtypeStruct((r, N_IDX_HEADS), F32))
    return pl.pallas_call(
        _inproj_kernel,
        grid=(nt,),
        in_specs=[rows(d), mod_spec, mod_spec,
                  pl.BlockSpec((1, d), lambda i: (0, 0)),
                  pl.BlockSpec((d, IN_PAD), lambda i: (0, 0)),
                  tab_spec, tab_spec, tab_spec],
        out_specs=(rows(CONV_DIM), rows(ATTN_DIM), rows(KV_DIM), rows(KV_DIM),
                   rows(N_IDX_HEADS * IDX_DIM), rows(IDX_DIM), rows(N_IDX_HEADS)),
        out_shape=out_shapes,
        compiler_params=_cparams(("arbitrary",)),
        name=name,
    )(x2, sc, sh, g1, w_in_p, *tabs)


CONV_HIST = CONV_WIDTH - 1
CONV_PAD = 32
CONV_RB = 64


def _conv_kernel(glu_ref, hist_ref, w_ref, b_ref, lg_ref, lb_ref, o_ref, ext_ref, y_ref, *, tt):
    t = pl.program_id(1)

    @pl.when(t == 0)
    def _():
        ext_ref[0:CONV_PAD - CONV_HIST, :] = jnp.zeros((CONV_PAD - CONV_HIST, CONV_DIM), F32)
        ext_ref[CONV_PAD - CONV_HIST:CONV_PAD, :] = hist_ref[...]

    @pl.when(t > 0)
    def _():
        ext_ref[0:CONV_PAD, :] = ext_ref[tt:tt + CONV_PAD, :]

    ext_ref[CONV_PAD:CONV_PAD + tt, :] = glu_ref[...]

    off = CONV_PAD - CONV_HIST
    rb = min(CONV_RB, tt)
    for r0 in range(0, tt, rb):
        for c0 in range(0, CONV_DIM, LANES):
            acc = ext_ref[off + r0:off + r0 + rb, c0:c0 + LANES] * w_ref[0:1, c0:c0 + LANES]
            for j in range(1, CONV_WIDTH):
                acc = acc + ext_ref[off + r0 + j:off + r0 + j + rb, c0:c0 + LANES] * w_ref[j:j + 1, c0:c0 + LANES]
            y_ref[r0:r0 + rb, c0:c0 + LANES] = acc + b_ref[:, c0:c0 + LANES]

    y = y_ref[...]
    mu = jnp.mean(y, axis=-1, keepdims=True)
    yc = y - mu
    var = jnp.mean(yc * yc, axis=-1, keepdims=True)
    o_ref[...] = _silu(yc * lax.rsqrt(var + EPS) * lg_ref[...] + lb_ref[...]).astype(BF16)


def _conv_call(glu3, hist, dw_w, dw_b, ln_g, ln_b, tt, name):
    b, t, c = glu3.shape
    vec = pl.BlockSpec((1, c), lambda i, j: (0, 0))
    return pl.pallas_call(
        functools.partial(_conv_kernel, tt=tt),
        grid=(b, t // tt),
        in_specs=[pl.BlockSpec((None, tt, c), lambda i, j: (i, j, 0)),
                  pl.BlockSpec((None, CONV_HIST, c), lambda i, j: (i, 0, 0)),
                  pl.BlockSpec((CONV_WIDTH, c), lambda i, j: (0, 0)),
                  vec, vec, vec],
        out_specs=pl.BlockSpec((None, tt, c), lambda i, j: (i, j, 0)),
        out_shape=jax.ShapeDtypeStruct((b, t, c), BF16),
        scratch_shapes=[pltpu.VMEM((tt + CONV_PAD, c), F32), pltpu.VMEM((tt, c), F32)],
        compiler_params=_cparams(("arbitrary", "arbitrary")),
        name=name,
    )(glu3, hist, dw_w, dw_b.reshape(1, c), ln_g.reshape(1, c), ln_b.reshape(1, c))


def _attn_kernel(q_ref, qi_ref, wi_ref, k_ref, v_ref, kit_ref, o_ref, key_ref, bias_ref,
                 *, tq, n_keys, t_new, past, k_sel):
    i = pl.program_id(1)
    row = lax.broadcasted_iota(I32, (tq, 1), 0) + i * tq
    limit = past + jnp.minimum((row // CHUNK + 1) * CHUNK, t_new)
    col = lax.broadcasted_iota(I32, (tq, n_keys), 1)
    adm = col < limit

    kit = kit_ref[...]
    wi = wi_ref[...]
    score = jnp.zeros((tq, n_keys), F32)
    for h in range(N_IDX_HEADS):
        d = jnp.dot(qi_ref[:, IDX_DIM * h:IDX_DIM * (h + 1)], kit, preferred_element_type=F32)
        score = score + wi[:, h:h + 1] * jnp.maximum(d, 0.0)
    bits = pltpu.bitcast(jnp.where(adm, score, -jnp.inf), I32)
    key_ref[...] = bits ^ ((bits >> 31) & 0x7FFFFFFF)

    def bit_step(it, r):
        cand = r + jnp.left_shift(jnp.int32(1), 31 - it)
        cnt = jnp.sum(jnp.where(key_ref[...] >= cand, 1.0, 0.0), axis=1, keepdims=True)
        return jnp.where(cnt >= k_sel, cand, r)

    thr = lax.fori_loop(0, 32, bit_step, jnp.full((tq, 1), INT_MIN, I32))

    keys = key_ref[...]
    ge = keys >= thr
    cnt_ge = jnp.sum(jnp.where(ge, 1.0, 0.0), axis=1, keepdims=True)
    bias_ref[...] = jnp.where(ge & adm, 0.0, -jnp.inf)
    split_tie = (cnt_ge > k_sel) & (thr != KEY_NEG_INF)
    any_split = jnp.max(jnp.where(split_tie, 1.0, 0.0)) > 0.0

    @pl.when(any_split)
    def _():
        gt = keys > thr
        eq = keys == thr
        need = k_sel - jnp.sum(jnp.where(gt, 1.0, 0.0), axis=1, keepdims=True)

        def idx_step(it, j):
            cand = j + jnp.left_shift(jnp.int32(1), IDX_BITS - 1 - it)
            below = jnp.sum(jnp.where(eq & (col < cand), 1.0, 0.0), axis=1, keepdims=True)
            return jnp.where(below < need, cand, j)

        last = lax.fori_loop(0, IDX_BITS, idx_step, jnp.zeros((tq, 1), I32))
        sel = (gt | (eq & (col <= last))) & adm
        bias_ref[...] = jnp.where(sel, 0.0, -jnp.inf)

    bias = bias_ref[...]
    rep = N_HEADS // N_KV_HEADS
    for h in range(N_HEADS):
        g = h // rep
        logits = lax.dot_general(q_ref[:, HEAD_DIM * h:HEAD_DIM * (h + 1)],
                                 k_ref[:, HEAD_DIM * g:HEAD_DIM * (g + 1)],
                                 (((1,), (1,)), ((), ())), preferred_element_type=F32) + bias
        m = jnp.max(logits, axis=1, keepdims=True)
        p = jnp.exp(logits - m)
        s = jnp.sum(p, axis=1, keepdims=True)
        o = jnp.dot(p.astype(BF16), v_ref[:, HEAD_DIM * g:HEAD_DIM * (g + 1)], preferred_element_type=F32)
        o_ref[:, HEAD_DIM * h:HEAD_DIM * (h + 1)] = (o / s).astype(BF16)


def _attn_call(q3, qi3, wi3, k_all, v_all, kit_all, tq, t_new, past, k_sel, name):
    b, t, _ = q3.shape
    n_keys = k_all.shape[1]
    assert n_keys % LANES == 0 and n_keys <= (1 << IDX_BITS) and t % tq == 0
    kern = functools.partial(_attn_kernel, tq=tq, n_keys=n_keys, t_new=t_new, past=past, k_sel=float(k_sel))
    return pl.pallas_call(
        kern,
        grid=(b, t // tq),
        in_specs=[pl.BlockSpec((None, tq, ATTN_DIM), lambda i, j: (i, j, 0)),
                  pl.BlockSpec((None, tq, N_IDX_HEADS * IDX_DIM), lambda i, j: (i, j, 0)),
                  pl.BlockSpec((None, tq, N_IDX_HEADS), lambda i, j: (i, j, 0)),
                  pl.BlockSpec((None, n_keys, KV_DIM), lambda i, j: (i, 0, 0)),
                  pl.BlockSpec((None, n_keys, KV_DIM), lambda i, j: (i, 0, 0)),
                  pl.BlockSpec((None, IDX_DIM, n_keys), lambda i, j: (i, 0, 0))],
        out_specs=pl.BlockSpec((None, tq, ATTN_DIM), lambda i, j: (i, j, 0)),
        out_shape=jax.ShapeDtypeStruct((b, t, ATTN_DIM), BF16),
        scratch_shapes=[pltpu.VMEM((tq, n_keys), I32), pltpu.VMEM((tq, n_keys), F32)],
        compiler_params=_cparams(("arbitrary", "arbitrary")),
        name=name,
    )(q3, qi3, wi3, k_all, v_all, kit_all)


FF_CHUNKS = ((0, 1024), (1024, 1024), (2048, 768))


def _post_kernel(x_ref, conv_ref, attn_ref, gt1_ref, sc2_ref, sh2_ref, gt2_ref, g2_ref, gf_ref,
                 wout_ref, wup_ref, wdn_ref, fw_ref, fb_ref, hist_ref,
                 y_ref, newffn_ref, *scratch, tm, t_len, tiles_per_batch):
    nseg = max(1, tm // t_len)
    seg = tm // nseg
    i = pl.program_id(0)
    if nseg == 1:
        carry_ref = scratch[0]

        @pl.when(i % tiles_per_batch == 0)
        def _():
            carry_ref[...] = hist_ref[0]

    mix = (jnp.dot(conv_ref[...], wout_ref[0:CONV_DIM, :], preferred_element_type=F32)
           + jnp.dot(attn_ref[...], wout_ref[CONV_DIM:CONV_DIM + ATTN_DIM, :], preferred_element_type=F32))
    x1 = x_ref[...] + gt1_ref[...] * mix
    h2 = _rmsnorm_mod(x1, g2_ref[...], sc2_ref[...], sh2_ref[...]).astype(BF16)

    row = lax.broadcasted_iota(I32, (seg, 1), 0)

    def causal3(u, col0, width):
        w0 = fw_ref[0:1, col0:col0 + width]
        w1 = fw_ref[1:2, col0:col0 + width]
        w2 = fw_ref[2:3, col0:col0 + width]
        outs = []
        for s in range(nseg):
            us = u[s * seg:(s + 1) * seg, :]
            if nseg == 1:
                h0 = carry_ref[0:1, col0:col0 + width]
                h1 = carry_ref[1:2, col0:col0 + width]
            else:
                h0 = hist_ref[s, 0:1, col0:col0 + width]
                h1 = hist_ref[s, 1:2, col0:col0 + width]
            p1 = jnp.where(row == 0, h1, pltpu.roll(us, 1, axis=0))
            p2 = jnp.where(row == 0, h0, jnp.where(row == 1, h1, pltpu.roll(us, 2, axis=0)))
            outs.append(us * w2 + p1 * w1 + p2 * w0 + fb_ref[:, col0:col0 + width])
            if nseg == 1:
                carry_ref[:, col0:col0 + width] = us[seg - 2:seg, :]
            else:
                newffn_ref[s, :, col0:col0 + width] = us[seg - 2:seg, :]
        return outs[0] if nseg == 1 else jnp.concatenate(outs, axis=0)

    acc = jnp.zeros((tm, D_MODEL), F32)
    for c0, cw in FF_CHUNKS:
        ua = jnp.dot(h2, wup_ref[:, c0:c0 + cw], preferred_element_type=F32)
        ug = jnp.dot(h2, wup_ref[:, D_FF + c0:D_FF + c0 + cw], preferred_element_type=F32)
        a = causal3(ua, c0, cw)
        g = causal3(ug, D_FF + c0, cw)
        acc = acc + jnp.dot((a * _silu(g)).astype(BF16), wdn_ref[c0:c0 + cw, :], preferred_element_type=F32)

    if nseg == 1:
        @pl.when(i % tiles_per_batch == tiles_per_batch - 1)
        def _():
            newffn_ref[0] = carry_ref[...]

    x2 = x1 + gt2_ref[...] * acc
    ms = jnp.mean(x2 * x2, axis=-1, keepdims=True)
    y_ref[...] = x2 * lax.rsqrt(ms + EPS) * gf_ref[...]


def _post_call(x2d, conv2d, attn2d, gt1, sc2, sh2, gt2, g2, gf, wout, wup, wdn, fw, fb, hist, tm, t_len, name):
    r, d = x2d.shape
    nt = r // tm
    nb = hist.shape[0]
    nseg = max(1, tm // t_len)
    tpb = max(1, t_len // tm)
    assert nt * nseg == nb * tpb and t_len >= FFN_CONV_WIDTH - 1
    if gt1.ndim == 3:
        mod_spec = pl.BlockSpec((None, 1, d), lambda i: (i // tpb, 0, 0))
    else:
        mod_spec = pl.BlockSpec((tm, d), lambda i: (i, 0))

    def rows(width):
        return pl.BlockSpec((tm, width), lambda i: (i, 0))

    def const(shape):
        return pl.BlockSpec(shape, lambda i: (0,) * len(shape), pipeline_mode=pl.Buffered(1))

    state_spec = pl.BlockSpec((nseg, FFN_CONV_WIDTH - 1, 2 * D_FF), lambda i: (i * nseg // tpb, 0, 0))
    kern = functools.partial(_post_kernel, tm=tm, t_len=t_len, tiles_per_batch=tpb)
    scratch = [pltpu.VMEM((FFN_CONV_WIDTH - 1, 2 * D_FF), F32)] if nseg == 1 else []
    return pl.pallas_call(
        kern,
        grid=(nt,),
        in_specs=[rows(d), rows(CONV_DIM), rows(ATTN_DIM), mod_spec, mod_spec, mod_spec, mod_spec,
                  const((1, d)), const((1, d)),
                  const((CONV_DIM + ATTN_DIM, d)), const((d, 2 * D_FF)), const((D_FF, d)),
                  const((FFN_CONV_WIDTH, 2 * D_FF)), const((1, 2 * D_FF)),
                  state_spec],
        out_specs=(rows(d), state_spec),
        out_shape=(jax.ShapeDtypeStruct((r, d), F32),
                   jax.ShapeDtypeStruct((nb, FFN_CONV_WIDTH - 1, 2 * D_FF), F32)),
        scratch_shapes=scratch,
        compiler_params=_cparams(("arbitrary",)),
        name=name,
    )(x2d, conv2d, attn2d, gt1, sc2, sh2, gt2, g2, gf, wout, wup, wdn, fw, fb, hist)


def _rope_tables(pos):
    half = ROT_DIM // 2
    inv = 1.0 / (ROPE_THETA ** (jnp.arange(0, ROT_DIM, 2, dtype=F32) / ROT_DIM))
    ang = pos.astype(F32)[:, None] * inv[None, :]
    cos, sin = jnp.cos(ang), jnp.sin(ang)
    t = pos.shape[0]
    rest1 = jnp.ones((t, HEAD_DIM - ROT_DIM), F32)
    rest0 = jnp.zeros((t, HEAD_DIM - ROT_DIM), F32)
    z = jnp.zeros((t, half), F32)
    c64 = jnp.concatenate([cos, cos, rest1], axis=1)
    a64 = jnp.concatenate([z, sin, rest0], axis=1)
    b64 = jnp.concatenate([-sin, z, rest0], axis=1)
    return tuple(jnp.concatenate([m, m], axis=1) for m in (c64, a64, b64))


def _pad_w_in(w_in):
    d = w_in.shape[0]
    return jnp.concatenate(
        [w_in[:, :COL_KI + IDX_DIM], jnp.zeros((d, LANES - IDX_DIM), F32),
         w_in[:, COL_KI + IDX_DIM:], jnp.zeros((d, LANES - N_IDX_HEADS), F32)], axis=1).astype(BF16)


def _layer_group(x, mods, pos, conv_hist, ffn_hist, past, w, *, per_row_mod, tm_in, tt_conv, tq, tm_post, tag):
    (g1, w_in_p, dw_w, dw_b, ln_g, ln_b, w_out, g2, w_up, fdw_w, fdw_b, w_down, gf) = w
    b, t, d = x.shape
    r = b * t
    x2d = x.reshape(r, d)
    if per_row_mod:
        sh1, sc1, gt1, sh2, sc2, gt2 = [jnp.repeat(m, t, axis=0) for m in mods]
        tabs = tuple(jnp.tile(m, (b, 1)) for m in _rope_tables(pos))
    else:
        sh1, sc1, gt1, sh2, sc2, gt2 = [m[:, None, :] for m in mods]
        tabs = _rope_tables(pos)

    glu, q, k, v, qi, ki, wi = _inproj_call(x2d, sc1, sh1, g1, w_in_p, tabs, tm_in, "inproj_" + tag)
    glu3 = glu.reshape(b, t, CONV_DIM)
    k3, v3, ki3 = k.reshape(b, t, KV_DIM), v.reshape(b, t, KV_DIM), ki.reshape(b, t, IDX_DIM)

    conv_out = _conv_call(glu3, conv_hist, dw_w, dw_b, ln_g, ln_b, tt_conv, "conv_" + tag)
    new_conv = glu3[:, t - CONV_HIST:, :]

    if past is None:
        past_len = 0
        k_all, v_all, ki_all = k3.astype(BF16), v3.astype(BF16), ki3.astype(BF16)
    else:
        ck, cv, cki = past
        past_len = ck.shape[1]
        k_all = jnp.concatenate([ck.reshape(b, past_len, KV_DIM).astype(BF16), k3.astype(BF16)], axis=1)
        v_all = jnp.concatenate([cv.reshape(b, past_len, KV_DIM).astype(BF16), v3.astype(BF16)], axis=1)
        ki_all = jnp.concatenate([cki.astype(BF16), ki3.astype(BF16)], axis=1)
    n_keys = past_len + t
    k_sel = min(TOPK_MAX, n_keys // 4)
    pad = (-n_keys) % LANES
    if pad:
        k_all, v_all, ki_all = [jnp.pad(a, ((0, 0), (0, pad), (0, 0))) for a in (k_all, v_all, ki_all)]
    kit_all = jnp.swapaxes(ki_all, 1, 2)
    attn_out = _attn_call(q.reshape(b, t, ATTN_DIM), qi.reshape(b, t, -1), wi.reshape(b, t, -1),
                          k_all, v_all, kit_all, tq, t, past_len, k_sel, "attn_" + tag)

    y, new_ffn = _post_call(x2d, conv_out.reshape(r, CONV_DIM), attn_out.reshape(r, ATTN_DIM),
                            gt1, sc2, sh2, gt2, g2, gf, w_out, w_up, w_down, fdw_w, fdw_b, ffn_hist,
                            tm_post, t, "post_" + tag)
    return (y.reshape(b, t, d), k3.reshape(b, t, N_KV_HEADS, HEAD_DIM), v3.reshape(b, t, N_KV_HEADS, HEAD_DIM),
            ki3, new_conv, new_ffn)


def kernel(x_prompt, x_sample, cache_k, cache_v, cache_kidx, state_conv, state_ffn_conv, c_prompt, c_sample,
           w_ada, b_ada, norm1_g, w_in, conv_dw_w, conv_dw_b, conv_ln_g, conv_ln_b, w_out, norm2_g,
           w_up, ffn_dw_w, ffn_dw_b, w_down, final_norm_g):
    depth = w_ada.shape[0]
    assert depth == 1, "the final norm is fused into the single layer's last kernel"
    bp, sp, d = x_prompt.shape
    bs, ts, _ = x_sample.shape
    past_len = cache_k.shape[2]
    l = 0
    mod = _mod_call(jnp.concatenate([c_prompt, c_sample], axis=0), w_ada[l], b_ada[l])
    mods_p = jnp.split(mod[:bp], 6, axis=-1)
    mods_s = jnp.split(mod[bp:], 6, axis=-1)
    w = (norm1_g[l].reshape(1, d), _pad_w_in(w_in[l]), conv_dw_w[l], conv_dw_b[l], conv_ln_g[l], conv_ln_b[l],
         w_out[l].astype(BF16), norm2_g[l].reshape(1, d), w_up[l].astype(BF16), ffn_dw_w[l],
         ffn_dw_b[l].reshape(1, -1), w_down[l].astype(BF16), final_norm_g.reshape(1, d))

    conv0 = jnp.zeros((bp, CONV_HIST, CONV_DIM), F32)
    ffn0 = jnp.zeros((bp, FFN_CONV_WIDTH - 1, 2 * D_FF), F32)
    out_p = _layer_group(x_prompt, mods_p, jnp.arange(sp, dtype=I32), conv0, ffn0, None, w,
                         per_row_mod=False, tm_in=512, tt_conv=128, tq=128, tm_post=256, tag="p")
    out_s = _layer_group(x_sample, mods_s, past_len + jnp.arange(ts, dtype=I32), state_conv[l], state_ffn_conv[l],
                         (cache_k[l], cache_v[l], cache_kidx[l]), w,
                         per_row_mod=True, tm_in=bs * ts, tt_conv=ts, tq=ts, tm_post=bs * ts, tag="s")
    y_p, k_p, v_p, ki_p, conv_p, ffn_p = out_p
    y_s, k_s, v_s, ki_s, conv_s, ffn_s = out_s
    st = lambda a: a[None]
    return (y_p, y_s, st(k_p), st(v_p), st(ki_p), st(conv_p), st(ffn_p),
            st(k_s), st(v_s), st(ki_s), st(conv_s), st(ffn_s))
```

```python
import functools

import jax
import jax.numpy as jnp
from jax import lax
from jax.experimental import pallas as pl
from jax.experimental.pallas import tpu as pltpu

F32 = jnp.float32
BF16 = jnp.bfloat16
I32 = jnp.int32

D_MODEL = 1024
CHUNK = 64
CONV_DIM = 512
CONV_WIDTH = 31
N_HEADS = 8
HEAD_DIM = 64
N_KV_HEADS = 2
ATTN_DIM = N_HEADS * HEAD_DIM
KV_DIM = N_KV_HEADS * HEAD_DIM
ROT_DIM = HEAD_DIM // 4
ROPE_THETA = 500000.0
N_IDX_HEADS = 8
IDX_DIM = 32
TOPK_MAX = 256
D_FF = 2816
FFN_CONV_WIDTH = 3
EPS = 1e-6

LANES = 128
MXU_N = 256
VMEM_LIMIT = 52 * 1024 * 1024

COL_U, COL_UG, COL_Q, COL_K, COL_V, COL_QI = 0, 512, 1024, 1536, 1664, 1792
COL_KI = 2048
COL_WI = COL_KI + LANES
IN_PAD = COL_WI + LANES

INT_MIN = -2147483648
KEY_NEG_INF = -2139095041
IDX_BITS = 13


def _cparams(sem):
    return pltpu.CompilerParams(dimension_semantics=sem, vmem_limit_bytes=VMEM_LIMIT)


def _silu(x):
    return x * jax.nn.sigmoid(x)


def _mod_kernel(c_ref, w_ref, b_ref, o_ref):
    s = _silu(c_ref[...]).astype(BF16)
    o_ref[...] = jnp.dot(s, w_ref[...].astype(BF16), preferred_element_type=F32) + b_ref[...]


def _mod_call(c_all, w_ada, b_ada):
    nb, d = c_all.shape
    n = w_ada.shape[1]
    tn = 1536
    return pl.pallas_call(
        _mod_kernel,
        grid=(n // tn,),
        in_specs=[pl.BlockSpec((nb, d), lambda j: (0, 0)),
                  pl.BlockSpec((d, tn), lambda j: (0, j)),
                  pl.BlockSpec((1, tn), lambda j: (0, j))],
        out_specs=pl.BlockSpec((nb, tn), lambda j: (0, j)),
        out_shape=jax.ShapeDtypeStruct((nb, n), F32),
        compiler_params=_cparams(("arbitrary",)),
        name="mod",
    )(c_all, w_ada, b_ada.reshape(1, n))


def _rmsnorm_mod(x, g, sc, sh):
    ms = jnp.mean(x * x, axis=-1, keepdims=True)
    return (x * lax.rsqrt(ms + EPS) * g) * (1.0 + sc) + sh


def _inproj_kernel(x_ref, sc_ref, sh_ref, g_ref, w_ref, cos_ref, sa_ref, sb_ref,
                   glu_ref, q_ref, k_ref, v_ref, qi_ref, ki_ref, wi_ref):
    h = _rmsnorm_mod(x_ref[...], g_ref[...], sc_ref[...], sh_ref[...])
    z = jnp.dot(h.astype(BF16), w_ref[...], preferred_element_type=F32)
    glu_ref[...] = z[:, COL_U:COL_U + CONV_DIM] * jax.nn.sigmoid(z[:, COL_UG:COL_UG + CONV_DIM])
    cos, sa, sb = cos_ref[...], sa_ref[...], sb_ref[...]

    def rope(xs):
        return (xs * cos + pltpu.roll(xs, ROT_DIM // 2, axis=1) * sa
                + pltpu.roll(xs, LANES - ROT_DIM // 2, axis=1) * sb)

    for j in range(ATTN_DIM // LANES):
        c0 = COL_Q + LANES * j
        q_ref[:, LANES * j:LANES * (j + 1)] = (rope(z[:, c0:c0 + LANES]) * (HEAD_DIM ** -0.5)).astype(BF16)
    k_ref[...] = rope(z[:, COL_K:COL_K + KV_DIM])
    v_ref[...] = z[:, COL_V:COL_V + KV_DIM]
    qi_ref[...] = z[:, COL_QI:COL_QI + N_IDX_HEADS * IDX_DIM].astype(BF16)
    ki_ref[...] = z[:, COL_KI:COL_KI + IDX_DIM]
    wi_ref[...] = z[:, COL_WI:COL_WI + N_IDX_HEADS]


def _inproj_call(x2, sc, sh, g1, w_in_p, tabs, tm, name):
    r, d = x2.shape
    nt = r // tm
    if sc.ndim == 3:
        tpb = nt // sc.shape[0]
        mod_spec = pl.BlockSpec((None, 1, d), lambda i: (i // tpb, 0, 0))
    else:
        mod_spec = pl.BlockSpec((tm, d), lambda i: (i, 0))
    ntab = tabs[0].shape[0] // tm
    tab_spec = pl.BlockSpec((tm, LANES), lambda i: (i % ntab, 0))

    def rows(width):
        return pl.BlockSpec((tm, width), lambda i: (i, 0))

    out_shapes = (jax.ShapeDtypeStruct((r, CONV_DIM), F32), jax.ShapeDtypeStruct((r, ATTN_DIM), BF16),
                  jax.ShapeDtypeStruct((r, KV_DIM), F32), jax.ShapeDtypeStruct((r, KV_DIM), F32),
                  jax.ShapeDtypeStruct((r, N_IDX_HEADS * IDX_DIM), BF16),
                  jax.ShapeDtypeStruct((r, IDX_DIM), F32), jax.ShapeDtypeStruct((r, N_IDX_HEADS), F32))
    return pl.pallas_call(
        _inproj_kernel,
        grid=(nt,),
        in_specs=[rows(d), mod_spec, mod_spec,
                  pl.BlockSpec((1, d), lambda i: (0, 0)),
                  pl.BlockSpec((d, IN_PAD), lambda i: (0, 0)),
                  tab_spec, tab_spec, tab_spec],
        out_specs=(rows(CONV_DIM), rows(ATTN_DIM), rows(KV_DIM), rows(KV_DIM),
                   rows(N_IDX_HEADS * IDX_DIM), rows(IDX_DIM), rows(N_IDX_HEADS)),
        out_shape=out_shapes,
        compiler_params=_cparams(("arbitrary",)),
        name=name,
    )(x2, sc, sh, g1, w_in_p, *tabs)


CONV_HIST = CONV_WIDTH - 1
CONV_PAD = 32
CONV_RB = 64


def _conv_kernel(glu_ref, hist_ref, w_ref, b_ref, lg_ref, lb_ref, o_ref, ext_ref, y_ref, *, tt):
    t = pl.program_id(1)

    @pl.when(t == 0)
    def _():
        ext_ref[0:CONV_PAD - CONV_HIST, :] = jnp.zeros((CONV_PAD - CONV_HIST, CONV_DIM), F32)
        ext_ref[CONV_PAD - CONV_HIST:CONV_PAD, :] = hist_ref[...]

    @pl.when(t > 0)
    def _():
        ext_ref[0:CONV_PAD, :] = ext_ref[tt:tt + CONV_PAD, :]

    ext_ref[CONV_PAD:CONV_PAD + tt, :] = glu_ref[...]

    off = CONV_PAD - CONV_HIST
    rb = min(CONV_RB, tt)
    for r0 in range(0, tt, rb):
        for c0 in range(0, CONV_DIM, LANES):
            acc = ext_ref[off + r0:off + r0 + rb, c0:c0 + LANES] * w_ref[0:1, c0:c0 + LANES]
            for j in range(1, CONV_WIDTH):
                acc = acc + ext_ref[off + r0 + j:off + r0 + j + rb, c0:c0 + LANES] * w_ref[j:j + 1, c0:c0 + LANES]
            y_ref[r0:r0 + rb, c0:c0 + LANES] = acc + b_ref[:, c0:c0 + LANES]

    y = y_ref[...]
    mu = jnp.mean(y, axis=-1, keepdims=True)
    yc = y - mu
    var = jnp.mean(yc * yc, axis=-1, keepdims=True)
    o_ref[...] = _silu(yc * lax.rsqrt(var + EPS) * lg_ref[...] + lb_ref[...]).astype(BF16)


def _conv_call(glu3, hist, dw_w, dw_b, ln_g, ln_b, tt, name):
    b, t, c = glu3.shape
    vec = pl.BlockSpec((1, c), lambda i, j: (0, 0))
    return pl.pallas_call(
        functools.partial(_conv_kernel, tt=tt),
        grid=(b, t // tt),
        in_specs=[pl.BlockSpec((None, tt, c), lambda i, j: (i, j, 0)),
                  pl.BlockSpec((None, CONV_HIST, c), lambda i, j: (i, 0, 0)),
                  pl.BlockSpec((CONV_WIDTH, c), lambda i, j: (0, 0)),
                  vec, vec, vec],
        out_specs=pl.BlockSpec((None, tt, c), lambda i, j: (i, j, 0)),
        out_shape=jax.ShapeDtypeStruct((b, t, c), BF16),
        scratch_shapes=[pltpu.VMEM((tt + CONV_PAD, c), F32), pltpu.VMEM((tt, c), F32)],
        compiler_params=_cparams(("arbitrary", "arbitrary")),
        name=name,
    )(glu3, hist, dw_w, dw_b.reshape(1, c), ln_g.reshape(1, c), ln_b.reshape(1, c))


CNT_ROWS = 64


def _attn_kernel(q_ref, qi_ref, wi_ref, k_ref, v_ref, kit_ref, o_ref, key_ref, bias_ref,
                 *, tq, n_keys, q0, t_new, past, k_sel):
    j = pl.program_id(1)
    row = lax.broadcasted_iota(I32, (tq, 1), 0) + (q0 + j * tq)
    limit = past + jnp.minimum((row // CHUNK + 1) * CHUNK, t_new)
    col = lax.broadcasted_iota(I32, (tq, n_keys), 1)
    adm = col < limit

    kit = kit_ref[...]
    wi = wi_ref[...]
    score = jnp.zeros((tq, n_keys), F32)
    for h in range(N_IDX_HEADS):
        d = jnp.dot(qi_ref[:, IDX_DIM * h:IDX_DIM * (h + 1)], kit, preferred_element_type=F32)
        score = score + wi[:, h:h + 1] * jnp.maximum(d, 0.0)
    bits = pltpu.bitcast(jnp.where(adm, score, -jnp.inf), I32)
    key_ref[...] = bits ^ ((bits >> 31) & 0x7FFFFFFF)

    rg = min(CNT_ROWS, tq)

    def count_ge(cand):
        outs = []
        for r0 in range(0, tq, rg):
            c = jnp.broadcast_to(cand[r0:r0 + rg, :], (rg, LANES))
            acc = jnp.zeros((rg, LANES), F32)
            for c0 in range(0, n_keys, LANES):
                acc = acc + jnp.where(key_ref[r0:r0 + rg, c0:c0 + LANES] >= c, 1.0, 0.0)
            outs.append(jnp.sum(acc, axis=1, keepdims=True))
        return outs[0] if len(outs) == 1 else jnp.concatenate(outs, axis=0)

    def bit_step(it, r):
        cand = r + jnp.left_shift(jnp.int32(1), 31 - it)
        return jnp.where(count_ge(cand) >= k_sel, cand, r)

    thr = lax.fori_loop(0, 32, bit_step, jnp.full((tq, 1), INT_MIN, I32))

    keys = key_ref[...]
    ge = keys >= thr
    cnt_ge = jnp.sum(jnp.where(ge, 1.0, 0.0), axis=1, keepdims=True)
    bias_ref[...] = jnp.where(ge & adm, 0.0, -jnp.inf)
    split_tie = (cnt_ge > k_sel) & (thr != KEY_NEG_INF)
    any_split = jnp.max(jnp.where(split_tie, 1.0, 0.0)) > 0.0

    @pl.when(any_split)
    def _():
        gt = keys > thr
        eq = keys == thr
        need = k_sel - jnp.sum(jnp.where(gt, 1.0, 0.0), axis=1, keepdims=True)

        def idx_step(it, last):
            cand = last + jnp.left_shift(jnp.int32(1), IDX_BITS - 1 - it)
            below = jnp.sum(jnp.where(eq & (col < cand), 1.0, 0.0), axis=1, keepdims=True)
            return jnp.where(below < need, cand, last)

        last = lax.fori_loop(0, IDX_BITS, idx_step, jnp.zeros((tq, 1), I32))
        sel = (gt | (eq & (col <= last))) & adm
        bias_ref[...] = jnp.where(sel, 0.0, -jnp.inf)

    bias = bias_ref[...]
    rep = N_HEADS // N_KV_HEADS
    for g in range(N_KV_HEADS):
        qg = jnp.concatenate([q_ref[:, HEAD_DIM * (rep * g + r):HEAD_DIM * (rep * g + r + 1)] for r in range(rep)],
                             axis=0)
        logits = lax.dot_general(qg, k_ref[:, HEAD_DIM * g:HEAD_DIM * (g + 1)],
                                 (((1,), (1,)), ((), ())), preferred_element_type=F32)
        ps, ss = [], []
        for r in range(rep):
            lg = logits[r * tq:(r + 1) * tq, :] + bias
            p = jnp.exp(lg - jnp.max(lg, axis=1, keepdims=True))
            ss.append(jnp.sum(p, axis=1, keepdims=True))
            ps.append(p.astype(BF16))
        o = jnp.dot(jnp.concatenate(ps, axis=0), v_ref[:, HEAD_DIM * g:HEAD_DIM * (g + 1)],
                    preferred_element_type=F32)
        for r in range(rep):
            h = rep * g + r
            o_ref[:, HEAD_DIM * h:HEAD_DIM * (h + 1)] = (o[r * tq:(r + 1) * tq, :] / ss[r]).astype(BF16)


def _attn_call(q3, qi3, wi3, k_all, v_all, kit_all, tq, q_blk0, n_q_blk, n_keys, t_new, past, k_sel, name):
    b = q3.shape[0]
    assert n_keys % LANES == 0 and n_keys <= (1 << IDX_BITS) and n_keys <= k_all.shape[1]
    kern = functools.partial(_attn_kernel, tq=tq, n_keys=n_keys, q0=q_blk0 * tq, t_new=t_new, past=past,
                             k_sel=float(k_sel))
    return pl.pallas_call(
        kern,
        grid=(b, n_q_blk),
        in_specs=[pl.BlockSpec((None, tq, ATTN_DIM), lambda i, j: (i, q_blk0 + j, 0)),
                  pl.BlockSpec((None, tq, N_IDX_HEADS * IDX_DIM), lambda i, j: (i, q_blk0 + j, 0)),
                  pl.BlockSpec((None, tq, N_IDX_HEADS), lambda i, j: (i, q_blk0 + j, 0)),
                  pl.BlockSpec((None, n_keys, KV_DIM), lambda i, j: (i, 0, 0)),
                  pl.BlockSpec((None, n_keys, KV_DIM), lambda i, j: (i, 0, 0)),
                  pl.BlockSpec((None, IDX_DIM, n_keys), lambda i, j: (i, 0, 0))],
        out_specs=pl.BlockSpec((None, tq, ATTN_DIM), lambda i, j: (i, j, 0)),
        out_shape=jax.ShapeDtypeStruct((b, n_q_blk * tq, ATTN_DIM), BF16),
        scratch_shapes=[pltpu.VMEM((tq, n_keys), I32), pltpu.VMEM((tq, n_keys), F32)],
        compiler_params=_cparams(("arbitrary", "arbitrary")),
        name=name,
    )(q3, qi3, wi3, k_all, v_all, kit_all)


FF_CHUNKS = ((0, 1024), (1024, 1024), (2048, 768))


def _post_kernel(x_ref, conv_ref, attn_ref, gt1_ref, sc2_ref, sh2_ref, gt2_ref, g2_ref, gf_ref,
                 wout_ref, wup_ref, wdn_ref, fw_ref, fb_ref, hist_ref,
                 y_ref, newffn_ref, *scratch, tm, t_len, tiles_per_batch):
    nseg = max(1, tm // t_len)
    seg = tm // nseg
    i = pl.program_id(0)
    if nseg == 1:
        carry_ref = scratch[0]

        @pl.when(i % tiles_per_batch == 0)
        def _():
            carry_ref[...] = hist_ref[0]

    mix = (jnp.dot(conv_ref[...], wout_ref[0:CONV_DIM, :], preferred_element_type=F32)
           + jnp.dot(attn_ref[...], wout_ref[CONV_DIM:CONV_DIM + ATTN_DIM, :], preferred_element_type=F32))
    x1 = x_ref[...] + gt1_ref[...] * mix
    h2 = _rmsnorm_mod(x1, g2_ref[...], sc2_ref[...], sh2_ref[...]).astype(BF16)

    row = lax.broadcasted_iota(I32, (seg, 1), 0)

    def causal3(u, col0, width):
        w0 = fw_ref[0:1, col0:col0 + width]
        w1 = fw_ref[1:2, col0:col0 + width]
        w2 = fw_ref[2:3, col0:col0 + width]
        outs = []
        for s in range(nseg):
            us = u[s * seg:(s + 1) * seg, :]
            if nseg == 1:
                h0 = carry_ref[0:1, col0:col0 + width]
                h1 = carry_ref[1:2, col0:col0 + width]
            else:
                h0 = hist_ref[s, 0:1, col0:col0 + width]
                h1 = hist_ref[s, 1:2, col0:col0 + width]
            p1 = jnp.where(row == 0, h1, pltpu.roll(us, 1, axis=0))
            p2 = jnp.where(row == 0, h0, jnp.where(row == 1, h1, pltpu.roll(us, 2, axis=0)))
            outs.append(us * w2 + p1 * w1 + p2 * w0 + fb_ref[:, col0:col0 + width])
            if nseg == 1:
                carry_ref[:, col0:col0 + width] = us[seg - 2:seg, :]
            else:
                newffn_ref[s, :, col0:col0 + width] = us[seg - 2:seg, :]
        return outs[0] if nseg == 1 else jnp.concatenate(outs, axis=0)

    acc = jnp.zeros((tm, D_MODEL), F32)
    for c0, cw in FF_CHUNKS:
        ua = jnp.dot(h2, wup_ref[:, c0:c0 + cw], preferred_element_type=F32)
        ug = jnp.dot(h2, wup_ref[:, D_FF + c0:D_FF + c0 + cw], preferred_element_type=F32)
        a = causal3(ua, c0, cw)
        g = causal3(ug, D_FF + c0, cw)
        acc = acc + jnp.dot((a * _silu(g)).astype(BF16), wdn_ref[c0:c0 + cw, :], preferred_element_type=F32)

    if nseg == 1:
        @pl.when(i % tiles_per_batch == tiles_per_batch - 1)
        def _():
            newffn_ref[0] = carry_ref[...]

    x2 = x1 + gt2_ref[...] * acc
    ms = jnp.mean(x2 * x2, axis=-1, keepdims=True)
    y_ref[...] = x2 * lax.rsqrt(ms + EPS) * gf_ref[...]


def _post_call(x2d, conv2d, attn2d, gt1, sc2, sh2, gt2, g2, gf, wout, wup, wdn, fw, fb, hist, tm, t_len, name):
    r, d = x2d.shape
    nt = r // tm
    nb = hist.shape[0]
    nseg = max(1, tm // t_len)
    tpb = max(1, t_len // tm)
    assert nt * nseg == nb * tpb and t_len >= FFN_CONV_WIDTH - 1
    if gt1.ndim == 3:
        mod_spec = pl.BlockSpec((None, 1, d), lambda i: (i // tpb, 0, 0))
    else:
        mod_spec = pl.BlockSpec((tm, d), lambda i: (i, 0))

    def rows(width):
        return pl.BlockSpec((tm, width), lambda i: (i, 0))

    def const(shape):
        return pl.BlockSpec(shape, lambda i: (0,) * len(shape), pipeline_mode=pl.Buffered(1))

    state_spec = pl.BlockSpec((nseg, FFN_CONV_WIDTH - 1, 2 * D_FF), lambda i: (i * nseg // tpb, 0, 0))
    kern = functools.partial(_post_kernel, tm=tm, t_len=t_len, tiles_per_batch=tpb)
    scratch = [pltpu.VMEM((FFN_CONV_WIDTH - 1, 2 * D_FF), F32)] if nseg == 1 else []
    return pl.pallas_call(
        kern,
        grid=(nt,),
        in_specs=[rows(d), rows(CONV_DIM), rows(ATTN_DIM), mod_spec, mod_spec, mod_spec, mod_spec,
                  const((1, d)), const((1, d)),
                  const((CONV_DIM + ATTN_DIM, d)), const((d, 2 * D_FF)), const((D_FF, d)),
                  const((FFN_CONV_WIDTH, 2 * D_FF)), const((1, 2 * D_FF)),
                  state_spec],
        out_specs=(rows(d), state_spec),
        out_shape=(jax.ShapeDtypeStruct((r, d), F32),
                   jax.ShapeDtypeStruct((nb, FFN_CONV_WIDTH - 1, 2 * D_FF), F32)),
        scratch_shapes=scratch,
        compiler_params=_cparams(("arbitrary",)),
        name=name,
    )(x2d, conv2d, attn2d, gt1, sc2, sh2, gt2, g2, gf, wout, wup, wdn, fw, fb, hist)


def _rope_tables(pos):
    half = ROT_DIM // 2
    inv = 1.0 / (ROPE_THETA ** (jnp.arange(0, ROT_DIM, 2, dtype=F32) / ROT_DIM))
    ang = pos.astype(F32)[:, None] * inv[None, :]
    cos, sin = jnp.cos(ang), jnp.sin(ang)
    t = pos.shape[0]
    rest1 = jnp.ones((t, HEAD_DIM - ROT_DIM), F32)
    rest0 = jnp.zeros((t, HEAD_DIM - ROT_DIM), F32)
    z = jnp.zeros((t, half), F32)
    c64 = jnp.concatenate([cos, cos, rest1], axis=1)
    a64 = jnp.concatenate([z, sin, rest0], axis=1)
    b64 = jnp.concatenate([-sin, z, rest0], axis=1)
    return tuple(jnp.concatenate([m, m], axis=1) for m in (c64, a64, b64))


def _pad_w_in(w_in):
    d = w_in.shape[0]
    return jnp.concatenate(
        [w_in[:, :COL_KI + IDX_DIM], jnp.zeros((d, LANES - IDX_DIM), F32),
         w_in[:, COL_KI + IDX_DIM:], jnp.zeros((d, LANES - N_IDX_HEADS), F32)], axis=1).astype(BF16)


def _layer_group(x, mods, pos, conv_hist, ffn_hist, past, w, *, per_row_mod, tm_in, tt_conv, tq, tm_post, tag):
    (g1, w_in_p, dw_w, dw_b, ln_g, ln_b, w_out, g2, w_up, fdw_w, fdw_b, w_down, gf) = w
    b, t, d = x.shape
    r = b * t
    x2d = x.reshape(r, d)
    if per_row_mod:
        sh1, sc1, gt1, sh2, sc2, gt2 = [jnp.repeat(m, t, axis=0) for m in mods]
        tabs = tuple(jnp.tile(m, (b, 1)) for m in _rope_tables(pos))
    else:
        sh1, sc1, gt1, sh2, sc2, gt2 = [m[:, None, :] for m in mods]
        tabs = _rope_tables(pos)

    glu, q, k, v, qi, ki, wi = _inproj_call(x2d, sc1, sh1, g1, w_in_p, tabs, tm_in, "inproj_" + tag)
    glu3 = glu.reshape(b, t, CONV_DIM)
    k3, v3, ki3 = k.reshape(b, t, KV_DIM), v.reshape(b, t, KV_DIM), ki.reshape(b, t, IDX_DIM)

    conv_out = _conv_call(glu3, conv_hist, dw_w, dw_b, ln_g, ln_b, tt_conv, "conv_" + tag)
    new_conv = glu3[:, t - CONV_HIST:, :]

    if past is None:
        past_len = 0
        k_all, v_all, ki_all = k3.astype(BF16), v3.astype(BF16), ki3.astype(BF16)
    else:
        ck, cv, cki = past
        past_len = ck.shape[1]
        k_all = jnp.concatenate([ck.reshape(b, past_len, KV_DIM).astype(BF16), k3.astype(BF16)], axis=1)
        v_all = jnp.concatenate([cv.reshape(b, past_len, KV_DIM).astype(BF16), v3.astype(BF16)], axis=1)
        ki_all = jnp.concatenate([cki.astype(BF16), ki3.astype(BF16)], axis=1)
    n_keys = past_len + t
    k_sel = min(TOPK_MAX, n_keys // 4)
    pad = (-n_keys) % LANES
    if pad:
        k_all, v_all, ki_all = [jnp.pad(a, ((0, 0), (0, pad), (0, 0))) for a in (k_all, v_all, ki_all)]
    kit_all = jnp.swapaxes(ki_all, 1, 2)
    q3, qi3, wi3 = q.reshape(b, t, ATTN_DIM), qi.reshape(b, t, -1), wi.reshape(b, t, -1)
    if past is None:
        pieces = [_attn_call(q3, qi3, wi3, k_all, v_all, kit_all, tq, i, 1, (i + 1) * tq, t, 0, k_sel,
                             "attn_%s%d" % (tag, i)) for i in range(t // tq)]
        attn_out = jnp.concatenate(pieces, axis=1)
    else:
        attn_out = _attn_call(q3, qi3, wi3, k_all, v_all, kit_all, tq, 0, t // tq, n_keys + pad, t, past_len, k_sel,
                              "attn_" + tag)

    y, new_ffn = _post_call(x2d, conv_out.reshape(r, CONV_DIM), attn_out.reshape(r, ATTN_DIM),
                            gt1, sc2, sh2, gt2, g2, gf, w_out, w_up, w_down, fdw_w, fdw_b, ffn_hist,
                            tm_post, t, "post_" + tag)
    return (y.reshape(b, t, d), k3.reshape(b, t, N_KV_HEADS, HEAD_DIM), v3.reshape(b, t, N_KV_HEADS, HEAD_DIM),
            ki3, new_conv, new_ffn)


def kernel(x_prompt, x_sample, cache_k, cache_v, cache_kidx, state_conv, state_ffn_conv, c_prompt, c_sample,
           w_ada, b_ada, norm1_g, w_in, conv_dw_w, conv_dw_b, conv_ln_g, conv_ln_b, w_out, norm2_g,
           w_up, ffn_dw_w, ffn_dw_b, w_down, final_norm_g):
    depth = w_ada.shape[0]
    assert depth == 1, "the final norm is fused into the single layer's last kernel"
    bp, sp, d = x_prompt.shape
    bs, ts, _ = x_sample.shape
    past_len = cache_k.shape[2]
    l = 0
    mod = _mod_call(jnp.concatenate([c_prompt, c_sample], axis=0), w_ada[l], b_ada[l])
    mods_p = jnp.split(mod[:bp], 6, axis=-1)
    mods_s = jnp.split(mod[bp:], 6, axis=-1)
    w = (norm1_g[l].reshape(1, d), _pad_w_in(w_in[l]), conv_dw_w[l], conv_dw_b[l], conv_ln_g[l], conv_ln_b[l],
         w_out[l].astype(BF16), norm2_g[l].reshape(1, d), w_up[l].astype(BF16), ffn_dw_w[l],
         ffn_dw_b[l].reshape(1, -1), w_down[l].astype(BF16), final_norm_g.reshape(1, d))

    conv0 = jnp.zeros((bp, CONV_HIST, CONV_DIM), F32)
    ffn0 = jnp.zeros((bp, FFN_CONV_WIDTH - 1, 2 * D_FF), F32)
    out_p = _layer_group(x_prompt, mods_p, jnp.arange(sp, dtype=I32), conv0, ffn0, None, w,
                         per_row_mod=False, tm_in=512, tt_conv=128, tq=256, tm_post=256, tag="p")
    out_s = _layer_group(x_sample, mods_s, past_len + jnp.arange(ts, dtype=I32), state_conv[l], state_ffn_conv[l],
                         (cache_k[l], cache_v[l], cache_kidx[l]), w,
                         per_row_mod=True, tm_in=bs * ts, tt_conv=ts, tq=ts, tm_post=bs * ts, tag="s")
    y_p, k_p, v_p, ki_p, conv_p, ffn_p = out_p
    y_s, k_s, v_s, ki_s, conv_s, ffn_s = out_s
    st = lambda a: a[None]
    return (y_p, y_s, st(k_p), st(v_p), st(ki_p), st(conv_p), st(ffn_p),
            st(k_s), st(v_s), st(ki_s), st(conv_s), st(ffn_s))
```

```python
import functools

import jax
import jax.numpy as jnp
from jax import lax
from jax.experimental import pallas as pl
from jax.experimental.pallas import tpu as pltpu

F32 = jnp.float32
BF16 = jnp.bfloat16
I32 = jnp.int32

D_MODEL = 1024
CHUNK = 64
CONV_DIM = 512
CONV_WIDTH = 31
N_HEADS = 8
HEAD_DIM = 64
N_KV_HEADS = 2
ATTN_DIM = N_HEADS * HEAD_DIM
KV_DIM = N_KV_HEADS * HEAD_DIM
ROT_DIM = HEAD_DIM // 4
ROPE_THETA = 500000.0
N_IDX_HEADS = 8
IDX_DIM = 32
TOPK_MAX = 256
D_FF = 2816
FFN_CONV_WIDTH = 3
EPS = 1e-6

LANES = 128
SUBLANES = 8
MXU_N = 256
VMEM_LIMIT = 52 * 1024 * 1024

COL_U, COL_UG, COL_Q, COL_K, COL_V, COL_QI = 0, 512, 1024, 1536, 1664, 1792
COL_KI = 2048
COL_WI = COL_KI + LANES
IN_PAD = COL_WI + LANES

Q_SCALE = HEAD_DIM ** -0.5 * 1.4426950408889634

INT_MIN = -2147483648
KEY_NEG_INF = -2139095041
IDX_BITS = 13


def _cparams(sem):
    return pltpu.CompilerParams(dimension_semantics=sem, vmem_limit_bytes=VMEM_LIMIT)


def _silu(x):
    return x * jax.nn.sigmoid(x)


def _mod_kernel(c_ref, w_ref, b_ref, o_ref):
    s = _silu(c_ref[...]).astype(BF16)
    o_ref[...] = jnp.dot(s, w_ref[...].astype(BF16), preferred_element_type=F32) + b_ref[...]


def _mod_call(c_all, w_ada, b_ada):
    nb, d = c_all.shape
    n = w_ada.shape[1]
    tn = 512
    return pl.pallas_call(
        _mod_kernel,
        grid=(n // tn,),
        in_specs=[pl.BlockSpec((nb, d), lambda j: (0, 0)),
                  pl.BlockSpec((d, tn), lambda j: (0, j)),
                  pl.BlockSpec((1, tn), lambda j: (0, j))],
        out_specs=pl.BlockSpec((nb, tn), lambda j: (0, j)),
        out_shape=jax.ShapeDtypeStruct((nb, n), F32),
        compiler_params=_cparams(("arbitrary",)),
        name="mod",
    )(c_all, w_ada, b_ada.reshape(1, n))


def _rmsnorm_mod(x, g, sc, sh):
    ms = jnp.mean(x * x, axis=-1, keepdims=True)
    return (x * lax.rsqrt(ms + EPS) * g) * (1.0 + sc) + sh


def _inproj_kernel(x_ref, sc_ref, sh_ref, g_ref, w_ref, cos_ref, sa_ref, sb_ref,
                   glu_ref, q_ref, k_ref, v_ref, qi_ref, ki_ref, wi_ref):
    h = _rmsnorm_mod(x_ref[...], g_ref[...], sc_ref[...], sh_ref[...])
    z = jnp.dot(h.astype(BF16), w_ref[...], preferred_element_type=F32)
    glu_ref[...] = z[:, COL_U:COL_U + CONV_DIM] * jax.nn.sigmoid(z[:, COL_UG:COL_UG + CONV_DIM])
    cos, sa, sb = cos_ref[...], sa_ref[...], sb_ref[...]

    def rope(xs):
        return (xs * cos + pltpu.roll(xs, ROT_DIM // 2, axis=1) * sa
                + pltpu.roll(xs, LANES - ROT_DIM // 2, axis=1) * sb)

    for j in range(ATTN_DIM // LANES):
        c0 = COL_Q + LANES * j
        q_ref[:, LANES * j:LANES * (j + 1)] = (rope(z[:, c0:c0 + LANES]) * Q_SCALE).astype(BF16)
    k_ref[...] = rope(z[:, COL_K:COL_K + KV_DIM])
    v_ref[...] = z[:, COL_V:COL_V + KV_DIM]
    qi_ref[...] = z[:, COL_QI:COL_QI + N_IDX_HEADS * IDX_DIM].astype(BF16)
    ki_ref[...] = z[:, COL_KI:COL_KI + IDX_DIM]
    wi_ref[...] = z[:, COL_WI:COL_WI + N_IDX_HEADS]


def _inproj_call(x2, sc, sh, g1, w_in_p, tabs, tm, name):
    r, d = x2.shape
    nt = r // tm
    if sc.ndim == 3:
        tpb = nt // sc.shape[0]
        mod_spec = pl.BlockSpec((None, 1, d), lambda i: (i // tpb, 0, 0))
    else:
        mod_spec = pl.BlockSpec((tm, d), lambda i: (i, 0))
    ntab = tabs[0].shape[0] // tm
    tab_spec = pl.BlockSpec((tm, LANES), lambda i: (i % ntab, 0))

    def rows(width):
        return pl.BlockSpec((tm, width), lambda i: (i, 0))

    out_shapes = (jax.ShapeDtypeStruct((r, CONV_DIM), F32), jax.ShapeDtypeStruct((r, ATTN_DIM), BF16),
                  jax.ShapeDtypeStruct((r, KV_DIM), F32), jax.ShapeDtypeStruct((r, KV_DIM), F32),
                  jax.ShapeDtypeStruct((r, N_IDX_HEADS * IDX_DIM), BF16),
                  jax.ShapeDtypeStruct((r, IDX_DIM), F32), jax.ShapeDtypeStruct((r, N_IDX_HEADS), F32))
    return pl.pallas_call(
        _inproj_kernel,
        grid=(nt,),
        in_specs=[rows(d), mod_spec, mod_spec,
                  pl.BlockSpec((1, d), lambda i: (0, 0)),
                  pl.BlockSpec((d, IN_PAD), lambda i: (0, 0)),
                  tab_spec, tab_spec, tab_spec],
        out_specs=(rows(CONV_DIM), rows(ATTN_DIM), rows(KV_DIM), rows(KV_DIM),
                   rows(N_IDX_HEADS * IDX_DIM), rows(IDX_DIM), rows(N_IDX_HEADS)),
        out_shape=out_shapes,
        compiler_params=_cparams(("arbitrary",)),
        name=name,
    )(x2, sc, sh, g1, w_in_p, *tabs)


CONV_HIST = CONV_WIDTH - 1
CONV_PAD = 32
CONV_RB = 64


def _conv_kernel(glu_ref, hist_ref, w_ref, b_ref, lg_ref, lb_ref, o_ref, ext_ref, y_ref, *, tt):
    t = pl.program_id(1)

    @pl.when(t == 0)
    def _():
        ext_ref[0:CONV_PAD - CONV_HIST, :] = jnp.zeros((CONV_PAD - CONV_HIST, CONV_DIM), F32)
        ext_ref[CONV_PAD - CONV_HIST:CONV_PAD, :] = hist_ref[...]

    @pl.when(t > 0)
    def _():
        ext_ref[0:CONV_PAD, :] = ext_ref[tt:tt + CONV_PAD, :]

    ext_ref[CONV_PAD:CONV_PAD + tt, :] = glu_ref[...]

    off = CONV_PAD - CONV_HIST
    rb = min(CONV_RB, tt)
    for r0 in range(0, tt, rb):
        for c0 in range(0, CONV_DIM, LANES):
            acc = None
            for b in range(SUBLANES):
                span = (CONV_WIDTH - 1 - b) // SUBLANES * SUBLANES
                mis = (off + b) % SUBLANES
                base = off + b - mis + r0
                cover = -(-(mis + rb + span) // SUBLANES) * SUBLANES
                win = ext_ref[base:base + cover, c0:c0 + LANES]
                if mis:
                    win = pltpu.roll(win, cover - mis, axis=0)
                for j in range(b, CONV_WIDTH, SUBLANES):
                    term = win[j - b:j - b + rb, :] * w_ref[j:j + 1, c0:c0 + LANES]
                    acc = term if acc is None else acc + term
            y_ref[r0:r0 + rb, c0:c0 + LANES] = acc + b_ref[:, c0:c0 + LANES]

    y = y_ref[...]
    mu = jnp.mean(y, axis=-1, keepdims=True)
    yc = y - mu
    var = jnp.mean(yc * yc, axis=-1, keepdims=True)
    o_ref[...] = _silu(yc * lax.rsqrt(var + EPS) * lg_ref[...] + lb_ref[...]).astype(BF16)


def _conv_call(glu3, hist, dw_w, dw_b, ln_g, ln_b, tt, name):
    b, t, c = glu3.shape
    vec = pl.BlockSpec((1, c), lambda i, j: (0, 0))
    return pl.pallas_call(
        functools.partial(_conv_kernel, tt=tt),
        grid=(b, t // tt),
        in_specs=[pl.BlockSpec((None, tt, c), lambda i, j: (i, j, 0)),
                  pl.BlockSpec((None, CONV_HIST, c), lambda i, j: (i, 0, 0)),
                  pl.BlockSpec((CONV_WIDTH, c), lambda i, j: (0, 0)),
                  vec, vec, vec],
        out_specs=pl.BlockSpec((None, tt, c), lambda i, j: (i, j, 0)),
        out_shape=jax.ShapeDtypeStruct((b, t, c), BF16),
        scratch_shapes=[pltpu.VMEM((tt + CONV_PAD, c), F32), pltpu.VMEM((tt, c), F32)],
        compiler_params=_cparams(("arbitrary", "arbitrary")),
        name=name,
    )(glu3, hist, dw_w, dw_b.reshape(1, c), ln_g.reshape(1, c), ln_b.reshape(1, c))


CNT_ROWS = 64


def _attn_kernel(q_ref, qi_ref, wi_ref, k_ref, v_ref, kit_ref, o_ref, key_ref, bias_ref,
                 *, tq, n_keys, q0, t_new, past, k_sel):
    j = pl.program_id(1)
    row = lax.broadcasted_iota(I32, (tq, 1), 0) + (q0 + j * tq)
    limit = past + jnp.minimum((row // CHUNK + 1) * CHUNK, t_new)
    col = lax.broadcasted_iota(I32, (tq, n_keys), 1)
    adm = col < limit

    kit = kit_ref[...]
    wi = wi_ref[...]
    score = jnp.zeros((tq, n_keys), F32)
    for h in range(N_IDX_HEADS):
        d = jnp.dot(qi_ref[:, IDX_DIM * h:IDX_DIM * (h + 1)], kit, preferred_element_type=F32)
        score = score + wi[:, h:h + 1] * jnp.maximum(d, 0.0)
    bits = pltpu.bitcast(jnp.where(adm, score, -jnp.inf), I32)
    key_ref[...] = bits ^ ((bits >> 31) & 0x7FFFFFFF)

    rg = min(CNT_ROWS, tq)

    def count_ge(cands):
        outs = [[] for _ in cands]
        for r0 in range(0, tq, rg):
            cs = [jnp.broadcast_to(c[r0:r0 + rg, :], (rg, LANES)) for c in cands]
            accs = [jnp.zeros((rg, LANES), F32) for _ in cands]
            for c0 in range(0, n_keys, LANES):
                kv = key_ref[r0:r0 + rg, c0:c0 + LANES]
                accs = [a + jnp.where(kv >= c, 1.0, 0.0) for a, c in zip(accs, cs)]
            for o, a in zip(outs, accs):
                o.append(jnp.sum(a, axis=1, keepdims=True))
        return [o[0] if len(o) == 1 else jnp.concatenate(o, axis=0) for o in outs]

    def bit2_step(it, r):
        sh = 30 - 2 * it
        step = jnp.left_shift(jnp.int32(1), sh)
        counts = count_ge([r + step, r + 2 * step, r + 3 * step])
        inc = sum(jnp.where(n >= k_sel, 1, 0) for n in counts)
        return r + jnp.left_shift(inc, sh)

    thr = lax.fori_loop(0, 16, bit2_step, jnp.full((tq, 1), INT_MIN, I32))

    keys = key_ref[...]
    ge = keys >= thr
    cnt_ge = jnp.sum(jnp.where(ge, 1.0, 0.0), axis=1, keepdims=True)
    bias_ref[...] = jnp.where(ge & adm, 0.0, -jnp.inf)
    split_tie = (cnt_ge > k_sel) & (thr != KEY_NEG_INF)
    any_split = jnp.max(jnp.where(split_tie, 1.0, 0.0)) > 0.0

    @pl.when(any_split)
    def _():
        gt = keys > thr
        eq = keys == thr
        need = k_sel - jnp.sum(jnp.where(gt, 1.0, 0.0), axis=1, keepdims=True)

        def idx_step(it, last):
            cand = last + jnp.left_shift(jnp.int32(1), IDX_BITS - 1 - it)
            below = jnp.sum(jnp.where(eq & (col < cand), 1.0, 0.0), axis=1, keepdims=True)
            return jnp.where(below < need, cand, last)

        last = lax.fori_loop(0, IDX_BITS, idx_step, jnp.zeros((tq, 1), I32))
        sel = (gt | (eq & (col <= last))) & adm
        bias_ref[...] = jnp.where(sel, 0.0, -jnp.inf)

    bias = bias_ref[...]
    rep = N_HEADS // N_KV_HEADS
    for g in range(N_KV_HEADS):
        qg = jnp.concatenate([q_ref[:, HEAD_DIM * (rep * g + r):HEAD_DIM * (rep * g + r + 1)] for r in range(rep)],
                             axis=0)
        logits = lax.dot_general(qg, k_ref[:, HEAD_DIM * g:HEAD_DIM * (g + 1)],
                                 (((1,), (1,)), ((), ())), preferred_element_type=F32)
        ps, ss = [], []
        for r in range(rep):
            lg = logits[r * tq:(r + 1) * tq, :] + bias
            p = jnp.exp2(lg - jnp.max(lg, axis=1, keepdims=True))
            ss.append(jnp.sum(p, axis=1, keepdims=True))
            ps.append(p.astype(BF16))
        o = jnp.dot(jnp.concatenate(ps, axis=0), v_ref[:, HEAD_DIM * g:HEAD_DIM * (g + 1)],
                    preferred_element_type=F32)
        for r in range(rep):
            h = rep * g + r
            o_ref[:, HEAD_DIM * h:HEAD_DIM * (h + 1)] = (o[r * tq:(r + 1) * tq, :] / ss[r]).astype(BF16)


def _attn_call(q3, qi3, wi3, k_all, v_all, kit_all, tq, q_blk0, n_q_blk, n_keys, t_new, past, k_sel, name):
    b = q3.shape[0]
    assert n_keys % LANES == 0 and n_keys <= (1 << IDX_BITS) and n_keys <= k_all.shape[1]
    kern = functools.partial(_attn_kernel, tq=tq, n_keys=n_keys, q0=q_blk0 * tq, t_new=t_new, past=past,
                             k_sel=float(k_sel))
    return pl.pallas_call(
        kern,
        grid=(b, n_q_blk),
        in_specs=[pl.BlockSpec((None, tq, ATTN_DIM), lambda i, j: (i, q_blk0 + j, 0)),
                  pl.BlockSpec((None, tq, N_IDX_HEADS * IDX_DIM), lambda i, j: (i, q_blk0 + j, 0)),
                  pl.BlockSpec((None, tq, N_IDX_HEADS), lambda i, j: (i, q_blk0 + j, 0)),
                  pl.BlockSpec((None, n_keys, KV_DIM), lambda i, j: (i, 0, 0)),
                  pl.BlockSpec((None, n_keys, KV_DIM), lambda i, j: (i, 0, 0)),
                  pl.BlockSpec((None, IDX_DIM, n_keys), lambda i, j: (i, 0, 0))],
        out_specs=pl.BlockSpec((None, tq, ATTN_DIM), lambda i, j: (i, j, 0)),
        out_shape=jax.ShapeDtypeStruct((b, n_q_blk * tq, ATTN_DIM), BF16),
        scratch_shapes=[pltpu.VMEM((tq, n_keys), I32), pltpu.VMEM((tq, n_keys), F32)],
        compiler_params=_cparams(("arbitrary", "arbitrary")),
        name=name,
    )(q3, qi3, wi3, k_all, v_all, kit_all)


SCORE_KEYS = 256
CNT_KEYS = 64


def _attn_t_kernel(q_ref, qi_ref, wit_ref, k_ref, vt_ref, ki_ref, o_ref, key_ref, bias_ref,
                   *, tq, n_keys, q0, t_new, past, k_sel):
    j = pl.program_id(1)
    qpos = lax.broadcasted_iota(I32, (1, tq), 1) + (q0 + j * tq)
    limit = past + jnp.minimum((qpos // CHUNK + 1) * CHUNK, t_new)
    nt = (((1,), (1,)), ((), ()))

    qi_stack = jnp.concatenate([qi_ref[:, IDX_DIM * h:IDX_DIM * (h + 1)] for h in range(N_IDX_HEADS)], axis=0)
    wit = wit_ref[...]
    kc = min(SCORE_KEYS, n_keys)
    for c0 in range(0, n_keys, kc):
        d = lax.dot_general(ki_ref[c0:c0 + kc, :], qi_stack, nt, preferred_element_type=F32)
        sc = wit[0:1, :] * jnp.maximum(d[:, 0:tq], 0.0)
        for h in range(1, N_IDX_HEADS):
            sc = sc + wit[h:h + 1, :] * jnp.maximum(d[:, h * tq:(h + 1) * tq], 0.0)
        kpos = lax.broadcasted_iota(I32, (kc, tq), 0) + c0
        bits = pltpu.bitcast(jnp.where(kpos < limit, sc, -jnp.inf), I32)
        key_ref[c0:c0 + kc, :] = bits ^ ((bits >> 31) & 0x7FFFFFFF)

    sl = min(CNT_KEYS, n_keys)

    def count_ge(cand):
        cb = jnp.broadcast_to(cand, (sl, tq))
        acc = jnp.zeros((sl, tq), F32)
        for c0 in range(0, n_keys, sl):
            acc = acc + jnp.where(key_ref[c0:c0 + sl, :] >= cb, 1.0, 0.0)
        return jnp.sum(acc, axis=0, keepdims=True)

    def bit_step(it, r):
        cand = r + jnp.left_shift(jnp.int32(1), 31 - it)
        return jnp.where(count_ge(cand) >= k_sel, cand, r)

    thr = lax.fori_loop(0, 32, bit_step, jnp.full((1, tq), INT_MIN, I32))

    keys = key_ref[...]
    kpos = lax.broadcasted_iota(I32, (n_keys, tq), 0)
    adm = kpos < limit
    ge = keys >= thr
    cnt_ge = jnp.sum(jnp.where(ge, 1.0, 0.0), axis=0, keepdims=True)
    bias_ref[...] = jnp.where(ge & adm, 0.0, -jnp.inf)
    split_tie = (cnt_ge > k_sel) & (thr != KEY_NEG_INF)
    any_split = jnp.max(jnp.where(split_tie, 1.0, 0.0)) > 0.0

    @pl.when(any_split)
    def _():
        gt = keys > thr
        eq = keys == thr
        need = k_sel - jnp.sum(jnp.where(gt, 1.0, 0.0), axis=0, keepdims=True)

        def idx_step(it, last):
            cand = last + jnp.left_shift(jnp.int32(1), IDX_BITS - 1 - it)
            below = jnp.sum(jnp.where(eq & (kpos < cand), 1.0, 0.0), axis=0, keepdims=True)
            return jnp.where(below < need, cand, last)

        last = lax.fori_loop(0, IDX_BITS, idx_step, jnp.zeros((1, tq), I32))
        sel = (gt | (eq & (kpos <= last))) & adm
        bias_ref[...] = jnp.where(sel, 0.0, -jnp.inf)

    bias = bias_ref[...]
    rep = N_HEADS // N_KV_HEADS
    for g in range(N_KV_HEADS):
        qg = jnp.concatenate([q_ref[:, HEAD_DIM * (rep * g + r):HEAD_DIM * (rep * g + r + 1)] for r in range(rep)],
                             axis=0)
        logits = lax.dot_general(k_ref[:, HEAD_DIM * g:HEAD_DIM * (g + 1)], qg, nt, preferred_element_type=F32)
        ps, ss = [], []
        for r in range(rep):
            lg = logits[:, r * tq:(r + 1) * tq] + bias
            p = jnp.exp2(lg - jnp.max(lg, axis=0, keepdims=True))
            ss.append(jnp.sum(p, axis=0, keepdims=True))
            ps.append(p.astype(BF16))
        o = jnp.dot(vt_ref[HEAD_DIM * g:HEAD_DIM * (g + 1), :], jnp.concatenate(ps, axis=1),
                    preferred_element_type=F32)
        for r in range(rep):
            h = rep * g + r
            o_ref[HEAD_DIM * h:HEAD_DIM * (h + 1), :] = (o[:, r * tq:(r + 1) * tq] / ss[r]).astype(BF16)


def _attn_t_call(q3, qi3, wit3, k_all, vt_all, ki_all, tq, q_blk0, n_q_blk, n_keys, t_new, past, k_sel, name):
    b = q3.shape[0]
    assert n_keys % LANES == 0 and n_keys <= (1 << IDX_BITS) and n_keys <= k_all.shape[1] and tq % LANES == 0
    kern = functools.partial(_attn_t_kernel, tq=tq, n_keys=n_keys, q0=q_blk0 * tq, t_new=t_new, past=past,
                             k_sel=float(k_sel))
    return pl.pallas_call(
        kern,
        grid=(b, n_q_blk),
        in_specs=[pl.BlockSpec((None, tq, ATTN_DIM), lambda i, j: (i, q_blk0 + j, 0)),
                  pl.BlockSpec((None, tq, N_IDX_HEADS * IDX_DIM), lambda i, j: (i, q_blk0 + j, 0)),
                  pl.BlockSpec((None, N_IDX_HEADS, tq), lambda i, j: (i, 0, q_blk0 + j)),
                  pl.BlockSpec((None, n_keys, KV_DIM), lambda i, j: (i, 0, 0)),
                  pl.BlockSpec((None, KV_DIM, n_keys), lambda i, j: (i, 0, 0)),
                  pl.BlockSpec((None, n_keys, IDX_DIM), lambda i, j: (i, 0, 0))],
        out_specs=pl.BlockSpec((None, ATTN_DIM, tq), lambda i, j: (i, 0, j)),
        out_shape=jax.ShapeDtypeStruct((b, ATTN_DIM, n_q_blk * tq), BF16),
        scratch_shapes=[pltpu.VMEM((n_keys, tq), I32), pltpu.VMEM((n_keys, tq), F32)],
        compiler_params=_cparams(("arbitrary", "arbitrary")),
        name=name,
    )(q3, qi3, wit3, k_all, vt_all, ki_all)


FF_CHUNKS = ((0, 1024), (1024, 1024), (2048, 768))


def _post_kernel(x_ref, conv_ref, attn_ref, gt1_ref, sc2_ref, sh2_ref, gt2_ref, g2_ref, gf_ref,
                 wout_ref, wup_ref, wdn_ref, fw_ref, fb_ref, hist_ref,
                 y_ref, newffn_ref, *scratch, tm, t_len, tiles_per_batch, attn_transposed):
    nseg = max(1, tm // t_len)
    seg = tm // nseg
    i = pl.program_id(0)
    if nseg == 1:
        carry_ref = scratch[0]

        @pl.when(i % tiles_per_batch == 0)
        def _():
            carry_ref[...] = hist_ref[0]

    w_attn = wout_ref[CONV_DIM:CONV_DIM + ATTN_DIM, :]
    if attn_transposed:
        mix_attn = lax.dot_general(attn_ref[...], w_attn, (((0,), (0,)), ((), ())), preferred_element_type=F32)
    else:
        mix_attn = jnp.dot(attn_ref[...], w_attn, preferred_element_type=F32)
    mix = jnp.dot(conv_ref[...], wout_ref[0:CONV_DIM, :], preferred_element_type=F32) + mix_attn
    x1 = x_ref[...] + gt1_ref[...] * mix
    h2 = _rmsnorm_mod(x1, g2_ref[...], sc2_ref[...], sh2_ref[...]).astype(BF16)

    row = lax.broadcasted_iota(I32, (SUBLANES, 1), 0)

    def causal3(u, col0, width):
        w0 = fw_ref[0:1, col0:col0 + width]
        w1 = fw_ref[1:2, col0:col0 + width]
        w2 = fw_ref[2:3, col0:col0 + width]
        outs = []
        for s in range(nseg):
            us = u[s * seg:(s + 1) * seg, :]
            if nseg == 1:
                h0 = carry_ref[0:1, col0:col0 + width]
                h1 = carry_ref[1:2, col0:col0 + width]
            else:
                h0 = hist_ref[s, 0:1, col0:col0 + width]
                h1 = hist_ref[s, 1:2, col0:col0 + width]
            p1 = pltpu.roll(us, 1, axis=0)
            p2 = pltpu.roll(us, 2, axis=0)
            p1 = jnp.concatenate([jnp.where(row == 0, h1, p1[0:SUBLANES, :]), p1[SUBLANES:, :]], axis=0)
            p2 = jnp.concatenate([jnp.where(row == 0, h0, jnp.where(row == 1, h1, p2[0:SUBLANES, :])),
                                  p2[SUBLANES:, :]], axis=0)
            outs.append(us * w2 + p1 * w1 + p2 * w0 + fb_ref[:, col0:col0 + width])
            if nseg == 1:
                carry_ref[:, col0:col0 + width] = us[seg - 2:seg, :]
            else:
                newffn_ref[s, :, col0:col0 + width] = us[seg - 2:seg, :]
        return outs[0] if nseg == 1 else jnp.concatenate(outs, axis=0)

    acc = jnp.zeros((tm, D_MODEL), F32)
    for c0, cw in FF_CHUNKS:
        ua = jnp.dot(h2, wup_ref[:, c0:c0 + cw], preferred_element_type=F32)
        ug = jnp.dot(h2, wup_ref[:, D_FF + c0:D_FF + c0 + cw], preferred_element_type=F32)
        a = causal3(ua, c0, cw)
        g = causal3(ug, D_FF + c0, cw)
        acc = acc + jnp.dot((a * _silu(g)).astype(BF16), wdn_ref[c0:c0 + cw, :], preferred_element_type=F32)

    if nseg == 1:
        @pl.when(i % tiles_per_batch == tiles_per_batch - 1)
        def _():
            newffn_ref[0] = carry_ref[...]

    x2 = x1 + gt2_ref[...] * acc
    ms = jnp.mean(x2 * x2, axis=-1, keepdims=True)
    y_ref[...] = x2 * lax.rsqrt(ms + EPS) * gf_ref[...]


def _post_call(x2d, conv2d, attn, gt1, sc2, sh2, gt2, g2, gf, wout, wup, wdn, fw, fb, hist, tm, t_len, name):
    r, d = x2d.shape
    nt = r // tm
    nb = hist.shape[0]
    nseg = max(1, tm // t_len)
    tpb = max(1, t_len // tm)
    assert nt * nseg == nb * tpb and t_len >= FFN_CONV_WIDTH - 1
    if gt1.ndim == 3:
        mod_spec = pl.BlockSpec((None, 1, d), lambda i: (i // tpb, 0, 0))
    else:
        mod_spec = pl.BlockSpec((tm, d), lambda i: (i, 0))

    def rows(width):
        return pl.BlockSpec((tm, width), lambda i: (i, 0))

    def const(shape):
        return pl.BlockSpec(shape, lambda i: (0,) * len(shape), pipeline_mode=pl.Buffered(1))

    state_spec = pl.BlockSpec((nseg, FFN_CONV_WIDTH - 1, 2 * D_FF), lambda i: (i * nseg // tpb, 0, 0))
    attn_t = attn.ndim == 3
    if attn_t:
        assert nseg == 1
        attn_spec = pl.BlockSpec((None, ATTN_DIM, tm), lambda i: (i // tpb, 0, i % tpb))
    else:
        attn_spec = rows(ATTN_DIM)
    kern = functools.partial(_post_kernel, tm=tm, t_len=t_len, tiles_per_batch=tpb, attn_transposed=attn_t)
    scratch = [pltpu.VMEM((FFN_CONV_WIDTH - 1, 2 * D_FF), F32)] if nseg == 1 else []
    return pl.pallas_call(
        kern,
        grid=(nt,),
        in_specs=[rows(d), rows(CONV_DIM), attn_spec, mod_spec, mod_spec, mod_spec, mod_spec,
                  const((1, d)), const((1, d)),
                  const((CONV_DIM + ATTN_DIM, d)), const((d, 2 * D_FF)), const((D_FF, d)),
                  const((FFN_CONV_WIDTH, 2 * D_FF)), const((1, 2 * D_FF)),
                  state_spec],
        out_specs=(rows(d), state_spec),
        out_shape=(jax.ShapeDtypeStruct((r, d), F32),
                   jax.ShapeDtypeStruct((nb, FFN_CONV_WIDTH - 1, 2 * D_FF), F32)),
        scratch_shapes=scratch,
        compiler_params=_cparams(("arbitrary",)),
        name=name,
    )(x2d, conv2d, attn, gt1, sc2, sh2, gt2, g2, gf, wout, wup, wdn, fw, fb, hist)


def _rope_tables(pos):
    half = ROT_DIM // 2
    inv = 1.0 / (ROPE_THETA ** (jnp.arange(0, ROT_DIM, 2, dtype=F32) / ROT_DIM))
    ang = pos.astype(F32)[:, None] * inv[None, :]
    cos, sin = jnp.cos(ang), jnp.sin(ang)
    t = pos.shape[0]
    rest1 = jnp.ones((t, HEAD_DIM - ROT_DIM), F32)
    rest0 = jnp.zeros((t, HEAD_DIM - ROT_DIM), F32)
    z = jnp.zeros((t, half), F32)
    c64 = jnp.concatenate([cos, cos, rest1], axis=1)
    a64 = jnp.concatenate([z, sin, rest0], axis=1)
    b64 = jnp.concatenate([-sin, z, rest0], axis=1)
    return tuple(jnp.concatenate([m, m], axis=1) for m in (c64, a64, b64))


def _pad_w_in(w_in):
    d = w_in.shape[0]
    return jnp.concatenate(
        [w_in[:, :COL_KI + IDX_DIM], jnp.zeros((d, LANES - IDX_DIM), F32),
         w_in[:, COL_KI + IDX_DIM:], jnp.zeros((d, LANES - N_IDX_HEADS), F32)], axis=1).astype(BF16)


def _layer_group(x, mods, pos, conv_hist, ffn_hist, past, w, *, per_row_mod, tm_in, tt_conv, tq, tm_post, tag):
    (g1, w_in_p, dw_w, dw_b, ln_g, ln_b, w_out, g2, w_up, fdw_w, fdw_b, w_down, gf) = w
    b, t, d = x.shape
    r = b * t
    x2d = x.reshape(r, d)
    if per_row_mod:
        sh1, sc1, gt1, sh2, sc2, gt2 = [jnp.repeat(m, t, axis=0) for m in mods]
        tabs = tuple(jnp.tile(m, (b, 1)) for m in _rope_tables(pos))
    else:
        sh1, sc1, gt1, sh2, sc2, gt2 = [m[:, None, :] for m in mods]
        tabs = _rope_tables(pos)

    glu, q, k, v, qi, ki, wi = _inproj_call(x2d, sc1, sh1, g1, w_in_p, tabs, tm_in, "inproj_" + tag)
    glu3 = glu.reshape(b, t, CONV_DIM)
    k3, v3, ki3 = k.reshape(b, t, KV_DIM), v.reshape(b, t, KV_DIM), ki.reshape(b, t, IDX_DIM)

    conv_out = _conv_call(glu3, conv_hist, dw_w, dw_b, ln_g, ln_b, tt_conv, "conv_" + tag)
    new_conv = glu3[:, t - CONV_HIST:, :]

    if past is None:
        past_len = 0
        k_all, v_all, ki_all = k3.astype(BF16), v3.astype(BF16), ki3.astype(BF16)
    else:
        ck, cv, cki = past
        past_len = ck.shape[1]
        k_all = jnp.concatenate([ck.astype(BF16).reshape(b, past_len, KV_DIM), k3.astype(BF16)], axis=1)
        v_all = jnp.concatenate([cv.astype(BF16).reshape(b, past_len, KV_DIM), v3.astype(BF16)], axis=1)
        ki_all = jnp.concatenate([cki.astype(BF16), ki3.astype(BF16)], axis=1)
    n_keys = past_len + t
    k_sel = min(TOPK_MAX, n_keys // 4)
    pad = (-n_keys) % LANES
    if pad:
        k_all, v_all, ki_all = [jnp.pad(a, ((0, 0), (0, pad), (0, 0))) for a in (k_all, v_all, ki_all)]
    q3, qi3, wi3 = q.reshape(b, t, ATTN_DIM), qi.reshape(b, t, -1), wi.reshape(b, t, -1)
    if past is None:
        wit3, vt_all = jnp.swapaxes(wi3, 1, 2), jnp.swapaxes(v_all, 1, 2)
        pieces = [_attn_t_call(q3, qi3, wit3, k_all, vt_all, ki_all, tq, i, 1, (i + 1) * tq, t, 0, k_sel,
                               "attn_%s%d" % (tag, i)) for i in range(t // tq)]
        attn_out = jnp.concatenate(pieces, axis=2)
    else:
        kit_all = jnp.swapaxes(ki_all, 1, 2)
        attn_out = _attn_call(q3, qi3, wi3, k_all, v_all, kit_all, tq, 0, t // tq, n_keys + pad, t, past_len, k_sel,
                              "attn_" + tag).reshape(r, ATTN_DIM)

    y, new_ffn = _post_call(x2d, conv_out.reshape(r, CONV_DIM), attn_out,
                            gt1, sc2, sh2, gt2, g2, gf, w_out, w_up, w_down, fdw_w, fdw_b, ffn_hist,
                            tm_post, t, "post_" + tag)
    return (y.reshape(b, t, d), k3.reshape(b, t, N_KV_HEADS, HEAD_DIM), v3.reshape(b, t, N_KV_HEADS, HEAD_DIM),
            ki3, new_conv, new_ffn)


def kernel(x_prompt, x_sample, cache_k, cache_v, cache_kidx, state_conv, state_ffn_conv, c_prompt, c_sample,
           w_ada, b_ada, norm1_g, w_in, conv_dw_w, conv_dw_b, conv_ln_g, conv_ln_b, w_out, norm2_g,
           w_up, ffn_dw_w, ffn_dw_b, w_down, final_norm_g):
    depth = w_ada.shape[0]
    assert depth == 1, "the final norm is fused into the single layer's last kernel"
    bp, sp, d = x_prompt.shape
    bs, ts, _ = x_sample.shape
    past_len = cache_k.shape[2]
    l = 0
    mod = _mod_call(jnp.concatenate([c_prompt, c_sample], axis=0), w_ada[l], b_ada[l])
    mods_p = jnp.split(mod[:bp], 6, axis=-1)
    mods_s = jnp.split(mod[bp:], 6, axis=-1)
    w = (norm1_g[l].reshape(1, d), _pad_w_in(w_in[l]), conv_dw_w[l], conv_dw_b[l], conv_ln_g[l], conv_ln_b[l],
         w_out[l].astype(BF16), norm2_g[l].reshape(1, d), w_up[l].astype(BF16), ffn_dw_w[l],
         ffn_dw_b[l].reshape(1, -1), w_down[l].astype(BF16), final_norm_g.reshape(1, d))

    conv0 = jnp.zeros((bp, CONV_HIST, CONV_DIM), F32)
    ffn0 = jnp.zeros((bp, FFN_CONV_WIDTH - 1, 2 * D_FF), F32)
    out_p = _layer_group(x_prompt, mods_p, jnp.arange(sp, dtype=I32), conv0, ffn0, None, w,
                         per_row_mod=False, tm_in=512, tt_conv=128, tq=256, tm_post=512, tag="p")
    out_s = _layer_group(x_sample, mods_s, past_len + jnp.arange(ts, dtype=I32), state_conv[l], state_ffn_conv[l],
                         (cache_k[l], cache_v[l], cache_kidx[l]), w,
                         per_row_mod=True, tm_in=bs * ts, tt_conv=ts, tq=ts, tm_post=bs * ts, tag="s")
    y_p, k_p, v_p, ki_p, conv_p, ffn_p = out_p
    y_s, k_s, v_s, ki_s, conv_s, ffn_s = out_s
    st = lambda a: a[None]
    return (y_p, y_s, st(k_p), st(v_p), st(ki_p), st(conv_p), st(ffn_p),
            st(k_s), st(v_s), st(ki_s), st(conv_s), st(ffn_s))
```

```python
import functools

import jax
import jax.numpy as jnp
from jax import lax
from jax.experimental import pallas as pl
from jax.experimental.pallas import tpu as pltpu

F32 = jnp.float32
BF16 = jnp.bfloat16
I32 = jnp.int32
I16 = jnp.int16

D_MODEL = 1024
CHUNK = 64
CONV_DIM = 512
CONV_WIDTH = 31
N_HEADS = 8
HEAD_DIM = 64
N_KV_HEADS = 2
ATTN_DIM = N_HEADS * HEAD_DIM
KV_DIM = N_KV_HEADS * HEAD_DIM
ROT_DIM = HEAD_DIM // 4
ROPE_THETA = 500000.0
N_IDX_HEADS = 8
IDX_DIM = 32
TOPK_MAX = 256
D_FF = 2816
FFN_CONV_WIDTH = 3
EPS = 1e-6

LANES = 128
SUBLANES = 8
MXU_N = 256
VMEM_LIMIT = 52 * 1024 * 1024

COL_U, COL_UG, COL_Q, COL_K, COL_V, COL_QI = 0, 512, 1024, 1536, 1664, 1792
COL_KI = 2048
COL_WI = COL_KI + LANES
IN_PAD = COL_WI + LANES

Q_SCALE = HEAD_DIM ** -0.5 * 1.4426950408889634

INT_MIN = -2147483648
KEY_NEG_INF = -2139095040


def _order_key(x):
    bits = pltpu.bitcast(x, I32)
    return jnp.where(bits < 0, INT_MIN - bits, bits)
IDX_BITS = 13


def _cparams(sem):
    return pltpu.CompilerParams(dimension_semantics=sem, vmem_limit_bytes=VMEM_LIMIT)


def _silu(x):
    return x * jax.nn.sigmoid(x)


def _mod_kernel(c_ref, w_ref, b_ref, o_ref):
    s = _silu(c_ref[...]).astype(BF16)
    o_ref[...] = jnp.dot(s, w_ref[...].astype(BF16), preferred_element_type=F32) + b_ref[...]


def _mod_call(c_all, w_ada, b_ada):
    nb, d = c_all.shape
    n = w_ada.shape[1]
    tn = 512
    return pl.pallas_call(
        _mod_kernel,
        grid=(n // tn,),
        in_specs=[pl.BlockSpec((nb, d), lambda j: (0, 0)),
                  pl.BlockSpec((d, tn), lambda j: (0, j)),
                  pl.BlockSpec((1, tn), lambda j: (0, j))],
        out_specs=pl.BlockSpec((nb, tn), lambda j: (0, j)),
        out_shape=jax.ShapeDtypeStruct((nb, n), F32),
        compiler_params=_cparams(("arbitrary",)),
        name="mod",
    )(c_all, w_ada, b_ada.reshape(1, n))


def _rmsnorm_mod(x, g, sc, sh):
    ms = jnp.mean(x * x, axis=-1, keepdims=True)
    return (x * lax.rsqrt(ms + EPS) * g) * (1.0 + sc) + sh


def _inproj_kernel(x_ref, sc_ref, sh_ref, g_ref, w_ref, cos_ref, sa_ref, sb_ref,
                   glu_ref, q_ref, k_ref, v_ref, qi_ref, ki_ref, wi_ref):
    h = _rmsnorm_mod(x_ref[...], g_ref[...], sc_ref[...], sh_ref[...])
    z = jnp.dot(h.astype(BF16), w_ref[...], preferred_element_type=F32)
    glu_ref[...] = z[:, COL_U:COL_U + CONV_DIM] * jax.nn.sigmoid(z[:, COL_UG:COL_UG + CONV_DIM])
    cos, sa, sb = cos_ref[...], sa_ref[...], sb_ref[...]

    def rope(xs):
        return (xs * cos + pltpu.roll(xs, ROT_DIM // 2, axis=1) * sa
                + pltpu.roll(xs, LANES - ROT_DIM // 2, axis=1) * sb)

    for j in range(ATTN_DIM // LANES):
        c0 = COL_Q + LANES * j
        q_ref[:, LANES * j:LANES * (j + 1)] = (rope(z[:, c0:c0 + LANES]) * Q_SCALE).astype(BF16)
    k_ref[...] = rope(z[:, COL_K:COL_K + KV_DIM])
    v_ref[...] = z[:, COL_V:COL_V + KV_DIM]
    qi_ref[...] = z[:, COL_QI:COL_QI + N_IDX_HEADS * IDX_DIM].astype(BF16)
    ki_ref[...] = z[:, COL_KI:COL_KI + IDX_DIM]
    wi_ref[...] = z[:, COL_WI:COL_WI + N_IDX_HEADS]


def _inproj_call(x2, sc, sh, g1, w_in_p, tabs, tm, name):
    r, d = x2.shape
    nt = r // tm
    if sc.ndim == 3:
        tpb = nt // sc.shape[0]
        mod_spec = pl.BlockSpec((None, 1, d), lambda i: (i // tpb, 0, 0))
    else:
        mod_spec = pl.BlockSpec((tm, d), lambda i: (i, 0))
    ntab = tabs[0].shape[0] // tm
    tab_spec = pl.BlockSpec((tm, LANES), lambda i: (i % ntab, 0))

    def rows(width):
        return pl.BlockSpec((tm, width), lambda i: (i, 0))

    out_shapes = (jax.ShapeDtypeStruct((r, CONV_DIM), F32), jax.ShapeDtypeStruct((r, ATTN_DIM), BF16),
                  jax.ShapeDtypeStruct((r, KV_DIM), F32), jax.ShapeDtypeStruct((r, KV_DIM), F32),
                  jax.ShapeDtypeStruct((r, N_IDX_HEADS * IDX_DIM), BF16),
                  jax.ShapeDtypeStruct((r, IDX_DIM), F32), jax.ShapeDtypeStruct((r, N_IDX_HEADS), F32))
    return pl.pallas_call(
        _inproj_kernel,
        grid=(nt,),
        in_specs=[rows(d), mod_spec, mod_spec,
                  pl.BlockSpec((1, d), lambda i: (0, 0)),
                  pl.BlockSpec((d, IN_PAD), lambda i: (0, 0)),
                  tab_spec, tab_spec, tab_spec],
        out_specs=(rows(CONV_DIM), rows(ATTN_DIM), rows(KV_DIM), rows(KV_DIM),
                   rows(N_IDX_HEADS * IDX_DIM), rows(IDX_DIM), rows(N_IDX_HEADS)),
        out_shape=out_shapes,
        compiler_params=_cparams(("arbitrary",)),
        name=name,
    )(x2, sc, sh, g1, w_in_p, *tabs)


CONV_HIST = CONV_WIDTH - 1
CONV_PAD = 32
CONV_RB = 64


def _conv_kernel(glu_ref, hist_ref, w_ref, b_ref, lg_ref, lb_ref, o_ref, ext_ref, y_ref, *, tt):
    t = pl.program_id(1)

    @pl.when(t == 0)
    def _():
        ext_ref[0:CONV_PAD - CONV_HIST, :] = jnp.zeros((CONV_PAD - CONV_HIST, CONV_DIM), F32)
        ext_ref[CONV_PAD - CONV_HIST:CONV_PAD, :] = hist_ref[...]

    @pl.when(t > 0)
    def _():
        ext_ref[0:CONV_PAD, :] = ext_ref[tt:tt + CONV_PAD, :]

    ext_ref[CONV_PAD:CONV_PAD + tt, :] = glu_ref[...]

    off = CONV_PAD - CONV_HIST
    rb = min(CONV_RB, tt)
    for r0 in range(0, tt, rb):
        for c0 in range(0, CONV_DIM, LANES):
            acc = None
            for b in range(SUBLANES):
                span = (CONV_WIDTH - 1 - b) // SUBLANES * SUBLANES
                mis = (off + b) % SUBLANES
                base = off + b - mis + r0
                cover = -(-(mis + rb + span) // SUBLANES) * SUBLANES
                win = ext_ref[base:base + cover, c0:c0 + LANES]
                if mis:
                    win = pltpu.roll(win, cover - mis, axis=0)
                for j in range(b, CONV_WIDTH, SUBLANES):
                    term = win[j - b:j - b + rb, :] * w_ref[j:j + 1, c0:c0 + LANES]
                    acc = term if acc is None else acc + term
            y_ref[r0:r0 + rb, c0:c0 + LANES] = acc + b_ref[:, c0:c0 + LANES]

    y = y_ref[...]
    mu = jnp.mean(y, axis=-1, keepdims=True)
    yc = y - mu
    var = jnp.mean(yc * yc, axis=-1, keepdims=True)
    o_ref[...] = _silu(yc * lax.rsqrt(var + EPS) * lg_ref[...] + lb_ref[...]).astype(BF16)


def _conv_call(glu3, hist, dw_w, dw_b, ln_g, ln_b, tt, name):
    b, t, c = glu3.shape
    vec = pl.BlockSpec((1, c), lambda i, j: (0, 0))
    return pl.pallas_call(
        functools.partial(_conv_kernel, tt=tt),
        grid=(b, t // tt),
        in_specs=[pl.BlockSpec((None, tt, c), lambda i, j: (i, j, 0)),
                  pl.BlockSpec((None, CONV_HIST, c), lambda i, j: (i, 0, 0)),
                  pl.BlockSpec((CONV_WIDTH, c), lambda i, j: (0, 0)),
                  vec, vec, vec],
        out_specs=pl.BlockSpec((None, tt, c), lambda i, j: (i, j, 0)),
        out_shape=jax.ShapeDtypeStruct((b, t, c), BF16),
        scratch_shapes=[pltpu.VMEM((tt + CONV_PAD, c), F32), pltpu.VMEM((tt, c), F32)],
        compiler_params=_cparams(("arbitrary", "arbitrary")),
        name=name,
    )(glu3, hist, dw_w, dw_b.reshape(1, c), ln_g.reshape(1, c), ln_b.reshape(1, c))


CNT_ROWS = 64


def _attn_kernel(q_ref, qi_ref, wi_ref, k_ref, v_ref, kit_ref, o_ref, key_ref, bias_ref,
                 *, nb, tq, n_keys, q0, t_new, past, k_sel):
    j = pl.program_id(1)
    rows = nb * tq
    row = lax.broadcasted_iota(I32, (rows, 1), 0) % tq + (q0 + j * tq)
    limit = past + jnp.minimum((row // CHUNK + 1) * CHUNK, t_new)
    col = lax.broadcasted_iota(I32, (rows, n_keys), 1)
    adm = col < limit

    for bb in range(nb):
        kit = kit_ref[bb]
        wi = wi_ref[bb]
        score = jnp.zeros((tq, n_keys), F32)
        for h in range(N_IDX_HEADS):
            d = jnp.dot(qi_ref[bb, :, IDX_DIM * h:IDX_DIM * (h + 1)], kit, preferred_element_type=F32)
            score = score + wi[:, h:h + 1] * jnp.maximum(d, 0.0)
        key_ref[bb * tq:(bb + 1) * tq, :] = _order_key(jnp.where(adm[bb * tq:(bb + 1) * tq, :], score, -jnp.inf))

    rg = min(CNT_ROWS, rows)

    def count_ge(cands):
        outs = [[] for _ in cands]
        for r0 in range(0, rows, rg):
            cs = [jnp.broadcast_to(c[r0:r0 + rg, :], (rg, LANES)) for c in cands]
            accs = [jnp.zeros((rg, LANES), F32) for _ in cands]
            for c0 in range(0, n_keys, LANES):
                kv = key_ref[r0:r0 + rg, c0:c0 + LANES]
                accs = [a + jnp.where(kv >= c, 1.0, 0.0) for a, c in zip(accs, cs)]
            for o, a in zip(outs, accs):
                o.append(jnp.sum(a, axis=1, keepdims=True))
        return [o[0] if len(o) == 1 else jnp.concatenate(o, axis=0) for o in outs]

    bpp = 2 if rows <= 32 else 1

    def bits_step(it, r):
        sh = 32 - bpp * (it + 1)
        step = jnp.left_shift(jnp.int32(1), sh)
        counts = count_ge([r + m * step for m in range(1, 1 << bpp)])
        inc = sum(jnp.where(n >= k_sel, 1, 0) for n in counts)
        return r + jnp.left_shift(inc, sh)

    thr = lax.fori_loop(0, 32 // bpp, bits_step, jnp.full((rows, 1), INT_MIN, I32))

    keys = key_ref[...]
    ge = keys >= thr
    cnt_ge = jnp.sum(jnp.where(ge, 1.0, 0.0), axis=1, keepdims=True)
    bias_ref[...] = jnp.where(ge & adm, 0.0, -jnp.inf)
    split_tie = (cnt_ge > k_sel) & (thr != KEY_NEG_INF)
    any_split = jnp.max(jnp.where(split_tie, 1.0, 0.0)) > 0.0

    @pl.when(any_split)
    def _():
        gt = keys > thr
        eq = keys == thr
        need = k_sel - jnp.sum(jnp.where(gt, 1.0, 0.0), axis=1, keepdims=True)

        def idx_step(it, last):
            cand = last + jnp.left_shift(jnp.int32(1), IDX_BITS - 1 - it)
            below = jnp.sum(jnp.where(eq & (col < cand), 1.0, 0.0), axis=1, keepdims=True)
            return jnp.where(below < need, cand, last)

        last = lax.fori_loop(0, IDX_BITS, idx_step, jnp.zeros((rows, 1), I32))
        sel = (gt | (eq & (col <= last))) & adm
        bias_ref[...] = jnp.where(sel, 0.0, -jnp.inf)

    rep = N_HEADS // N_KV_HEADS
    for bb in range(nb):
        bias = bias_ref[bb * tq:(bb + 1) * tq, :]
        for g in range(N_KV_HEADS):
            qg = jnp.concatenate([q_ref[bb, :, HEAD_DIM * (rep * g + r):HEAD_DIM * (rep * g + r + 1)]
                                  for r in range(rep)], axis=0)
            logits = lax.dot_general(qg, k_ref[bb, :, HEAD_DIM * g:HEAD_DIM * (g + 1)],
                                     (((1,), (1,)), ((), ())), preferred_element_type=F32)
            ps, ss = [], []
            for r in range(rep):
                lg = logits[r * tq:(r + 1) * tq, :] + bias
                p = jnp.exp2(lg - jnp.max(lg, axis=1, keepdims=True))
                ss.append(jnp.sum(p, axis=1, keepdims=True))
                ps.append(p.astype(BF16))
            o = jnp.dot(jnp.concatenate(ps, axis=0), v_ref[bb, :, HEAD_DIM * g:HEAD_DIM * (g + 1)],
                        preferred_element_type=F32)
            for r in range(rep):
                h = rep * g + r
                o_ref[bb, :, HEAD_DIM * h:HEAD_DIM * (h + 1)] = (o[r * tq:(r + 1) * tq, :] / ss[r]).astype(BF16)


def _attn_call(q3, qi3, wi3, k_all, v_all, kit_all, nb, tq, q_blk0, n_q_blk, n_keys, t_new, past, k_sel, name):
    b = q3.shape[0]
    assert n_keys % LANES == 0 and n_keys <= (1 << IDX_BITS) and n_keys <= k_all.shape[1] and b % nb == 0
    kern = functools.partial(_attn_kernel, nb=nb, tq=tq, n_keys=n_keys, q0=q_blk0 * tq, t_new=t_new, past=past,
                             k_sel=float(k_sel))
    return pl.pallas_call(
        kern,
        grid=(b // nb, n_q_blk),
        in_specs=[pl.BlockSpec((nb, tq, ATTN_DIM), lambda i, j: (i, q_blk0 + j, 0)),
                  pl.BlockSpec((nb, tq, N_IDX_HEADS * IDX_DIM), lambda i, j: (i, q_blk0 + j, 0)),
                  pl.BlockSpec((nb, tq, N_IDX_HEADS), lambda i, j: (i, q_blk0 + j, 0)),
                  pl.BlockSpec((nb, n_keys, KV_DIM), lambda i, j: (i, 0, 0)),
                  pl.BlockSpec((nb, n_keys, KV_DIM), lambda i, j: (i, 0, 0)),
                  pl.BlockSpec((nb, IDX_DIM, n_keys), lambda i, j: (i, 0, 0))],
        out_specs=pl.BlockSpec((nb, tq, ATTN_DIM), lambda i, j: (i, j, 0)),
        out_shape=jax.ShapeDtypeStruct((b, n_q_blk * tq, ATTN_DIM), BF16),
        scratch_shapes=[pltpu.VMEM((nb * tq, n_keys), I32), pltpu.VMEM((nb * tq, n_keys), F32)],
        compiler_params=_cparams(("arbitrary", "arbitrary")),
        name=name,
    )(q3, qi3, wi3, k_all, v_all, kit_all)


VT_ROWS = HEAD_DIM + 16
SCORE_KEYS = 256
CNT_KEYS = 64


def _attn_t_kernel(q_ref, qi_ref, wit_ref, k_ref, vt_ref, ki_ref, o_ref, key_ref, bias_ref, hi_ref, lo_ref,
                   *, tq, n_keys, q0, t_new, past, k_sel):
    j = pl.program_id(1)
    qpos = lax.broadcasted_iota(I32, (1, tq), 1) + (q0 + j * tq)
    limit = past + jnp.minimum((qpos // CHUNK + 1) * CHUNK, t_new)
    nt = (((1,), (1,)), ((), ()))

    qi_stack = jnp.concatenate([qi_ref[:, IDX_DIM * h:IDX_DIM * (h + 1)] for h in range(N_IDX_HEADS)], axis=0)
    wit = wit_ref[...]
    kc = min(SCORE_KEYS, n_keys)
    for c0 in range(0, n_keys, kc):
        d = lax.dot_general(ki_ref[c0:c0 + kc, :], qi_stack, nt, preferred_element_type=F32)
        sc = wit[0:1, :] * jnp.maximum(d[:, 0:tq], 0.0)
        for h in range(1, N_IDX_HEADS):
            sc = sc + wit[h:h + 1, :] * jnp.maximum(d[:, h * tq:(h + 1) * tq], 0.0)
        kpos = lax.broadcasted_iota(I32, (kc, tq), 0) + c0
        kc_keys = _order_key(jnp.where(kpos < limit, sc, -jnp.inf))
        key_ref[c0:c0 + kc, :] = kc_keys
        hi_ref[c0:c0 + kc, :] = (kc_keys >> 16).astype(I16)
        lo_ref[c0:c0 + kc, :] = (kc_keys ^ 0x8000).astype(I16)

    sl = min(CNT_KEYS, n_keys)

    def search16(ref):
        def bit_step(it, r):
            cand = r + jnp.left_shift(jnp.int32(1), 15 - it)
            cb = jnp.broadcast_to((cand - 32768).astype(I16), (sl, tq))
            acc = jnp.zeros((sl, tq), I16)
            for c0 in range(0, n_keys, sl):
                acc = acc + jnp.where(ref[c0:c0 + sl, :] >= cb, I16(1), I16(0))
            cnt = jnp.sum(acc.astype(I32), axis=0, keepdims=True)
            return jnp.where(cnt >= k_sel, cand, r)
        return lax.fori_loop(0, 16, bit_step, jnp.zeros((1, tq), I32))

    r_hi = search16(hi_ref)
    h16 = (r_hi - 32768).astype(I16)
    hi = hi_ref[...]
    lo_ref[...] = jnp.where(hi == h16, lo_ref[...], jnp.where(hi > h16, I16(32767), I16(-32768)))
    thr = ((r_hi - 32768) << 16) | search16(lo_ref)

    keys = key_ref[...]
    kpos = lax.broadcasted_iota(I32, (n_keys, tq), 0)
    adm = kpos < limit
    ge = keys >= thr
    cnt_ge = jnp.sum(jnp.where(ge, 1.0, 0.0), axis=0, keepdims=True)
    bias_ref[...] = jnp.where(ge & adm, 0.0, -jnp.inf)
    split_tie = (cnt_ge > k_sel) & (thr != KEY_NEG_INF)
    any_split = jnp.max(jnp.where(split_tie, 1.0, 0.0)) > 0.0

    @pl.when(any_split)
    def _():
        gt = keys > thr
        eq = keys == thr
        need = k_sel - jnp.sum(jnp.where(gt, 1.0, 0.0), axis=0, keepdims=True)

        def idx_step(it, last):
            cand = last + jnp.left_shift(jnp.int32(1), IDX_BITS - 1 - it)
            below = jnp.sum(jnp.where(eq & (kpos < cand), 1.0, 0.0), axis=0, keepdims=True)
            return jnp.where(below < need, cand, last)

        last = lax.fori_loop(0, IDX_BITS, idx_step, jnp.zeros((1, tq), I32))
        sel = (gt | (eq & (kpos <= last))) & adm
        bias_ref[...] = jnp.where(sel, 0.0, -jnp.inf)

    bias = bias_ref[...]
    rep = N_HEADS // N_KV_HEADS
    for g in range(N_KV_HEADS):
        qg = jnp.concatenate([q_ref[:, HEAD_DIM * (rep * g + r):HEAD_DIM * (rep * g + r + 1)] for r in range(rep)],
                             axis=0)
        logits = lax.dot_general(k_ref[:, HEAD_DIM * g:HEAD_DIM * (g + 1)], qg, nt, preferred_element_type=F32)
        ps = []
        for r in range(rep):
            lg = logits[:, r * tq:(r + 1) * tq] + bias
            ps.append(jnp.exp2(lg - jnp.max(lg, axis=0, keepdims=True)).astype(BF16))
        o = jnp.dot(vt_ref[g], jnp.concatenate(ps, axis=1), preferred_element_type=F32)
        o = o[0:HEAD_DIM, :] / o[HEAD_DIM:HEAD_DIM + 1, :]
        for r in range(rep):
            h = rep * g + r
            o_ref[HEAD_DIM * h:HEAD_DIM * (h + 1), :] = o[:, r * tq:(r + 1) * tq].astype(BF16)


def _attn_t_call(q3, qi3, wit3, k_all, vt_all, ki_all, tq, q_blk0, n_q_blk, n_keys, t_new, past, k_sel, name):
    b = q3.shape[0]
    assert n_keys % LANES == 0 and n_keys <= (1 << IDX_BITS) and n_keys <= k_all.shape[1] and tq % LANES == 0
    kern = functools.partial(_attn_t_kernel, tq=tq, n_keys=n_keys, q0=q_blk0 * tq, t_new=t_new, past=past,
                             k_sel=float(k_sel))
    return pl.pallas_call(
        kern,
        grid=(b, n_q_blk),
        in_specs=[pl.BlockSpec((None, tq, ATTN_DIM), lambda i, j: (i, q_blk0 + j, 0)),
                  pl.BlockSpec((None, tq, N_IDX_HEADS * IDX_DIM), lambda i, j: (i, q_blk0 + j, 0)),
                  pl.BlockSpec((None, N_IDX_HEADS, tq), lambda i, j: (i, 0, q_blk0 + j)),
                  pl.BlockSpec((None, n_keys, KV_DIM), lambda i, j: (i, 0, 0)),
                  pl.BlockSpec((None, N_KV_HEADS, VT_ROWS, n_keys), lambda i, j: (i, 0, 0, 0)),
                  pl.BlockSpec((None, n_keys, IDX_DIM), lambda i, j: (i, 0, 0))],
        out_specs=pl.BlockSpec((None, ATTN_DIM, tq), lambda i, j: (i, 0, j)),
        out_shape=jax.ShapeDtypeStruct((b, ATTN_DIM, n_q_blk * tq), BF16),
        scratch_shapes=[pltpu.VMEM((n_keys, tq), I32), pltpu.VMEM((n_keys, tq), F32),
                        pltpu.VMEM((n_keys, tq), I16), pltpu.VMEM((n_keys, tq), I16)],
        compiler_params=_cparams(("arbitrary", "arbitrary")),
        name=name,
    )(q3, qi3, wit3, k_all, vt_all, ki_all)


FF_CHUNKS = ((0, 1024), (1024, 1024), (2048, 768))


def _post_kernel(x_ref, conv_ref, attn_ref, gt1_ref, sc2_ref, sh2_ref, gt2_ref, g2_ref, gf_ref,
                 wout_ref, wup_ref, wdn_ref, fw_ref, fb_ref, hist_ref,
                 y_ref, newffn_ref, *scratch, tm, t_len, tiles_per_batch, attn_transposed):
    nseg = max(1, tm // t_len)
    seg = tm // nseg
    i = pl.program_id(0)
    if nseg == 1:
        carry_ref = scratch[0]

        @pl.when(i % tiles_per_batch == 0)
        def _():
            carry_ref[...] = hist_ref[0]

    w_attn = wout_ref[CONV_DIM:CONV_DIM + ATTN_DIM, :]
    if attn_transposed:
        mix_attn = lax.dot_general(attn_ref[...], w_attn, (((0,), (0,)), ((), ())), preferred_element_type=F32)
    else:
        mix_attn = jnp.dot(attn_ref[...], w_attn, preferred_element_type=F32)
    mix = jnp.dot(conv_ref[...], wout_ref[0:CONV_DIM, :], preferred_element_type=F32) + mix_attn
    x1 = x_ref[...] + gt1_ref[...] * mix
    h2 = _rmsnorm_mod(x1, g2_ref[...], sc2_ref[...], sh2_ref[...]).astype(BF16)

    row = lax.broadcasted_iota(I32, (SUBLANES, 1), 0)

    def causal3(u, col0, width):
        w0 = fw_ref[0:1, col0:col0 + width]
        w1 = fw_ref[1:2, col0:col0 + width]
        w2 = fw_ref[2:3, col0:col0 + width]
        outs = []
        for s in range(nseg):
            us = u[s * seg:(s + 1) * seg, :]
            if nseg == 1:
                h0 = carry_ref[0:1, col0:col0 + width]
                h1 = carry_ref[1:2, col0:col0 + width]
            else:
                h0 = hist_ref[s, 0:1, col0:col0 + width]
                h1 = hist_ref[s, 1:2, col0:col0 + width]
            p1 = pltpu.roll(us, 1, axis=0)
            p2 = pltpu.roll(us, 2, axis=0)
            p1 = jnp.concatenate([jnp.where(row == 0, h1, p1[0:SUBLANES, :]), p1[SUBLANES:, :]], axis=0)
            p2 = jnp.concatenate([jnp.where(row == 0, h0, jnp.where(row == 1, h1, p2[0:SUBLANES, :])),
                                  p2[SUBLANES:, :]], axis=0)
            outs.append(us * w2 + p1 * w1 + p2 * w0 + fb_ref[:, col0:col0 + width])
            if nseg == 1:
                carry_ref[:, col0:col0 + width] = us[seg - 2:seg, :]
            else:
                newffn_ref[s, :, col0:col0 + width] = us[seg - 2:seg, :]
        return outs[0] if nseg == 1 else jnp.concatenate(outs, axis=0)

    acc = jnp.zeros((tm, D_MODEL), F32)
    for c0, cw in FF_CHUNKS:
        ua = jnp.dot(h2, wup_ref[:, c0:c0 + cw], preferred_element_type=F32)
        ug = jnp.dot(h2, wup_ref[:, D_FF + c0:D_FF + c0 + cw], preferred_element_type=F32)
        a = causal3(ua, c0, cw)
        g = causal3(ug, D_FF + c0, cw)
        acc = acc + jnp.dot((a * _silu(g)).astype(BF16), wdn_ref[c0:c0 + cw, :], preferred_element_type=F32)

    if nseg == 1:
        @pl.when(i % tiles_per_batch == tiles_per_batch - 1)
        def _():
            newffn_ref[0] = carry_ref[...]

    x2 = x1 + gt2_ref[...] * acc
    ms = jnp.mean(x2 * x2, axis=-1, keepdims=True)
    y_ref[...] = x2 * lax.rsqrt(ms + EPS) * gf_ref[...]


def _post_call(x2d, conv2d, attn, gt1, sc2, sh2, gt2, g2, gf, wout, wup, wdn, fw, fb, hist, tm, t_len, name):
    r, d = x2d.shape
    nt = r // tm
    nb = hist.shape[0]
    nseg = max(1, tm // t_len)
    tpb = max(1, t_len // tm)
    assert nt * nseg == nb * tpb and t_len >= FFN_CONV_WIDTH - 1
    if gt1.ndim == 3:
        mod_spec = pl.BlockSpec((None, 1, d), lambda i: (i // tpb, 0, 0))
    else:
        mod_spec = pl.BlockSpec((tm, d), lambda i: (i, 0))

    def rows(width):
        return pl.BlockSpec((tm, width), lambda i: (i, 0))

    def const(shape):
        return pl.BlockSpec(shape, lambda i: (0,) * len(shape), pipeline_mode=pl.Buffered(1))

    state_spec = pl.BlockSpec((nseg, FFN_CONV_WIDTH - 1, 2 * D_FF), lambda i: (i * nseg // tpb, 0, 0))
    attn_t = attn.ndim == 3
    if attn_t:
        assert nseg == 1
        attn_spec = pl.BlockSpec((None, ATTN_DIM, tm), lambda i: (i // tpb, 0, i % tpb))
    else:
        attn_spec = rows(ATTN_DIM)
    kern = functools.partial(_post_kernel, tm=tm, t_len=t_len, tiles_per_batch=tpb, attn_transposed=attn_t)
    scratch = [pltpu.VMEM((FFN_CONV_WIDTH - 1, 2 * D_FF), F32)] if nseg == 1 else []
    return pl.pallas_call(
        kern,
        grid=(nt,),
        in_specs=[rows(d), rows(CONV_DIM), attn_spec, mod_spec, mod_spec, mod_spec, mod_spec,
                  const((1, d)), const((1, d)),
                  const((CONV_DIM + ATTN_DIM, d)), const((d, 2 * D_FF)), const((D_FF, d)),
                  const((FFN_CONV_WIDTH, 2 * D_FF)), const((1, 2 * D_FF)),
                  state_spec],
        out_specs=(rows(d), state_spec),
        out_shape=(jax.ShapeDtypeStruct((r, d), F32),
                   jax.ShapeDtypeStruct((nb, FFN_CONV_WIDTH - 1, 2 * D_FF), F32)),
        scratch_shapes=scratch,
        compiler_params=_cparams(("arbitrary",)),
        name=name,
    )(x2d, conv2d, attn, gt1, sc2, sh2, gt2, g2, gf, wout, wup, wdn, fw, fb, hist)


def _rope_tables(pos):
    half = ROT_DIM // 2
    inv = 1.0 / (ROPE_THETA ** (jnp.arange(0, ROT_DIM, 2, dtype=F32) / ROT_DIM))
    ang = pos.astype(F32)[:, None] * inv[None, :]
    cos, sin = jnp.cos(ang), jnp.sin(ang)
    t = pos.shape[0]
    rest1 = jnp.ones((t, HEAD_DIM - ROT_DIM), F32)
    rest0 = jnp.zeros((t, HEAD_DIM - ROT_DIM), F32)
    z = jnp.zeros((t, half), F32)
    c64 = jnp.concatenate([cos, cos, rest1], axis=1)
    a64 = jnp.concatenate([z, sin, rest0], axis=1)
    b64 = jnp.concatenate([-sin, z, rest0], axis=1)
    return tuple(jnp.concatenate([m, m], axis=1) for m in (c64, a64, b64))


def _pad_w_in(w_in):
    d = w_in.shape[0]
    return jnp.concatenate(
        [w_in[:, :COL_KI + IDX_DIM], jnp.zeros((d, LANES - IDX_DIM), F32),
         w_in[:, COL_KI + IDX_DIM:], jnp.zeros((d, LANES - N_IDX_HEADS), F32)], axis=1).astype(BF16)


def _layer_group(x, mods, pos, conv_hist, ffn_hist, past, w, *, per_row_mod, tm_in, tt_conv, tq, tm_post, tag, nb_attn=1):
    (g1, w_in_p, dw_w, dw_b, ln_g, ln_b, w_out, g2, w_up, fdw_w, fdw_b, w_down, gf) = w
    b, t, d = x.shape
    r = b * t
    x2d = x.reshape(r, d)
    if per_row_mod:
        sh1, sc1, gt1, sh2, sc2, gt2 = [jnp.repeat(m, t, axis=0) for m in mods]
        tabs = tuple(jnp.tile(m, (b, 1)) for m in _rope_tables(pos))
    else:
        sh1, sc1, gt1, sh2, sc2, gt2 = [m[:, None, :] for m in mods]
        tabs = _rope_tables(pos)

    glu, q, k, v, qi, ki, wi = _inproj_call(x2d, sc1, sh1, g1, w_in_p, tabs, tm_in, "inproj_" + tag)
    glu3 = glu.reshape(b, t, CONV_DIM)
    k3, v3, ki3 = k.reshape(b, t, KV_DIM), v.reshape(b, t, KV_DIM), ki.reshape(b, t, IDX_DIM)

    conv_out = _conv_call(glu3, conv_hist, dw_w, dw_b, ln_g, ln_b, tt_conv, "conv_" + tag)
    new_conv = glu3[:, t - CONV_HIST:, :]

    if past is None:
        past_len = 0
        k_all, v_all, ki_all = k3.astype(BF16), v3.astype(BF16), ki3.astype(BF16)
    else:
        ck, cv, cki = past
        past_len = ck.shape[1]
        k_all = jnp.concatenate([ck.astype(BF16).reshape(b, past_len, KV_DIM), k3.astype(BF16)], axis=1)
        v_all = jnp.concatenate([cv.astype(BF16).reshape(b, past_len, KV_DIM), v3.astype(BF16)], axis=1)
        ki_all = jnp.concatenate([cki.astype(BF16), ki3.astype(BF16)], axis=1)
    n_keys = past_len + t
    k_sel = min(TOPK_MAX, n_keys // 4)
    pad = (-n_keys) % LANES
    if pad:
        k_all, v_all, ki_all = [jnp.pad(a, ((0, 0), (0, pad), (0, 0))) for a in (k_all, v_all, ki_all)]
    q3, qi3, wi3 = q.reshape(b, t, ATTN_DIM), qi.reshape(b, t, -1), wi.reshape(b, t, -1)
    if past is None:
        wit3 = jnp.swapaxes(wi3, 1, 2)
        vt4 = jnp.swapaxes(v_all, 1, 2).reshape(b, N_KV_HEADS, HEAD_DIM, n_keys)
        vt_all = jnp.concatenate([vt4, jnp.ones((b, N_KV_HEADS, 1, n_keys), BF16),
                                  jnp.zeros((b, N_KV_HEADS, VT_ROWS - HEAD_DIM - 1, n_keys), BF16)], axis=2)
        pieces = [_attn_t_call(q3, qi3, wit3, k_all, vt_all, ki_all, tq, i, 1, (i + 1) * tq, t, 0, k_sel,
                               "attn_%s%d" % (tag, i)) for i in range(t // tq)]
        attn_out = jnp.concatenate(pieces, axis=2)
    else:
        kit_all = jnp.swapaxes(ki_all, 1, 2)
        attn_out = _attn_call(q3, qi3, wi3, k_all, v_all, kit_all, nb_attn, tq, 0, t // tq, n_keys + pad, t,
                              past_len, k_sel, "attn_" + tag).reshape(r, ATTN_DIM)

    y, new_ffn = _post_call(x2d, conv_out.reshape(r, CONV_DIM), attn_out,
                            gt1, sc2, sh2, gt2, g2, gf, w_out, w_up, w_down, fdw_w, fdw_b, ffn_hist,
                            tm_post, t, "post_" + tag)
    return (y.reshape(b, t, d), k3.reshape(b, t, N_KV_HEADS, HEAD_DIM), v3.reshape(b, t, N_KV_HEADS, HEAD_DIM),
            ki3, new_conv, new_ffn)


def kernel(x_prompt, x_sample, cache_k, cache_v, cache_kidx, state_conv, state_ffn_conv, c_prompt, c_sample,
           w_ada, b_ada, norm1_g, w_in, conv_dw_w, conv_dw_b, conv_ln_g, conv_ln_b, w_out, norm2_g,
           w_up, ffn_dw_w, ffn_dw_b, w_down, final_norm_g):
    depth = w_ada.shape[0]
    assert depth == 1, "the final norm is fused into the single layer's last kernel"
    bp, sp, d = x_prompt.shape
    bs, ts, _ = x_sample.shape
    past_len = cache_k.shape[2]
    l = 0
    mod = _mod_call(jnp.concatenate([c_prompt, c_sample], axis=0), w_ada[l], b_ada[l])
    mods_p = jnp.split(mod[:bp], 6, axis=-1)
    mods_s = jnp.split(mod[bp:], 6, axis=-1)
    w = (norm1_g[l].reshape(1, d), _pad_w_in(w_in[l]), conv_dw_w[l], conv_dw_b[l], conv_ln_g[l], conv_ln_b[l],
         w_out[l].astype(BF16), norm2_g[l].reshape(1, d), w_up[l].astype(BF16), ffn_dw_w[l],
         ffn_dw_b[l].reshape(1, -1), w_down[l].astype(BF16), final_norm_g.reshape(1, d))

    conv0 = jnp.zeros((bp, CONV_HIST, CONV_DIM), F32)
    ffn0 = jnp.zeros((bp, FFN_CONV_WIDTH - 1, 2 * D_FF), F32)
    out_p = _layer_group(x_prompt, mods_p, jnp.arange(sp, dtype=I32), conv0, ffn0, None, w,
                         per_row_mod=False, tm_in=512, tt_conv=128, tq=256, tm_post=512, tag="p")
    out_s = _layer_group(x_sample, mods_s, past_len + jnp.arange(ts, dtype=I32), state_conv[l], state_ffn_conv[l],
                         (cache_k[l], cache_v[l], cache_kidx[l]), w,
                         per_row_mod=True, tm_in=bs * ts, tt_conv=ts, tq=ts, tm_post=bs * ts, tag="s", nb_attn=4)
    y_p, k_p, v_p, ki_p, conv_p, ffn_p = out_p
    y_s, k_s, v_s, ki_s, conv_s, ffn_s = out_s
    st = lambda a: a[None]
    return (y_p, y_s, st(k_p), st(v_p), st(ki_p), st(conv_p), st(ffn_p),
            st(k_s), st(v_s), st(ki_s), st(conv_s), st(ffn_s))
```

```python
import functools

import jax
import jax.numpy as jnp
from jax import lax
from jax.experimental import pallas as pl
from jax.experimental.pallas import tpu as pltpu

F32 = jnp.float32
BF16 = jnp.bfloat16
I32 = jnp.int32
I16 = jnp.int16

D_MODEL = 1024
CHUNK = 64
CONV_DIM = 512
CONV_WIDTH = 31
N_HEADS = 8
HEAD_DIM = 64
N_KV_HEADS = 2
ATTN_DIM = N_HEADS * HEAD_DIM
KV_DIM = N_KV_HEADS * HEAD_DIM
ROT_DIM = HEAD_DIM // 4
ROPE_THETA = 500000.0
N_IDX_HEADS = 8
IDX_DIM = 32
TOPK_MAX = 256
D_FF = 2816
FFN_CONV_WIDTH = 3
EPS = 1e-6

LANES = 128
SUBLANES = 8
MXU_N = 256
VMEM_LIMIT = 52 * 1024 * 1024

COL_U, COL_UG, COL_Q, COL_K, COL_V, COL_QI = 0, 512, 1024, 1536, 1664, 1792
COL_KI = 2048
COL_WI = COL_KI + LANES
IN_PAD = COL_WI + LANES

Q_SCALE = HEAD_DIM ** -0.5 * 1.4426950408889634

INT_MIN = -2147483648
KEY_NEG_INF = -2139095040
IDX_BITS = 13


def _order_key(x):
    bits = pltpu.bitcast(x, I32)
    return jnp.where(bits < 0, INT_MIN - bits, bits)


def _cparams(sem):
    return pltpu.CompilerParams(dimension_semantics=sem, vmem_limit_bytes=VMEM_LIMIT)


def _silu(x):
    return x * jax.nn.sigmoid(x)


def _mod_kernel(c_ref, w_ref, b_ref, o_ref):
    s = _silu(c_ref[...]).astype(BF16)
    o_ref[...] = jnp.dot(s, w_ref[...].astype(BF16), preferred_element_type=F32) + b_ref[...]


def _mod_call(c_all, w_ada, b_ada):
    nb, d = c_all.shape
    n = w_ada.shape[1]
    tn = 512
    return pl.pallas_call(
        _mod_kernel,
        grid=(n // tn,),
        in_specs=[pl.BlockSpec((nb, d), lambda j: (0, 0)),
                  pl.BlockSpec((d, tn), lambda j: (0, j)),
                  pl.BlockSpec((1, tn), lambda j: (0, j))],
        out_specs=pl.BlockSpec((nb, tn), lambda j: (0, j)),
        out_shape=jax.ShapeDtypeStruct((nb, n), F32),
        compiler_params=_cparams(("arbitrary",)),
        name="mod",
    )(c_all, w_ada, b_ada.reshape(1, n))


def _rmsnorm_mod(x, g, sc, sh):
    ms = jnp.mean(x * x, axis=-1, keepdims=True)
    return (x * lax.rsqrt(ms + EPS) * g) * (1.0 + sc) + sh


def _inproj_kernel(x_ref, sc_ref, sh_ref, g_ref, w_ref, cos_ref, sa_ref, sb_ref, glu_ref, q_ref, qi_ref, *kv_refs,
                   kv_transposed):
    h = _rmsnorm_mod(x_ref[...], g_ref[...], sc_ref[...], sh_ref[...])
    z = jnp.dot(h.astype(BF16), w_ref[...], preferred_element_type=F32)
    glu_ref[...] = z[:, COL_U:COL_U + CONV_DIM] * jax.nn.sigmoid(z[:, COL_UG:COL_UG + CONV_DIM])
    cos, sa, sb = cos_ref[...], sa_ref[...], sb_ref[...]

    def rope(xs):
        return (xs * cos + pltpu.roll(xs, ROT_DIM // 2, axis=1) * sa
                + pltpu.roll(xs, LANES - ROT_DIM // 2, axis=1) * sb)

    for j in range(ATTN_DIM // LANES):
        c0 = COL_Q + LANES * j
        q_ref[:, LANES * j:LANES * (j + 1)] = (rope(z[:, c0:c0 + LANES]) * Q_SCALE).astype(BF16)
    qi_ref[...] = z[:, COL_QI:COL_QI + N_IDX_HEADS * IDX_DIM].astype(BF16)
    k = rope(z[:, COL_K:COL_K + KV_DIM])
    v = z[:, COL_V:COL_V + KV_DIM]
    ki_slab = z[:, COL_KI:COL_KI + LANES]
    wi_slab = z[:, COL_WI:COL_WI + LANES]
    if kv_transposed:
        kbf_ref, kibf_ref, kt_ref, vt_ref, vtbf_ref, kit_ref, wit_ref = kv_refs
        kbf_ref[...] = k.astype(BF16)
        kibf_ref[...] = ki_slab[:, 0:IDX_DIM].astype(BF16)
        kt_ref[...] = k.T
        vt = v.T
        vt_ref[...] = vt
        vtbf_ref[...] = vt.astype(BF16)
        kit_ref[...] = ki_slab.T[0:IDX_DIM, :]
        wit_ref[...] = wi_slab.T[0:N_IDX_HEADS, :]
    else:
        k_ref, v_ref, ki_ref, wi_ref = kv_refs
        k_ref[...] = k
        v_ref[...] = v
        ki_ref[...] = ki_slab[:, 0:IDX_DIM]
        wi_ref[...] = wi_slab[:, 0:N_IDX_HEADS]


def _inproj_call(x2, sc, sh, g1, w_in_p, tabs, tm, name):
    r, d = x2.shape
    nt = r // tm
    kv_t = sc.ndim == 3
    if kv_t:
        nb = sc.shape[0]
        tpb = nt // nb
        t_len = r // nb
        mod_spec = pl.BlockSpec((None, 1, d), lambda i: (i // tpb, 0, 0))
    else:
        mod_spec = pl.BlockSpec((tm, d), lambda i: (i, 0))
    ntab = tabs[0].shape[0] // tm
    tab_spec = pl.BlockSpec((tm, LANES), lambda i: (i % ntab, 0))

    def rows(width):
        return pl.BlockSpec((tm, width), lambda i: (i, 0))

    def rows_shape(width, dtype):
        return jax.ShapeDtypeStruct((r, width), dtype)

    def cols(width):
        return pl.BlockSpec((None, width, tm), lambda i: (i // tpb, 0, i % tpb))

    def cols_shape(width, dtype):
        return jax.ShapeDtypeStruct((nb, width, t_len), dtype)

    out_specs = [rows(CONV_DIM), rows(ATTN_DIM), rows(N_IDX_HEADS * IDX_DIM)]
    out_shapes = [rows_shape(CONV_DIM, F32), rows_shape(ATTN_DIM, BF16), rows_shape(N_IDX_HEADS * IDX_DIM, BF16)]
    if kv_t:
        out_specs += [rows(KV_DIM), rows(IDX_DIM), cols(KV_DIM), cols(KV_DIM), cols(KV_DIM), cols(IDX_DIM),
                      cols(N_IDX_HEADS)]
        out_shapes += [rows_shape(KV_DIM, BF16), rows_shape(IDX_DIM, BF16), cols_shape(KV_DIM, F32),
                       cols_shape(KV_DIM, F32), cols_shape(KV_DIM, BF16), cols_shape(IDX_DIM, F32),
                       cols_shape(N_IDX_HEADS, F32)]
    else:
        out_specs += [rows(KV_DIM), rows(KV_DIM), rows(IDX_DIM), rows(N_IDX_HEADS)]
        out_shapes += [rows_shape(KV_DIM, F32), rows_shape(KV_DIM, F32), rows_shape(IDX_DIM, F32),
                       rows_shape(N_IDX_HEADS, F32)]
    return pl.pallas_call(
        functools.partial(_inproj_kernel, kv_transposed=kv_t),
        grid=(nt,),
        in_specs=[rows(d), mod_spec, mod_spec,
                  pl.BlockSpec((1, d), lambda i: (0, 0)),
                  pl.BlockSpec((d, IN_PAD), lambda i: (0, 0)),
                  tab_spec, tab_spec, tab_spec],
        out_specs=tuple(out_specs),
        out_shape=tuple(out_shapes),
        compiler_params=_cparams(("arbitrary",)),
        name=name,
    )(x2, sc, sh, g1, w_in_p, *tabs)


CONV_HIST = CONV_WIDTH - 1
CONV_PAD = 32
CONV_RB = 64


def _conv_kernel(glu_ref, hist_ref, w_ref, b_ref, lg_ref, lb_ref, o_ref, ext_ref, y_ref, *, tt):
    t = pl.program_id(1)

    @pl.when(t == 0)
    def _():
        ext_ref[0:CONV_PAD - CONV_HIST, :] = jnp.zeros((CONV_PAD - CONV_HIST, CONV_DIM), F32)
        ext_ref[CONV_PAD - CONV_HIST:CONV_PAD, :] = hist_ref[...]

    @pl.when(t > 0)
    def _():
        ext_ref[0:CONV_PAD, :] = ext_ref[tt:tt + CONV_PAD, :]

    ext_ref[CONV_PAD:CONV_PAD + tt, :] = glu_ref[...]

    off = CONV_PAD - CONV_HIST
    rb = min(CONV_RB, tt)
    for r0 in range(0, tt, rb):
        for c0 in range(0, CONV_DIM, LANES):
            acc = None
            for b in range(SUBLANES):
                span = (CONV_WIDTH - 1 - b) // SUBLANES * SUBLANES
                mis = (off + b) % SUBLANES
                base = off + b - mis + r0
                cover = -(-(mis + rb + span) // SUBLANES) * SUBLANES
                win = ext_ref[base:base + cover, c0:c0 + LANES]
                if mis:
                    win = pltpu.roll(win, cover - mis, axis=0)
                for j in range(b, CONV_WIDTH, SUBLANES):
                    term = win[j - b:j - b + rb, :] * w_ref[j:j + 1, c0:c0 + LANES]
                    acc = term if acc is None else acc + term
            y_ref[r0:r0 + rb, c0:c0 + LANES] = acc + b_ref[:, c0:c0 + LANES]

    y = y_ref[...]
    mu = jnp.mean(y, axis=-1, keepdims=True)
    yc = y - mu
    var = jnp.mean(yc * yc, axis=-1, keepdims=True)
    o_ref[...] = _silu(yc * lax.rsqrt(var + EPS) * lg_ref[...] + lb_ref[...]).astype(BF16)


def _conv_call(glu3, hist, dw_w, dw_b, ln_g, ln_b, tt, name):
    b, t, c = glu3.shape
    vec = pl.BlockSpec((1, c), lambda i, j: (0, 0))
    return pl.pallas_call(
        functools.partial(_conv_kernel, tt=tt),
        grid=(b, t // tt),
        in_specs=[pl.BlockSpec((None, tt, c), lambda i, j: (i, j, 0)),
                  pl.BlockSpec((None, CONV_HIST, c), lambda i, j: (i, 0, 0)),
                  pl.BlockSpec((CONV_WIDTH, c), lambda i, j: (0, 0)),
                  vec, vec, vec],
        out_specs=pl.BlockSpec((None, tt, c), lambda i, j: (i, j, 0)),
        out_shape=jax.ShapeDtypeStruct((b, t, c), BF16),
        scratch_shapes=[pltpu.VMEM((tt + CONV_PAD, c), F32), pltpu.VMEM((tt, c), F32)],
        compiler_params=_cparams(("arbitrary", "arbitrary")),
        name=name,
    )(glu3, hist, dw_w, dw_b.reshape(1, c), ln_g.reshape(1, c), ln_b.reshape(1, c))


CNT_ROWS = 64


def _pad_rows(x, n):
    return jnp.concatenate([x, jnp.zeros((n - x.shape[0], x.shape[1]), x.dtype)], axis=0)


def _attn_kernel(q_ref, qi_ref, wi_ref, kn_ref, vn_ref, kin_ref, kct_ref, vct_ref, kict_ref, o_ref, key_ref, bias_ref,
                 *, nb, tq, past, k_sel):
    rows = nb * tq
    n_keys = past + LANES
    nt = (((1,), (1,)), ((), ()))
    col = lax.broadcasted_iota(I32, (rows, n_keys), 1)
    adm = col < past + tq
    new_ok = lax.broadcasted_iota(I32, (tq, LANES), 1) < tq

    for bb in range(nb):
        kic = kict_ref[bb].astype(BF16)
        kin = _pad_rows(kin_ref[bb].astype(BF16), LANES)
        wi = wi_ref[bb]
        sc_c = jnp.zeros((tq, past), F32)
        sc_n = jnp.zeros((tq, LANES), F32)
        for h in range(N_IDX_HEADS):
            qih = qi_ref[bb, :, IDX_DIM * h:IDX_DIM * (h + 1)]
            sc_c = sc_c + wi[:, h:h + 1] * jnp.maximum(jnp.dot(qih, kic, preferred_element_type=F32), 0.0)
            sc_n = sc_n + wi[:, h:h + 1] * jnp.maximum(
                lax.dot_general(qih, kin, nt, preferred_element_type=F32), 0.0)
        key_ref[bb * tq:(bb + 1) * tq, 0:past] = _order_key(sc_c)
        key_ref[bb * tq:(bb + 1) * tq, past:n_keys] = _order_key(jnp.where(new_ok, sc_n, -jnp.inf))

    rg = min(CNT_ROWS, rows)

    def count_ge(cands):
        outs = [[] for _ in cands]
        for r0 in range(0, rows, rg):
            cs = [jnp.broadcast_to(c[r0:r0 + rg, :], (rg, LANES)) for c in cands]
            accs = [jnp.zeros((rg, LANES), F32) for _ in cands]
            for c0 in range(0, n_keys, LANES):
                kv = key_ref[r0:r0 + rg, c0:c0 + LANES]
                accs = [a + jnp.where(kv >= c, 1.0, 0.0) for a, c in zip(accs, cs)]
            for o, a in zip(outs, accs):
                o.append(jnp.sum(a, axis=1, keepdims=True))
        return [o[0] if len(o) == 1 else jnp.concatenate(o, axis=0) for o in outs]

    bpp = 2 if rows <= 64 else 1

    def bits_step(it, r):
        sh = 32 - bpp * (it + 1)
        step = jnp.left_shift(jnp.int32(1), sh)
        counts = count_ge([r + m * step for m in range(1, 1 << bpp)])
        inc = sum(jnp.where(n >= k_sel, 1, 0) for n in counts)
        return r + jnp.left_shift(inc, sh)

    thr = lax.fori_loop(0, 32 // bpp, bits_step, jnp.full((rows, 1), INT_MIN, I32))

    keys = key_ref[...]
    ge = keys >= thr
    cnt_ge = jnp.sum(jnp.where(ge, 1.0, 0.0), axis=1, keepdims=True)
    bias_ref[...] = jnp.where(ge & adm, 0.0, -jnp.inf)
    split_tie = (cnt_ge > k_sel) & (thr != KEY_NEG_INF)
    any_split = jnp.max(jnp.where(split_tie, 1.0, 0.0)) > 0.0

    @pl.when(any_split)
    def _():
        gt = keys > thr
        eq = keys == thr
        need = k_sel - jnp.sum(jnp.where(gt, 1.0, 0.0), axis=1, keepdims=True)

        def idx_step(it, last):
            cand = last + jnp.left_shift(jnp.int32(1), IDX_BITS - 1 - it)
            below = jnp.sum(jnp.where(eq & (col < cand), 1.0, 0.0), axis=1, keepdims=True)
            return jnp.where(below < need, cand, last)

        last = lax.fori_loop(0, IDX_BITS, idx_step, jnp.zeros((rows, 1), I32))
        sel = (gt | (eq & (col <= last))) & adm
        bias_ref[...] = jnp.where(sel, 0.0, -jnp.inf)

    rep = N_HEADS // N_KV_HEADS
    for bb in range(nb):
        bias = bias_ref[bb * tq:(bb + 1) * tq, :]
        for g in range(N_KV_HEADS):
            qg = jnp.concatenate([q_ref[bb, :, HEAD_DIM * (rep * g + r):HEAD_DIM * (rep * g + r + 1)]
                                  for r in range(rep)], axis=0)
            kn = _pad_rows(kn_ref[bb, :, HEAD_DIM * g:HEAD_DIM * (g + 1)].astype(BF16), LANES)
            vn = _pad_rows(vn_ref[bb, :, HEAD_DIM * g:HEAD_DIM * (g + 1)].astype(BF16), LANES)
            logits = jnp.concatenate(
                [jnp.dot(qg, kct_ref[bb, g].astype(BF16), preferred_element_type=F32),
                 lax.dot_general(qg, kn, nt, preferred_element_type=F32)], axis=1)
            ps, ss = [], []
            for r in range(rep):
                lg = logits[r * tq:(r + 1) * tq, :] + bias
                p = jnp.exp2(lg - jnp.max(lg, axis=1, keepdims=True))
                ss.append(jnp.sum(p, axis=1, keepdims=True))
                ps.append(p.astype(BF16))
            p_all = jnp.concatenate(ps, axis=0)
            o = (lax.dot_general(p_all[:, 0:past], vct_ref[bb, g].astype(BF16), nt, preferred_element_type=F32)
                 + jnp.dot(p_all[:, past:n_keys], vn, preferred_element_type=F32))
            for r in range(rep):
                h = rep * g + r
                o_ref[bb, :, HEAD_DIM * h:HEAD_DIM * (h + 1)] = (o[r * tq:(r + 1) * tq, :] / ss[r]).astype(BF16)


def _attn_call(q3, qi3, wi3, k_new, v_new, ki_new, kct, vct, kict, nb, k_sel, name):
    b, tq, _ = q3.shape
    past = kct.shape[-1]
    n_keys = past + LANES
    assert past % LANES == 0 and tq <= LANES and n_keys <= (1 << IDX_BITS) and b % nb == 0

    def rows(width):
        return pl.BlockSpec((nb, tq, width), lambda i: (i, 0, 0))

    kern = functools.partial(_attn_kernel, nb=nb, tq=tq, past=past, k_sel=float(k_sel))
    return pl.pallas_call(
        kern,
        grid=(b // nb,),
        in_specs=[rows(ATTN_DIM), rows(N_IDX_HEADS * IDX_DIM), rows(N_IDX_HEADS),
                  rows(KV_DIM), rows(KV_DIM), rows(IDX_DIM),
                  pl.BlockSpec((nb, N_KV_HEADS, HEAD_DIM, past), lambda i: (i, 0, 0, 0)),
                  pl.BlockSpec((nb, N_KV_HEADS, HEAD_DIM, past), lambda i: (i, 0, 0, 0)),
                  pl.BlockSpec((nb, IDX_DIM, past), lambda i: (i, 0, 0))],
        out_specs=rows(ATTN_DIM),
        out_shape=jax.ShapeDtypeStruct((b, tq, ATTN_DIM), BF16),
        scratch_shapes=[pltpu.VMEM((nb * tq, n_keys), I32), pltpu.VMEM((nb * tq, n_keys), F32)],
        compiler_params=_cparams(("arbitrary",)),
        name=name,
    )(q3, qi3, wi3, k_new, v_new, ki_new, kct, vct, kict)


VT_ROWS = HEAD_DIM + 16
SCORE_KEYS = 256
CNT_KEYS = 64


def _attn_t_kernel(q_ref, qi_ref, wit_ref, k_ref, vt_ref, ki_ref, o_ref, key_ref, bias_ref, hi_ref, lo_ref,
                   *, tq, n_keys, q0, t_new, past, k_sel):
    j = pl.program_id(1)
    qpos = lax.broadcasted_iota(I32, (1, tq), 1) + (q0 + j * tq)
    limit = past + jnp.minimum((qpos // CHUNK + 1) * CHUNK, t_new)
    nt = (((1,), (1,)), ((), ()))

    qi_stack = jnp.concatenate([qi_ref[:, IDX_DIM * h:IDX_DIM * (h + 1)] for h in range(N_IDX_HEADS)], axis=0)
    wit = wit_ref[...]
    kc = min(SCORE_KEYS, n_keys)
    for c0 in range(0, n_keys, kc):
        d = lax.dot_general(ki_ref[c0:c0 + kc, :], qi_stack, nt, preferred_element_type=F32)
        sc = wit[0:1, :] * jnp.maximum(d[:, 0:tq], 0.0)
        for h in range(1, N_IDX_HEADS):
            sc = sc + wit[h:h + 1, :] * jnp.maximum(d[:, h * tq:(h + 1) * tq], 0.0)
        kpos = lax.broadcasted_iota(I32, (kc, tq), 0) + c0
        kc_keys = _order_key(jnp.where(kpos < limit, sc, -jnp.inf))
        key_ref[c0:c0 + kc, :] = kc_keys
        hi_ref[c0:c0 + kc, :] = (kc_keys >> 16).astype(I16)
        lo_ref[c0:c0 + kc, :] = (kc_keys ^ 0x8000).astype(I16)

    sl = min(CNT_KEYS, n_keys)

    def search16(ref):
        def bit_step(it, r):
            cand = r + jnp.left_shift(jnp.int32(1), 15 - it)
            cb = jnp.broadcast_to((cand - 32768).astype(I16), (sl, tq))
            acc = jnp.zeros((sl, tq), I16)
            for c0 in range(0, n_keys, sl):
                acc = acc + jnp.where(ref[c0:c0 + sl, :] >= cb, I16(1), I16(0))
            cnt = jnp.sum(acc.astype(I32), axis=0, keepdims=True)
            return jnp.where(cnt >= k_sel, cand, r)
        return lax.fori_loop(0, 16, bit_step, jnp.zeros((1, tq), I32))

    r_hi = search16(hi_ref)
    h16 = (r_hi - 32768).astype(I16)
    hi = hi_ref[...]
    lo_ref[...] = jnp.where(hi == h16, lo_ref[...], jnp.where(hi > h16, I16(32767), I16(-32768)))
    thr = ((r_hi - 32768) << 16) | search16(lo_ref)

    keys = key_ref[...]
    kpos = lax.broadcasted_iota(I32, (n_keys, tq), 0)
    adm = kpos < limit
    ge = keys >= thr
    cnt_ge = jnp.sum(jnp.where(ge, 1.0, 0.0), axis=0, keepdims=True)
    bias_ref[...] = jnp.where(ge & adm, 0.0, -jnp.inf)
    split_tie = (cnt_ge > k_sel) & (thr != KEY_NEG_INF)
    any_split = jnp.max(jnp.where(split_tie, 1.0, 0.0)) > 0.0

    @pl.when(any_split)
    def _():
        gt = keys > thr
        eq = keys == thr
        need = k_sel - jnp.sum(jnp.where(gt, 1.0, 0.0), axis=0, keepdims=True)

        def idx_step(it, last):
            cand = last + jnp.left_shift(jnp.int32(1), IDX_BITS - 1 - it)
            below = jnp.sum(jnp.where(eq & (kpos < cand), 1.0, 0.0), axis=0, keepdims=True)
            return jnp.where(below < need, cand, last)

        last = lax.fori_loop(0, IDX_BITS, idx_step, jnp.zeros((1, tq), I32))
        sel = (gt | (eq & (kpos <= last))) & adm
        bias_ref[...] = jnp.where(sel, 0.0, -jnp.inf)

    bias = bias_ref[...]
    rep = N_HEADS // N_KV_HEADS
    for g in range(N_KV_HEADS):
        qg = jnp.concatenate([q_ref[:, HEAD_DIM * (rep * g + r):HEAD_DIM * (rep * g + r + 1)] for r in range(rep)],
                             axis=0)
        logits = lax.dot_general(k_ref[:, HEAD_DIM * g:HEAD_DIM * (g + 1)], qg, nt, preferred_element_type=F32)
        ps = []
        for r in range(rep):
            lg = logits[:, r * tq:(r + 1) * tq] + bias
            ps.append(jnp.exp2(lg - jnp.max(lg, axis=0, keepdims=True)).astype(BF16))
        vt_ext = jnp.concatenate([vt_ref[HEAD_DIM * g:HEAD_DIM * (g + 1), :],
                                  jnp.ones((VT_ROWS - HEAD_DIM, n_keys), BF16)], axis=0)
        o = jnp.dot(vt_ext, jnp.concatenate(ps, axis=1), preferred_element_type=F32)
        o = o[0:HEAD_DIM, :] / o[HEAD_DIM:HEAD_DIM + 1, :]
        for r in range(rep):
            h = rep * g + r
            o_ref[HEAD_DIM * h:HEAD_DIM * (h + 1), :] = o[:, r * tq:(r + 1) * tq].astype(BF16)


def _attn_t_call(q3, qi3, wit3, k_all, vt_all, ki_all, tq, q_blk0, n_q_blk, n_keys, t_new, past, k_sel, name):
    b = q3.shape[0]
    assert n_keys % LANES == 0 and n_keys <= (1 << IDX_BITS) and n_keys <= k_all.shape[1] and tq % LANES == 0
    kern = functools.partial(_attn_t_kernel, tq=tq, n_keys=n_keys, q0=q_blk0 * tq, t_new=t_new, past=past,
                             k_sel=float(k_sel))
    return pl.pallas_call(
        kern,
        grid=(b, n_q_blk),
        in_specs=[pl.BlockSpec((None, tq, ATTN_DIM), lambda i, j: (i, q_blk0 + j, 0)),
                  pl.BlockSpec((None, tq, N_IDX_HEADS * IDX_DIM), lambda i, j: (i, q_blk0 + j, 0)),
                  pl.BlockSpec((None, N_IDX_HEADS, tq), lambda i, j: (i, 0, q_blk0 + j)),
                  pl.BlockSpec((None, n_keys, KV_DIM), lambda i, j: (i, 0, 0)),
                  pl.BlockSpec((None, KV_DIM, n_keys), lambda i, j: (i, 0, 0)),
                  pl.BlockSpec((None, n_keys, IDX_DIM), lambda i, j: (i, 0, 0))],
        out_specs=pl.BlockSpec((None, ATTN_DIM, tq), lambda i, j: (i, 0, j)),
        out_shape=jax.ShapeDtypeStruct((b, ATTN_DIM, n_q_blk * tq), BF16),
        scratch_shapes=[pltpu.VMEM((n_keys, tq), I32), pltpu.VMEM((n_keys, tq), F32),
                        pltpu.VMEM((n_keys, tq), I16), pltpu.VMEM((n_keys, tq), I16)],
        compiler_params=_cparams(("arbitrary", "arbitrary")),
        name=name,
    )(q3, qi3, wit3, k_all, vt_all, ki_all)


FF_CHUNKS = ((0, 1024), (1024, 1024), (2048, 768))


def _post_kernel(x_ref, conv_ref, attn_ref, gt1_ref, sc2_ref, sh2_ref, gt2_ref, g2_ref, gf_ref,
                 wout_ref, wup_ref, wdn_ref, fw_ref, fb_ref, hist_ref,
                 y_ref, newffn_ref, *scratch, tm, t_len, tiles_per_batch, attn_transposed):
    nseg = max(1, tm // t_len)
    seg = tm // nseg
    i = pl.program_id(0)
    if nseg == 1:
        carry_ref = scratch[0]

        @pl.when(i % tiles_per_batch == 0)
        def _():
            carry_ref[...] = hist_ref[0]

    w_attn = wout_ref[CONV_DIM:CONV_DIM + ATTN_DIM, :]
    if attn_transposed:
        mix_attn = lax.dot_general(attn_ref[...], w_attn, (((0,), (0,)), ((), ())), preferred_element_type=F32)
    else:
        mix_attn = jnp.dot(attn_ref[...], w_attn, preferred_element_type=F32)
    mix = jnp.dot(conv_ref[...], wout_ref[0:CONV_DIM, :], preferred_element_type=F32) + mix_attn
    x1 = x_ref[...] + gt1_ref[...] * mix
    h2 = _rmsnorm_mod(x1, g2_ref[...], sc2_ref[...], sh2_ref[...]).astype(BF16)

    row = lax.broadcasted_iota(I32, (SUBLANES, 1), 0)

    def causal3(u, col0, width):
        w0 = fw_ref[0:1, col0:col0 + width]
        w1 = fw_ref[1:2, col0:col0 + width]
        w2 = fw_ref[2:3, col0:col0 + width]
        outs = []
        for s in range(nseg):
            us = u[s * seg:(s + 1) * seg, :]
            if nseg == 1:
                h0 = carry_ref[0:1, col0:col0 + width]
                h1 = carry_ref[1:2, col0:col0 + width]
            else:
                h0 = hist_ref[s, 0:1, col0:col0 + width]
                h1 = hist_ref[s, 1:2, col0:col0 + width]
            p1 = pltpu.roll(us, 1, axis=0)
            p2 = pltpu.roll(us, 2, axis=0)
            p1 = jnp.concatenate([jnp.where(row == 0, h1, p1[0:SUBLANES, :]), p1[SUBLANES:, :]], axis=0)
            p2 = jnp.concatenate([jnp.where(row == 0, h0, jnp.where(row == 1, h1, p2[0:SUBLANES, :])),
                                  p2[SUBLANES:, :]], axis=0)
            outs.append(us * w2 + p1 * w1 + p2 * w0 + fb_ref[:, col0:col0 + width])
            if nseg == 1:
                carry_ref[:, col0:col0 + width] = us[seg - 2:seg, :]
            else:
                newffn_ref[s, :, col0:col0 + width] = us[seg - 2:seg, :]
        return outs[0] if nseg == 1 else jnp.concatenate(outs, axis=0)

    acc = jnp.zeros((tm, D_MODEL), F32)
    for c0, cw in FF_CHUNKS:
        ua = jnp.dot(h2, wup_ref[:, c0:c0 + cw], preferred_element_type=F32)
        ug = jnp.dot(h2, wup_ref[:, D_FF + c0:D_FF + c0 + cw], preferred_element_type=F32)
        a = causal3(ua, c0, cw)
        g = causal3(ug, D_FF + c0, cw)
        acc = acc + jnp.dot((a * _silu(g)).astype(BF16), wdn_ref[c0:c0 + cw, :], preferred_element_type=F32)

    if nseg == 1:
        @pl.when(i % tiles_per_batch == tiles_per_batch - 1)
        def _():
            newffn_ref[0] = carry_ref[...]

    x2 = x1 + gt2_ref[...] * acc
    ms = jnp.mean(x2 * x2, axis=-1, keepdims=True)
    y_ref[...] = x2 * lax.rsqrt(ms + EPS) * gf_ref[...]


def _post_call(x2d, conv2d, attn, gt1, sc2, sh2, gt2, g2, gf, wout, wup, wdn, fw, fb, hist, tm, t_len, name):
    r, d = x2d.shape
    nt = r // tm
    nb = hist.shape[0]
    nseg = max(1, tm // t_len)
    tpb = max(1, t_len // tm)
    assert nt * nseg == nb * tpb and t_len >= FFN_CONV_WIDTH - 1
    if gt1.ndim == 3:
        mod_spec = pl.BlockSpec((None, 1, d), lambda i: (i // tpb, 0, 0))
    else:
        mod_spec = pl.BlockSpec((tm, d), lambda i: (i, 0))

    def rows(width):
        return pl.BlockSpec((tm, width), lambda i: (i, 0))

    def const(shape):
        return pl.BlockSpec(shape, lambda i: (0,) * len(shape), pipeline_mode=pl.Buffered(1))

    state_spec = pl.BlockSpec((nseg, FFN_CONV_WIDTH - 1, 2 * D_FF), lambda i: (i * nseg // tpb, 0, 0))
    attn_t = attn.ndim == 3
    if attn_t:
        assert nseg == 1
        attn_spec = pl.BlockSpec((None, ATTN_DIM, tm), lambda i: (i // tpb, 0, i % tpb))
    else:
        attn_spec = rows(ATTN_DIM)
    kern = functools.partial(_post_kernel, tm=tm, t_len=t_len, tiles_per_batch=tpb, attn_transposed=attn_t)
    scratch = [pltpu.VMEM((FFN_CONV_WIDTH - 1, 2 * D_FF), F32)] if nseg == 1 else []
    return pl.pallas_call(
        kern,
        grid=(nt,),
        in_specs=[rows(d), rows(CONV_DIM), attn_spec, mod_spec, mod_spec, mod_spec, mod_spec,
                  const((1, d)), const((1, d)),
                  const((CONV_DIM + ATTN_DIM, d)), const((d, 2 * D_FF)), const((D_FF, d)),
                  const((FFN_CONV_WIDTH, 2 * D_FF)), const((1, 2 * D_FF)),
                  state_spec],
        out_specs=(rows(d), state_spec),
        out_shape=(jax.ShapeDtypeStruct((r, d), F32),
                   jax.ShapeDtypeStruct((nb, FFN_CONV_WIDTH - 1, 2 * D_FF), F32)),
        scratch_shapes=scratch,
        compiler_params=_cparams(("arbitrary",)),
        name=name,
    )(x2d, conv2d, attn, gt1, sc2, sh2, gt2, g2, gf, wout, wup, wdn, fw, fb, hist)


def _rope_tables(pos):
    half = ROT_DIM // 2
    inv = 1.0 / (ROPE_THETA ** (jnp.arange(0, ROT_DIM, 2, dtype=F32) / ROT_DIM))
    ang = pos.astype(F32)[:, None] * inv[None, :]
    cos, sin = jnp.cos(ang), jnp.sin(ang)
    t = pos.shape[0]
    rest1 = jnp.ones((t, HEAD_DIM - ROT_DIM), F32)
    rest0 = jnp.zeros((t, HEAD_DIM - ROT_DIM), F32)
    z = jnp.zeros((t, half), F32)
    c64 = jnp.concatenate([cos, cos, rest1], axis=1)
    a64 = jnp.concatenate([z, sin, rest0], axis=1)
    b64 = jnp.concatenate([-sin, z, rest0], axis=1)
    return tuple(jnp.concatenate([m, m], axis=1) for m in (c64, a64, b64))


def _pad_w_in(w_in):
    d = w_in.shape[0]
    return jnp.concatenate(
        [w_in[:, :COL_KI + IDX_DIM], jnp.zeros((d, LANES - IDX_DIM), F32),
         w_in[:, COL_KI + IDX_DIM:], jnp.zeros((d, LANES - N_IDX_HEADS), F32)], axis=1).astype(BF16)


def _layer_group(x, mods, pos, conv_hist, ffn_hist, past, w, *, per_row_mod, tm_in, tt_conv, tq, tm_post, tag, nb_attn=1):
    (g1, w_in_p, dw_w, dw_b, ln_g, ln_b, w_out, g2, w_up, fdw_w, fdw_b, w_down, gf) = w
    b, t, d = x.shape
    r = b * t
    x2d = x.reshape(r, d)
    if per_row_mod:
        sh1, sc1, gt1, sh2, sc2, gt2 = [jnp.repeat(m, t, axis=0) for m in mods]
        tabs = tuple(jnp.tile(m, (b, 1)) for m in _rope_tables(pos))
    else:
        sh1, sc1, gt1, sh2, sc2, gt2 = [m[:, None, :] for m in mods]
        tabs = _rope_tables(pos)

    outs = _inproj_call(x2d, sc1, sh1, g1, w_in_p, tabs, tm_in, "inproj_" + tag)
    glu, q, qi = outs[:3]
    glu3 = glu.reshape(b, t, CONV_DIM)
    q3, qi3 = q.reshape(b, t, ATTN_DIM), qi.reshape(b, t, -1)

    conv_out = _conv_call(glu3, conv_hist, dw_w, dw_b, ln_g, ln_b, tt_conv, "conv_" + tag)
    new_conv = glu3[:, t - CONV_HIST:, :]

    if past is None:
        k_bf, ki_bf, kt, vt, vt_bf, kit, wit = outs[3:]
        k_sel = min(TOPK_MAX, t // 4)
        pieces = [_attn_t_call(q3, qi3, wit, k_bf.reshape(b, t, KV_DIM), vt_bf, ki_bf.reshape(b, t, IDX_DIM),
                               tq, i, 1, (i + 1) * tq, t, 0, k_sel, "attn_%s%d" % (tag, i)) for i in range(t // tq)]
        attn_out = jnp.concatenate(pieces, axis=2)
        new_k = jnp.transpose(kt.reshape(b, N_KV_HEADS, HEAD_DIM, t), (0, 3, 1, 2))
        new_v = jnp.transpose(vt.reshape(b, N_KV_HEADS, HEAD_DIM, t), (0, 3, 1, 2))
        new_ki = jnp.transpose(kit, (0, 2, 1))
    else:
        k, v, ki, wi = outs[3:]
        ck, cv, cki = past
        k3, v3, ki3 = k.reshape(b, t, KV_DIM), v.reshape(b, t, KV_DIM), ki.reshape(b, t, IDX_DIM)
        k_sel = min(TOPK_MAX, (ck.shape[1] + t) // 4)
        attn_out = _attn_call(q3, qi3, wi.reshape(b, t, -1), k3, v3, ki3,
                              jnp.transpose(ck, (0, 2, 3, 1)), jnp.transpose(cv, (0, 2, 3, 1)),
                              jnp.transpose(cki, (0, 2, 1)), nb_attn, k_sel, "attn_" + tag).reshape(r, ATTN_DIM)
        new_k, new_v, new_ki = k3.reshape(b, t, N_KV_HEADS, HEAD_DIM), v3.reshape(b, t, N_KV_HEADS, HEAD_DIM), ki3

    y, new_ffn = _post_call(x2d, conv_out.reshape(r, CONV_DIM), attn_out,
                            gt1, sc2, sh2, gt2, g2, gf, w_out, w_up, w_down, fdw_w, fdw_b, ffn_hist,
                            tm_post, t, "post_" + tag)
    return (y.reshape(b, t, d), new_k, new_v, new_ki, new_conv, new_ffn)


def kernel(x_prompt, x_sample, cache_k, cache_v, cache_kidx, state_conv, state_ffn_conv, c_prompt, c_sample,
           w_ada, b_ada, norm1_g, w_in, conv_dw_w, conv_dw_b, conv_ln_g, conv_ln_b, w_out, norm2_g,
           w_up, ffn_dw_w, ffn_dw_b, w_down, final_norm_g):
    depth = w_ada.shape[0]
    assert depth == 1, "the final norm is fused into the single layer's last kernel"
    bp, sp, d = x_prompt.shape
    bs, ts, _ = x_sample.shape
    past_len = cache_k.shape[2]
    l = 0
    mod = _mod_call(jnp.concatenate([c_prompt, c_sample], axis=0), w_ada[l], b_ada[l])
    mods_p = jnp.split(mod[:bp], 6, axis=-1)
    mods_s = jnp.split(mod[bp:], 6, axis=-1)
    w = (norm1_g[l].reshape(1, d), _pad_w_in(w_in[l]), conv_dw_w[l], conv_dw_b[l], conv_ln_g[l], conv_ln_b[l],
         w_out[l].astype(BF16), norm2_g[l].reshape(1, d), w_up[l].astype(BF16), ffn_dw_w[l],
         ffn_dw_b[l].reshape(1, -1), w_down[l].astype(BF16), final_norm_g.reshape(1, d))

    conv0 = jnp.zeros((bp, CONV_HIST, CONV_DIM), F32)
    ffn0 = jnp.zeros((bp, FFN_CONV_WIDTH - 1, 2 * D_FF), F32)
    out_p = _layer_group(x_prompt, mods_p, jnp.arange(sp, dtype=I32), conv0, ffn0, None, w,
                         per_row_mod=False, tm_in=512, tt_conv=128, tq=256, tm_post=512, tag="p")
    out_s = _layer_group(x_sample, mods_s, past_len + jnp.arange(ts, dtype=I32), state_conv[l], state_ffn_conv[l],
                         (cache_k[l], cache_v[l], cache_kidx[l]), w,
                         per_row_mod=True, tm_in=bs * ts, tt_conv=ts, tq=ts, tm_post=bs * ts, tag="s", nb_attn=2)
    y_p, k_p, v_p, ki_p, conv_p, ffn_p = out_p
    y_s, k_s, v_s, ki_s, conv_s, ffn_s = out_s
    st = lambda a: a[None]
    return (y_p, y_s, st(k_p), st(v_p), st(ki_p), st(conv_p), st(ffn_p),
            st(k_s), st(v_s), st(ki_s), st(conv_s), st(ffn_s))
```

```python
import functools

import jax
import jax.numpy as jnp
from jax import lax
from jax.experimental import pallas as pl
from jax.experimental.pallas import tpu as pltpu

F32 = jnp.float32
BF16 = jnp.bfloat16
I32 = jnp.int32
I16 = jnp.int16

D_MODEL = 1024
CHUNK = 64
CONV_DIM = 512
CONV_WIDTH = 31
N_HEADS = 8
HEAD_DIM = 64
N_KV_HEADS = 2
ATTN_DIM = N_HEADS * HEAD_DIM
KV_DIM = N_KV_HEADS * HEAD_DIM
ROT_DIM = HEAD_DIM // 4
ROPE_THETA = 500000.0
N_IDX_HEADS = 8
IDX_DIM = 32
TOPK_MAX = 256
D_FF = 2816
FFN_CONV_WIDTH = 3
EPS = 1e-6

LANES = 128
SUBLANES = 8
MXU_N = 256
VMEM_LIMIT = 52 * 1024 * 1024

COL_U, COL_UG, COL_Q, COL_K, COL_V, COL_QI = 0, 512, 1024, 1536, 1664, 1792
COL_KI = 2048
COL_WI = COL_KI + LANES
IN_PAD = COL_WI + LANES

Q_SCALE = HEAD_DIM ** -0.5 * 1.4426950408889634

INT_MIN = -2147483648
KEY_NEG_INF = -2139095040
IDX_BITS = 13


def _order_key(x):
    bits = pltpu.bitcast(x, I32)
    return jnp.where(bits < 0, INT_MIN - bits, bits)


def _cparams(sem):
    return pltpu.CompilerParams(dimension_semantics=sem, vmem_limit_bytes=VMEM_LIMIT)


def _silu(x):
    return x * jax.nn.sigmoid(x)


def _mod_kernel(c_ref, w_ref, b_ref, o_ref):
    s = _silu(c_ref[...]).astype(BF16)
    o_ref[...] = jnp.dot(s, w_ref[...].astype(BF16), preferred_element_type=F32) + b_ref[...]


def _mod_call(c_all, w_ada, b_ada):
    nb, d = c_all.shape
    n = w_ada.shape[1]
    tn = 512
    return pl.pallas_call(
        _mod_kernel,
        grid=(n // tn,),
        in_specs=[pl.BlockSpec((nb, d), lambda j: (0, 0)),
                  pl.BlockSpec((d, tn), lambda j: (0, j)),
                  pl.BlockSpec((1, tn), lambda j: (0, j))],
        out_specs=pl.BlockSpec((nb, tn), lambda j: (0, j)),
        out_shape=jax.ShapeDtypeStruct((nb, n), F32),
        compiler_params=_cparams(("arbitrary",)),
        name="mod",
    )(c_all, w_ada, b_ada.reshape(1, n))


def _rmsnorm_mod(x, g, sc, sh):
    ms = jnp.mean(x * x, axis=-1, keepdims=True)
    return (x * lax.rsqrt(ms + EPS) * g) * (1.0 + sc) + sh


def _inproj_kernel(x_ref, sc_ref, sh_ref, g_ref, w_ref, cos_ref, sa_ref, sb_ref, glu_ref, q_ref, qi_ref, *kv_refs,
                   kv_transposed):
    h = _rmsnorm_mod(x_ref[...], g_ref[...], sc_ref[...], sh_ref[...])
    z = jnp.dot(h.astype(BF16), w_ref[...], preferred_element_type=F32)
    glu_ref[...] = z[:, COL_U:COL_U + CONV_DIM] * jax.nn.sigmoid(z[:, COL_UG:COL_UG + CONV_DIM])
    cos, sa, sb = cos_ref[...], sa_ref[...], sb_ref[...]

    def rope(xs):
        return (xs * cos + pltpu.roll(xs, ROT_DIM // 2, axis=1) * sa
                + pltpu.roll(xs, LANES - ROT_DIM // 2, axis=1) * sb)

    for j in range(ATTN_DIM // LANES):
        c0 = COL_Q + LANES * j
        q_ref[:, LANES * j:LANES * (j + 1)] = (rope(z[:, c0:c0 + LANES]) * Q_SCALE).astype(BF16)
    qi_ref[...] = z[:, COL_QI:COL_QI + N_IDX_HEADS * IDX_DIM].astype(BF16)
    k = rope(z[:, COL_K:COL_K + KV_DIM])
    v = z[:, COL_V:COL_V + KV_DIM]
    ki_slab = z[:, COL_KI:COL_KI + LANES]
    wi_slab = z[:, COL_WI:COL_WI + LANES]
    if kv_transposed:
        kbf_ref, kibf_ref, kt_ref, vt_ref, vtbf_ref, kit_ref, wit_ref = kv_refs
        kbf_ref[...] = k.astype(BF16)
        kibf_ref[...] = ki_slab[:, 0:IDX_DIM].astype(BF16)
        kt_ref[...] = k.T
        vt = v.T
        vt_ref[...] = vt
        vtbf_ref[...] = vt.astype(BF16)
        kit_ref[...] = ki_slab.T[0:IDX_DIM, :]
        wit_ref[...] = wi_slab.T[0:N_IDX_HEADS, :]
    else:
        k_ref, v_ref, ki_ref, wi_ref = kv_refs
        k_ref[...] = k
        v_ref[...] = v
        ki_ref[...] = ki_slab[:, 0:IDX_DIM]
        wi_ref[...] = wi_slab[:, 0:N_IDX_HEADS]


def _inproj_call(x2, sc, sh, g1, w_in_p, tabs, tm, name):
    r, d = x2.shape
    nt = r // tm
    kv_t = sc.ndim == 3
    if kv_t:
        nb = sc.shape[0]
        tpb = nt // nb
        t_len = r // nb
        mod_spec = pl.BlockSpec((None, 1, d), lambda i: (i // tpb, 0, 0))
    else:
        mod_spec = pl.BlockSpec((tm, d), lambda i: (i, 0))
    ntab = tabs[0].shape[0] // tm
    tab_spec = pl.BlockSpec((tm, LANES), lambda i: (i % ntab, 0))

    def rows(width):
        return pl.BlockSpec((tm, width), lambda i: (i, 0))

    def rows_shape(width, dtype):
        return jax.ShapeDtypeStruct((r, width), dtype)

    def cols(width):
        return pl.BlockSpec((None, width, tm), lambda i: (i // tpb, 0, i % tpb))

    def cols_shape(width, dtype):
        return jax.ShapeDtypeStruct((nb, width, t_len), dtype)

    out_specs = [rows(CONV_DIM), rows(ATTN_DIM), rows(N_IDX_HEADS * IDX_DIM)]
    out_shapes = [rows_shape(CONV_DIM, F32), rows_shape(ATTN_DIM, BF16), rows_shape(N_IDX_HEADS * IDX_DIM, BF16)]
    if kv_t:
        out_specs += [rows(KV_DIM), rows(IDX_DIM), cols(KV_DIM), cols(KV_DIM), cols(KV_DIM), cols(IDX_DIM),
                      cols(N_IDX_HEADS)]
        out_shapes += [rows_shape(KV_DIM, BF16), rows_shape(IDX_DIM, BF16), cols_shape(KV_DIM, F32),
                       cols_shape(KV_DIM, F32), cols_shape(KV_DIM, BF16), cols_shape(IDX_DIM, F32),
                       cols_shape(N_IDX_HEADS, F32)]
    else:
        out_specs += [rows(KV_DIM), rows(KV_DIM), rows(IDX_DIM), rows(N_IDX_HEADS)]
        out_shapes += [rows_shape(KV_DIM, F32), rows_shape(KV_DIM, F32), rows_shape(IDX_DIM, F32),
                       rows_shape(N_IDX_HEADS, F32)]
    return pl.pallas_call(
        functools.partial(_inproj_kernel, kv_transposed=kv_t),
        grid=(nt,),
        in_specs=[rows(d), mod_spec, mod_spec,
                  pl.BlockSpec((1, d), lambda i: (0, 0)),
                  pl.BlockSpec((d, IN_PAD), lambda i: (0, 0)),
                  tab_spec, tab_spec, tab_spec],
        out_specs=tuple(out_specs),
        out_shape=tuple(out_shapes),
        compiler_params=_cparams(("arbitrary",)),
        name=name,
    )(x2, sc, sh, g1, w_in_p, *tabs)


CONV_HIST = CONV_WIDTH - 1
CONV_PAD = 32
CONV_RB = 64


def _conv_kernel(glu_ref, hist_ref, w_ref, b_ref, lg_ref, lb_ref, o_ref, ext_ref, y_ref, *, tt):
    t = pl.program_id(1)

    @pl.when(t == 0)
    def _():
        ext_ref[0:CONV_PAD - CONV_HIST, :] = jnp.zeros((CONV_PAD - CONV_HIST, CONV_DIM), F32)
        ext_ref[CONV_PAD - CONV_HIST:CONV_PAD, :] = hist_ref[...]

    @pl.when(t > 0)
    def _():
        ext_ref[0:CONV_PAD, :] = ext_ref[tt:tt + CONV_PAD, :]

    ext_ref[CONV_PAD:CONV_PAD + tt, :] = glu_ref[...]

    off = CONV_PAD - CONV_HIST
    rb = min(CONV_RB, tt)
    for r0 in range(0, tt, rb):
        for c0 in range(0, CONV_DIM, LANES):
            acc = None
            for b in range(SUBLANES):
                span = (CONV_WIDTH - 1 - b) // SUBLANES * SUBLANES
                mis = (off + b) % SUBLANES
                base = off + b - mis + r0
                cover = -(-(mis + rb + span) // SUBLANES) * SUBLANES
                win = ext_ref[base:base + cover, c0:c0 + LANES]
                if mis:
                    win = pltpu.roll(win, cover - mis, axis=0)
                for j in range(b, CONV_WIDTH, SUBLANES):
                    term = win[j - b:j - b + rb, :] * w_ref[j:j + 1, c0:c0 + LANES]
                    acc = term if acc is None else acc + term
            y_ref[r0:r0 + rb, c0:c0 + LANES] = acc + b_ref[:, c0:c0 + LANES]

    y = y_ref[...]
    mu = jnp.mean(y, axis=-1, keepdims=True)
    yc = y - mu
    var = jnp.mean(yc * yc, axis=-1, keepdims=True)
    o_ref[...] = _silu(yc * lax.rsqrt(var + EPS) * lg_ref[...] + lb_ref[...]).astype(BF16)


def _conv_call(glu3, hist, dw_w, dw_b, ln_g, ln_b, tt, name):
    b, t, c = glu3.shape
    vec = pl.BlockSpec((1, c), lambda i, j: (0, 0))
    return pl.pallas_call(
        functools.partial(_conv_kernel, tt=tt),
        grid=(b, t // tt),
        in_specs=[pl.BlockSpec((None, tt, c), lambda i, j: (i, j, 0)),
                  pl.BlockSpec((None, CONV_HIST, c), lambda i, j: (i, 0, 0)),
                  pl.BlockSpec((CONV_WIDTH, c), lambda i, j: (0, 0)),
                  vec, vec, vec],
        out_specs=pl.BlockSpec((None, tt, c), lambda i, j: (i, j, 0)),
        out_shape=jax.ShapeDtypeStruct((b, t, c), BF16),
        scratch_shapes=[pltpu.VMEM((tt + CONV_PAD, c), F32), pltpu.VMEM((tt, c), F32)],
        compiler_params=_cparams(("arbitrary", "arbitrary")),
        name=name,
    )(glu3, hist, dw_w, dw_b.reshape(1, c), ln_g.reshape(1, c), ln_b.reshape(1, c))


CNT_ROWS = 64


def _pad_rows(x, n):
    return jnp.concatenate([x, jnp.zeros((n - x.shape[0], x.shape[1]), x.dtype)], axis=0)


def _attn_kernel(q_ref, qi_ref, wi_ref, kn_ref, vn_ref, kin_ref, kct_ref, vct_ref, kict_ref, o_ref, key_ref, bias_ref,
                 hi_ref, lo_ref, *, nb, tq, past, k_sel):
    rows = nb * tq
    n_keys = past + LANES
    nt = (((1,), (1,)), ((), ()))
    col = lax.broadcasted_iota(I32, (rows, n_keys), 1)
    adm = col < past + tq
    new_ok = lax.broadcasted_iota(I32, (tq, LANES), 1) < tq

    for bb in range(nb):
        kic = kict_ref[bb].astype(BF16)
        kin = _pad_rows(kin_ref[bb].astype(BF16), LANES)
        wi = wi_ref[bb]
        sc_c = jnp.zeros((tq, past), F32)
        sc_n = jnp.zeros((tq, LANES), F32)
        for h in range(N_IDX_HEADS):
            qih = qi_ref[bb, :, IDX_DIM * h:IDX_DIM * (h + 1)]
            sc_c = sc_c + wi[:, h:h + 1] * jnp.maximum(jnp.dot(qih, kic, preferred_element_type=F32), 0.0)
            sc_n = sc_n + wi[:, h:h + 1] * jnp.maximum(
                lax.dot_general(qih, kin, nt, preferred_element_type=F32), 0.0)
        key_ref[bb * tq:(bb + 1) * tq, 0:past] = _order_key(sc_c)
        key_ref[bb * tq:(bb + 1) * tq, past:n_keys] = _order_key(jnp.where(new_ok, sc_n, -jnp.inf))

    keys_all = key_ref[...]
    hi_ref[...] = (keys_all >> 16).astype(I16)
    lo_ref[...] = (keys_all ^ 0x8000).astype(I16)

    rg = min(CNT_ROWS, rows)

    def count_ge16(ref, cands):
        outs = [[] for _ in cands]
        for r0 in range(0, rows, rg):
            cs = [jnp.broadcast_to(c[r0:r0 + rg, :], (rg, LANES)).astype(I16) for c in cands]
            accs = [jnp.zeros((rg, LANES), I16) for _ in cands]
            for c0 in range(0, n_keys, LANES):
                kv = ref[r0:r0 + rg, c0:c0 + LANES]
                accs = [a + jnp.where(kv >= c, I16(1), I16(0)) for a, c in zip(accs, cs)]
            for o, a in zip(outs, accs):
                o.append(jnp.sum(a.astype(I32), axis=1, keepdims=True))
        return [o[0] if len(o) == 1 else jnp.concatenate(o, axis=0) for o in outs]

    bpp = 2 if rows <= 64 else 1

    def search16(ref):
        def bits_step(it, r):
            sh = 16 - bpp * (it + 1)
            step = jnp.left_shift(jnp.int32(1), sh)
            counts = count_ge16(ref, [r + m * step - 32768 for m in range(1, 1 << bpp)])
            inc = sum(jnp.where(n >= k_sel, 1, 0) for n in counts)
            return r + jnp.left_shift(inc, sh)
        return lax.fori_loop(0, 16 // bpp, bits_step, jnp.zeros((rows, 1), I32))

    r_hi = search16(hi_ref)
    h16 = jnp.broadcast_to(r_hi - 32768, (rows, n_keys)).astype(I16)
    hi = hi_ref[...]
    lo_ref[...] = jnp.where(hi == h16, lo_ref[...], jnp.where(hi > h16, I16(32767), I16(-32768)))
    thr = ((r_hi - 32768) << 16) | search16(lo_ref)

    keys = keys_all
    ge = keys >= thr
    cnt_ge = jnp.sum(jnp.where(ge, 1.0, 0.0), axis=1, keepdims=True)
    bias_ref[...] = jnp.where(ge & adm, 0.0, -jnp.inf)
    split_tie = (cnt_ge > k_sel) & (thr != KEY_NEG_INF)
    any_split = jnp.max(jnp.where(split_tie, 1.0, 0.0)) > 0.0

    @pl.when(any_split)
    def _():
        gt = keys > thr
        eq = keys == thr
        need = k_sel - jnp.sum(jnp.where(gt, 1.0, 0.0), axis=1, keepdims=True)

        def idx_step(it, last):
            cand = last + jnp.left_shift(jnp.int32(1), IDX_BITS - 1 - it)
            below = jnp.sum(jnp.where(eq & (col < cand), 1.0, 0.0), axis=1, keepdims=True)
            return jnp.where(below < need, cand, last)

        last = lax.fori_loop(0, IDX_BITS, idx_step, jnp.zeros((rows, 1), I32))
        sel = (gt | (eq & (col <= last))) & adm
        bias_ref[...] = jnp.where(sel, 0.0, -jnp.inf)

    rep = N_HEADS // N_KV_HEADS
    for bb in range(nb):
        bias = bias_ref[bb * tq:(bb + 1) * tq, :]
        for g in range(N_KV_HEADS):
            qg = jnp.concatenate([q_ref[bb, :, HEAD_DIM * (rep * g + r):HEAD_DIM * (rep * g + r + 1)]
                                  for r in range(rep)], axis=0)
            kn = _pad_rows(kn_ref[bb, :, HEAD_DIM * g:HEAD_DIM * (g + 1)].astype(BF16), LANES)
            vn = _pad_rows(vn_ref[bb, :, HEAD_DIM * g:HEAD_DIM * (g + 1)].astype(BF16), LANES)
            logits = jnp.concatenate(
                [jnp.dot(qg, kct_ref[bb, g].astype(BF16), preferred_element_type=F32),
                 lax.dot_general(qg, kn, nt, preferred_element_type=F32)], axis=1)
            ps, ss = [], []
            for r in range(rep):
                lg = logits[r * tq:(r + 1) * tq, :] + bias
                p = jnp.exp2(lg - jnp.max(lg, axis=1, keepdims=True))
                ss.append(jnp.sum(p, axis=1, keepdims=True))
                ps.append(p.astype(BF16))
            p_all = jnp.concatenate(ps, axis=0)
            o = (lax.dot_general(p_all[:, 0:past], vct_ref[bb, g].astype(BF16), nt, preferred_element_type=F32)
                 + jnp.dot(p_all[:, past:n_keys], vn, preferred_element_type=F32))
            for r in range(rep):
                h = rep * g + r
                o_ref[bb, :, HEAD_DIM * h:HEAD_DIM * (h + 1)] = (o[r * tq:(r + 1) * tq, :] / ss[r]).astype(BF16)


def _attn_call(q3, qi3, wi3, k_new, v_new, ki_new, kct, vct, kict, nb, k_sel, name):
    b, tq, _ = q3.shape
    past = kct.shape[-1]
    n_keys = past + LANES
    assert past % LANES == 0 and tq <= LANES and n_keys <= (1 << IDX_BITS) and b % nb == 0

    def rows(width):
        return pl.BlockSpec((nb, tq, width), lambda i: (i, 0, 0))

    kern = functools.partial(_attn_kernel, nb=nb, tq=tq, past=past, k_sel=float(k_sel))
    return pl.pallas_call(
        kern,
        grid=(b // nb,),
        in_specs=[rows(ATTN_DIM), rows(N_IDX_HEADS * IDX_DIM), rows(N_IDX_HEADS),
                  rows(KV_DIM), rows(KV_DIM), rows(IDX_DIM),
                  pl.BlockSpec((nb, N_KV_HEADS, HEAD_DIM, past), lambda i: (i, 0, 0, 0)),
                  pl.BlockSpec((nb, N_KV_HEADS, HEAD_DIM, past), lambda i: (i, 0, 0, 0)),
                  pl.BlockSpec((nb, IDX_DIM, past), lambda i: (i, 0, 0))],
        out_specs=rows(ATTN_DIM),
        out_shape=jax.ShapeDtypeStruct((b, tq, ATTN_DIM), BF16),
        scratch_shapes=[pltpu.VMEM((nb * tq, n_keys), I32), pltpu.VMEM((nb * tq, n_keys), F32),
                        pltpu.VMEM((nb * tq, n_keys), I16), pltpu.VMEM((nb * tq, n_keys), I16)],
        compiler_params=_cparams(("arbitrary",)),
        name=name,
    )(q3, qi3, wi3, k_new, v_new, ki_new, kct, vct, kict)


VT_ROWS = HEAD_DIM + 16
SCORE_KEYS = 256
CNT_KEYS = 64
SUB_KEYS = 32


def _attn_t_kernel(q_ref, qi_ref, wit_ref, k_ref, vt_ref, ki_ref, o_ref, key_ref, bias_ref, hi_ref, lo_ref,
                   d_ref, lg_ref, p_ref, *, tq, n_keys, q0, t_new, past, k_sel):
    j = pl.program_id(1)
    qpos = lax.broadcasted_iota(I32, (1, tq), 1) + (q0 + j * tq)
    limit = past + jnp.minimum((qpos // CHUNK + 1) * CHUNK, t_new)
    nt = (((1,), (1,)), ((), ()))
    sb = SUB_KEYS
    kpos0 = lax.broadcasted_iota(I32, (sb, tq), 0)

    qi_stack = jnp.concatenate([qi_ref[:, IDX_DIM * h:IDX_DIM * (h + 1)] for h in range(N_IDX_HEADS)], axis=0)
    wit = wit_ref[...]
    kc = min(SCORE_KEYS, n_keys)
    for c0 in range(0, n_keys, kc):
        d_ref[...] = lax.dot_general(ki_ref[c0:c0 + kc, :], qi_stack, nt, preferred_element_type=F32)
        for s0 in range(0, kc, sb):
            sc = wit[0:1, :] * jnp.maximum(d_ref[s0:s0 + sb, 0:tq], 0.0)
            for h in range(1, N_IDX_HEADS):
                sc = sc + wit[h:h + 1, :] * jnp.maximum(d_ref[s0:s0 + sb, h * tq:(h + 1) * tq], 0.0)
            keys = _order_key(jnp.where(kpos0 + (c0 + s0) < limit, sc, -jnp.inf))
            key_ref[c0 + s0:c0 + s0 + sb, :] = keys
            hi_ref[c0 + s0:c0 + s0 + sb, :] = (keys >> 16).astype(I16)
            lo_ref[c0 + s0:c0 + s0 + sb, :] = (keys ^ 0x8000).astype(I16)

    sl = min(CNT_KEYS, n_keys)

    def search16(ref):
        def bit_step(it, r):
            cand = r + jnp.left_shift(jnp.int32(1), 15 - it)
            cb = jnp.broadcast_to((cand - 32768).astype(I16), (sl, tq))
            acc = jnp.zeros((sl, tq), I16)
            for c0 in range(0, n_keys, sl):
                acc = acc + jnp.where(ref[c0:c0 + sl, :] >= cb, I16(1), I16(0))
            cnt = jnp.sum(acc.astype(I32), axis=0, keepdims=True)
            return jnp.where(cnt >= k_sel, cand, r)
        return lax.fori_loop(0, 16, bit_step, jnp.zeros((1, tq), I32))

    if n_keys > k_sel:
        r_hi = search16(hi_ref)
        h16 = jnp.broadcast_to((r_hi - 32768).astype(I16), (sb, tq))
        for s0 in range(0, n_keys, sb):
            hi = hi_ref[s0:s0 + sb, :]
            lo_ref[s0:s0 + sb, :] = jnp.where(hi == h16, lo_ref[s0:s0 + sb, :],
                                              jnp.where(hi > h16, I16(32767), I16(-32768)))
        thr = ((r_hi - 32768) << 16) | search16(lo_ref)
    else:
        thr = jnp.full((1, tq), KEY_NEG_INF, I32)

    cnt = jnp.zeros((sb, tq), F32)
    for s0 in range(0, n_keys, sb):
        ge = key_ref[s0:s0 + sb, :] >= thr
        cnt = cnt + jnp.where(ge, 1.0, 0.0)
        bias_ref[s0:s0 + sb, :] = jnp.where(ge & (kpos0 + s0 < limit), 0.0, -jnp.inf)
    cnt_ge = jnp.sum(cnt, axis=0, keepdims=True)
    split_tie = (cnt_ge > k_sel) & (thr != KEY_NEG_INF)
    any_split = jnp.max(jnp.where(split_tie, 1.0, 0.0)) > 0.0

    @pl.when(any_split)
    def _():
        keys = key_ref[...]
        kpos = lax.broadcasted_iota(I32, (n_keys, tq), 0)
        gt = keys > thr
        eq = keys == thr
        need = k_sel - jnp.sum(jnp.where(gt, 1.0, 0.0), axis=0, keepdims=True)

        def idx_step(it, last):
            cand = last + jnp.left_shift(jnp.int32(1), IDX_BITS - 1 - it)
            below = jnp.sum(jnp.where(eq & (kpos < cand), 1.0, 0.0), axis=0, keepdims=True)
            return jnp.where(below < need, cand, last)

        last = lax.fori_loop(0, IDX_BITS, idx_step, jnp.zeros((1, tq), I32))
        sel = (gt | (eq & (kpos <= last))) & (kpos < limit)
        bias_ref[...] = jnp.where(sel, 0.0, -jnp.inf)

    rep = N_HEADS // N_KV_HEADS
    for g in range(N_KV_HEADS):
        qg = jnp.concatenate([q_ref[:, HEAD_DIM * (rep * g + r):HEAD_DIM * (rep * g + r + 1)] for r in range(rep)],
                             axis=0)
        lg_ref[...] = lax.dot_general(k_ref[:, HEAD_DIM * g:HEAD_DIM * (g + 1)], qg, nt, preferred_element_type=F32)
        for r in range(rep):
            cols = slice(r * tq, (r + 1) * tq)
            mx = jnp.full((sb, tq), -jnp.inf, F32)
            for s0 in range(0, n_keys, sb):
                mx = jnp.maximum(mx, lg_ref[s0:s0 + sb, cols] + bias_ref[s0:s0 + sb, :])
            mb = jnp.broadcast_to(jnp.max(mx, axis=0, keepdims=True), (sb, tq))
            for s0 in range(0, n_keys, sb):
                p_ref[s0:s0 + sb, cols] = jnp.exp2(lg_ref[s0:s0 + sb, cols] + bias_ref[s0:s0 + sb, :] - mb).astype(BF16)
        vt_ext = jnp.concatenate([vt_ref[HEAD_DIM * g:HEAD_DIM * (g + 1), :],
                                  jnp.ones((VT_ROWS - HEAD_DIM, n_keys), BF16)], axis=0)
        o = jnp.dot(vt_ext, p_ref[...], preferred_element_type=F32)
        o = o[0:HEAD_DIM, :] / o[HEAD_DIM:HEAD_DIM + 1, :]
        for r in range(rep):
            h = rep * g + r
            o_ref[HEAD_DIM * h:HEAD_DIM * (h + 1), :] = o[:, r * tq:(r + 1) * tq].astype(BF16)


def _attn_t_call(q3, qi3, wit3, k_all, vt_all, ki_all, tq, q_blk0, n_q_blk, n_keys, t_new, past, k_sel, name):
    b = q3.shape[0]
    assert n_keys % LANES == 0 and n_keys <= (1 << IDX_BITS) and n_keys <= k_all.shape[1] and tq % LANES == 0
    kern = functools.partial(_attn_t_kernel, tq=tq, n_keys=n_keys, q0=q_blk0 * tq, t_new=t_new, past=past,
                             k_sel=float(k_sel))
    return pl.pallas_call(
        kern,
        grid=(b, n_q_blk),
        in_specs=[pl.BlockSpec((None, tq, ATTN_DIM), lambda i, j: (i, q_blk0 + j, 0)),
                  pl.BlockSpec((None, tq, N_IDX_HEADS * IDX_DIM), lambda i, j: (i, q_blk0 + j, 0)),
                  pl.BlockSpec((None, N_IDX_HEADS, tq), lambda i, j: (i, 0, q_blk0 + j)),
                  pl.BlockSpec((None, n_keys, KV_DIM), lambda i, j: (i, 0, 0)),
                  pl.BlockSpec((None, KV_DIM, n_keys), lambda i, j: (i, 0, 0)),
                  pl.BlockSpec((None, n_keys, IDX_DIM), lambda i, j: (i, 0, 0))],
        out_specs=pl.BlockSpec((None, ATTN_DIM, tq), lambda i, j: (i, 0, j)),
        out_shape=jax.ShapeDtypeStruct((b, ATTN_DIM, n_q_blk * tq), BF16),
        scratch_shapes=[pltpu.VMEM((n_keys, tq), I32), pltpu.VMEM((n_keys, tq), F32),
                        pltpu.VMEM((n_keys, tq), I16), pltpu.VMEM((n_keys, tq), I16),
                        pltpu.VMEM((min(SCORE_KEYS, n_keys), N_IDX_HEADS * tq), F32),
                        pltpu.VMEM((n_keys, N_HEADS // N_KV_HEADS * tq), F32),
                        pltpu.VMEM((n_keys, N_HEADS // N_KV_HEADS * tq), BF16)],
        compiler_params=_cparams(("arbitrary", "arbitrary")),
        name=name,
    )(q3, qi3, wit3, k_all, vt_all, ki_all)


FF_CHUNKS = ((0, D_FF),)


def _post_kernel(x_ref, conv_ref, attn_ref, gt1_ref, sc2_ref, sh2_ref, gt2_ref, g2_ref, gf_ref,
                 wout_ref, wup_ref, wdn_ref, fw_ref, fb_ref, hist_ref,
                 y_ref, newffn_ref, *scratch, tm, t_len, tiles_per_batch, attn_transposed):
    nseg = max(1, tm // t_len)
    seg = tm // nseg
    i = pl.program_id(0)
    if nseg == 1:
        carry_ref = scratch[0]

        @pl.when(i % tiles_per_batch == 0)
        def _():
            carry_ref[...] = hist_ref[0]

    w_attn = wout_ref[CONV_DIM:CONV_DIM + ATTN_DIM, :]
    if attn_transposed:
        mix_attn = lax.dot_general(attn_ref[...], w_attn, (((0,), (0,)), ((), ())), preferred_element_type=F32)
    else:
        mix_attn = jnp.dot(attn_ref[...], w_attn, preferred_element_type=F32)
    mix = jnp.dot(conv_ref[...], wout_ref[0:CONV_DIM, :], preferred_element_type=F32) + mix_attn
    x1 = x_ref[...] + gt1_ref[...] * mix
    h2 = _rmsnorm_mod(x1, g2_ref[...], sc2_ref[...], sh2_ref[...]).astype(BF16)

    row = lax.broadcasted_iota(I32, (SUBLANES, 1), 0)

    def causal3(u, col0, width):
        w0 = fw_ref[0:1, col0:col0 + width]
        w1 = fw_ref[1:2, col0:col0 + width]
        w2 = fw_ref[2:3, col0:col0 + width]
        outs = []
        for s in range(nseg):
            us = u[s * seg:(s + 1) * seg, :]
            if nseg == 1:
                h0 = carry_ref[0:1, col0:col0 + width]
                h1 = carry_ref[1:2, col0:col0 + width]
            else:
                h0 = hist_ref[s, 0:1, col0:col0 + width]
                h1 = hist_ref[s, 1:2, col0:col0 + width]
            p1 = pltpu.roll(us, 1, axis=0)
            p2 = pltpu.roll(us, 2, axis=0)
            p1 = jnp.concatenate([jnp.where(row == 0, h1, p1[0:SUBLANES, :]), p1[SUBLANES:, :]], axis=0)
            p2 = jnp.concatenate([jnp.where(row == 0, h0, jnp.where(row == 1, h1, p2[0:SUBLANES, :])),
                                  p2[SUBLANES:, :]], axis=0)
            outs.append(us * w2 + p1 * w1 + p2 * w0 + fb_ref[:, col0:col0 + width])
            if nseg == 1:
                carry_ref[:, col0:col0 + width] = us[seg - 2:seg, :]
            else:
                newffn_ref[s, :, col0:col0 + width] = us[seg - 2:seg, :]
        return outs[0] if nseg == 1 else jnp.concatenate(outs, axis=0)

    acc = jnp.zeros((tm, D_MODEL), F32)
    for c0, cw in FF_CHUNKS:
        ua = jnp.dot(h2, wup_ref[:, c0:c0 + cw], preferred_element_type=F32)
        ug = jnp.dot(h2, wup_ref[:, D_FF + c0:D_FF + c0 + cw], preferred_element_type=F32)
        a = causal3(ua, c0, cw)
        g = causal3(ug, D_FF + c0, cw)
        acc = acc + jnp.dot((a * _silu(g)).astype(BF16), wdn_ref[c0:c0 + cw, :], preferred_element_type=F32)

    if nseg == 1:
        @pl.when(i % tiles_per_batch == tiles_per_batch - 1)
        def _():
            newffn_ref[0] = carry_ref[...]

    x2 = x1 + gt2_ref[...] * acc
    ms = jnp.mean(x2 * x2, axis=-1, keepdims=True)
    y_ref[...] = x2 * lax.rsqrt(ms + EPS) * gf_ref[...]


def _post_call(x2d, conv2d, attn, gt1, sc2, sh2, gt2, g2, gf, wout, wup, wdn, fw, fb, hist, tm, t_len, name):
    r, d = x2d.shape
    nt = r // tm
    nb = hist.shape[0]
    nseg = max(1, tm // t_len)
    tpb = max(1, t_len // tm)
    assert nt * nseg == nb * tpb and t_len >= FFN_CONV_WIDTH - 1
    if gt1.ndim == 3:
        mod_spec = pl.BlockSpec((None, 1, d), lambda i: (i // tpb, 0, 0))
    else:
        mod_spec = pl.BlockSpec((tm, d), lambda i: (i, 0))

    def rows(width):
        return pl.BlockSpec((tm, width), lambda i: (i, 0))

    def const(shape):
        return pl.BlockSpec(shape, lambda i: (0,) * len(shape), pipeline_mode=pl.Buffered(1))

    state_spec = pl.BlockSpec((nseg, FFN_CONV_WIDTH - 1, 2 * D_FF), lambda i: (i * nseg // tpb, 0, 0))
    attn_t = attn.ndim == 3
    if attn_t:
        assert nseg == 1
        attn_spec = pl.BlockSpec((None, ATTN_DIM, tm), lambda i: (i // tpb, 0, i % tpb))
    else:
        attn_spec = rows(ATTN_DIM)
    kern = functools.partial(_post_kernel, tm=tm, t_len=t_len, tiles_per_batch=tpb, attn_transposed=attn_t)
    scratch = [pltpu.VMEM((FFN_CONV_WIDTH - 1, 2 * D_FF), F32)] if nseg == 1 else []
    return pl.pallas_call(
        kern,
        grid=(nt,),
        in_specs=[rows(d), rows(CONV_DIM), attn_spec, mod_spec, mod_spec, mod_spec, mod_spec,
                  const((1, d)), const((1, d)),
                  const((CONV_DIM + ATTN_DIM, d)), const((d, 2 * D_FF)), const((D_FF, d)),
                  const((FFN_CONV_WIDTH, 2 * D_FF)), const((1, 2 * D_FF)),
                  state_spec],
        out_specs=(rows(d), state_spec),
        out_shape=(jax.ShapeDtypeStruct((r, d), F32),
                   jax.ShapeDtypeStruct((nb, FFN_CONV_WIDTH - 1, 2 * D_FF), F32)),
        scratch_shapes=scratch,
        compiler_params=_cparams(("arbitrary",)),
        name=name,
    )(x2d, conv2d, attn, gt1, sc2, sh2, gt2, g2, gf, wout, wup, wdn, fw, fb, hist)


def _rope_tables(pos):
    half = ROT_DIM // 2
    inv = 1.0 / (ROPE_THETA ** (jnp.arange(0, ROT_DIM, 2, dtype=F32) / ROT_DIM))
    ang = pos.astype(F32)[:, None] * inv[None, :]
    cos, sin = jnp.cos(ang), jnp.sin(ang)
    t = pos.shape[0]
    rest1 = jnp.ones((t, HEAD_DIM - ROT_DIM), F32)
    rest0 = jnp.zeros((t, HEAD_DIM - ROT_DIM), F32)
    z = jnp.zeros((t, half), F32)
    c64 = jnp.concatenate([cos, cos, rest1], axis=1)
    a64 = jnp.concatenate([z, sin, rest0], axis=1)
    b64 = jnp.concatenate([-sin, z, rest0], axis=1)
    return tuple(jnp.concatenate([m, m], axis=1) for m in (c64, a64, b64))


def _pad_w_in(w_in):
    d = w_in.shape[0]
    return jnp.concatenate(
        [w_in[:, :COL_KI + IDX_DIM], jnp.zeros((d, LANES - IDX_DIM), F32),
         w_in[:, COL_KI + IDX_DIM:], jnp.zeros((d, LANES - N_IDX_HEADS), F32)], axis=1).astype(BF16)


def _layer_group(x, mods, pos, conv_hist, ffn_hist, past, w, *, per_row_mod, tm_in, tt_conv, tq, tm_post, tag, nb_attn=1):
    (g1, w_in_p, dw_w, dw_b, ln_g, ln_b, w_out, g2, w_up, fdw_w, fdw_b, w_down, gf) = w
    b, t, d = x.shape
    r = b * t
    x2d = x.reshape(r, d)
    if per_row_mod:
        sh1, sc1, gt1, sh2, sc2, gt2 = [jnp.repeat(m, t, axis=0) for m in mods]
        tabs = tuple(jnp.tile(m, (b, 1)) for m in _rope_tables(pos))
    else:
        sh1, sc1, gt1, sh2, sc2, gt2 = [m[:, None, :] for m in mods]
        tabs = _rope_tables(pos)

    outs = _inproj_call(x2d, sc1, sh1, g1, w_in_p, tabs, tm_in, "inproj_" + tag)
    glu, q, qi = outs[:3]
    glu3 = glu.reshape(b, t, CONV_DIM)
    q3, qi3 = q.reshape(b, t, ATTN_DIM), qi.reshape(b, t, -1)

    conv_out = _conv_call(glu3, conv_hist, dw_w, dw_b, ln_g, ln_b, tt_conv, "conv_" + tag)
    new_conv = glu3[:, t - CONV_HIST:, :]

    if past is None:
        k_bf, ki_bf, kt, vt, vt_bf, kit, wit = outs[3:]
        k_sel = min(TOPK_MAX, t // 4)
        pieces = [_attn_t_call(q3, qi3, wit, k_bf.reshape(b, t, KV_DIM), vt_bf, ki_bf.reshape(b, t, IDX_DIM),
                               tq, i, 1, (i + 1) * tq, t, 0, k_sel, "attn_%s%d" % (tag, i)) for i in range(t // tq)]
        attn_out = jnp.concatenate(pieces, axis=2)
        new_k = jnp.transpose(kt.reshape(b, N_KV_HEADS, HEAD_DIM, t), (0, 3, 1, 2))
        new_v = jnp.transpose(vt.reshape(b, N_KV_HEADS, HEAD_DIM, t), (0, 3, 1, 2))
        new_ki = jnp.transpose(kit, (0, 2, 1))
    else:
        k, v, ki, wi = outs[3:]
        ck, cv, cki = past
        k3, v3, ki3 = k.reshape(b, t, KV_DIM), v.reshape(b, t, KV_DIM), ki.reshape(b, t, IDX_DIM)
        k_sel = min(TOPK_MAX, (ck.shape[1] + t) // 4)
        attn_out = _attn_call(q3, qi3, wi.reshape(b, t, -1), k3, v3, ki3,
                              jnp.transpose(ck, (0, 2, 3, 1)), jnp.transpose(cv, (0, 2, 3, 1)),
                              jnp.transpose(cki, (0, 2, 1)), nb_attn, k_sel, "attn_" + tag).reshape(r, ATTN_DIM)
        new_k, new_v, new_ki = k3.reshape(b, t, N_KV_HEADS, HEAD_DIM), v3.reshape(b, t, N_KV_HEADS, HEAD_DIM), ki3

    y, new_ffn = _post_call(x2d, conv_out.reshape(r, CONV_DIM), attn_out,
                            gt1, sc2, sh2, gt2, g2, gf, w_out, w_up, w_down, fdw_w, fdw_b, ffn_hist,
                            tm_post, t, "post_" + tag)
    return (y.reshape(b, t, d), new_k, new_v, new_ki, new_conv, new_ffn)


def kernel(x_prompt, x_sample, cache_k, cache_v, cache_kidx, state_conv, state_ffn_conv, c_prompt, c_sample,
           w_ada, b_ada, norm1_g, w_in, conv_dw_w, conv_dw_b, conv_ln_g, conv_ln_b, w_out, norm2_g,
           w_up, ffn_dw_w, ffn_dw_b, w_down, final_norm_g):
    depth = w_ada.shape[0]
    assert depth == 1, "the final norm is fused into the single layer's last kernel"
    bp, sp, d = x_prompt.shape
    bs, ts, _ = x_sample.shape
    past_len = cache_k.shape[2]
    l = 0
    mod = _mod_call(jnp.concatenate([c_prompt, c_sample], axis=0), w_ada[l], b_ada[l])
    mods_p = jnp.split(mod[:bp], 6, axis=-1)
    mods_s = jnp.split(mod[bp:], 6, axis=-1)
    w = (norm1_g[l].reshape(1, d), _pad_w_in(w_in[l]), conv_dw_w[l], conv_dw_b[l], conv_ln_g[l], conv_ln_b[l],
         w_out[l].astype(BF16), norm2_g[l].reshape(1, d), w_up[l].astype(BF16), ffn_dw_w[l],
         ffn_dw_b[l].reshape(1, -1), w_down[l].astype(BF16), final_norm_g.reshape(1, d))

    conv0 = jnp.zeros((bp, CONV_HIST, CONV_DIM), F32)
    ffn0 = jnp.zeros((bp, FFN_CONV_WIDTH - 1, 2 * D_FF), F32)
    out_p = _layer_group(x_prompt, mods_p, jnp.arange(sp, dtype=I32), conv0, ffn0, None, w,
                         per_row_mod=False, tm_in=512, tt_conv=128, tq=256, tm_post=512, tag="p")
    out_s = _layer_group(x_sample, mods_s, past_len + jnp.arange(ts, dtype=I32), state_conv[l], state_ffn_conv[l],
                         (cache_k[l], cache_v[l], cache_kidx[l]), w,
                         per_row_mod=True, tm_in=bs * ts, tt_conv=ts, tq=ts, tm_post=bs * ts, tag="s", nb_attn=2)
    y_p, k_p, v_p, ki_p, conv_p, ffn_p = out_p
    y_s, k_s, v_s, ki_s, conv_s, ffn_s = out_s
    st = lambda a: a[None]
    return (y_p, y_s, st(k_p), st(v_p), st(ki_p), st(conv_p), st(ffn_p),
            st(k_s), st(v_s), st(ki_s), st(conv_s), st(ffn_s))
```

```python
import functools

import jax
import jax.numpy as jnp
from jax import lax
from jax.experimental import pallas as pl
from jax.experimental.pallas import tpu as pltpu

F32 = jnp.float32
BF16 = jnp.bfloat16
I32 = jnp.int32
I16 = jnp.int16

D_MODEL = 1024
CHUNK = 64
CONV_DIM = 512
CONV_WIDTH = 31
N_HEADS = 8
HEAD_DIM = 64
N_KV_HEADS = 2
ATTN_DIM = N_HEADS * HEAD_DIM
KV_DIM = N_KV_HEADS * HEAD_DIM
ROT_DIM = HEAD_DIM // 4
ROPE_THETA = 500000.0
N_IDX_HEADS = 8
IDX_DIM = 32
TOPK_MAX = 256
D_FF = 2816
FFN_CONV_WIDTH = 3
EPS = 1e-6

LANES = 128
SUBLANES = 8
MXU_N = 256
VMEM_LIMIT = 52 * 1024 * 1024

COL_U, COL_UG, COL_Q, COL_K, COL_V, COL_QI = 0, 512, 1024, 1536, 1664, 1792
COL_KI = 2048
COL_WI = COL_KI + LANES
IN_PAD = COL_WI + LANES

Q_SCALE = HEAD_DIM ** -0.5 * 1.4426950408889634

INT_MIN = -2147483648
KEY_NEG_INF = -2139095040
IDX_BITS = 13


def _order_key(x):
    bits = pltpu.bitcast(x, I32)
    return jnp.where(bits < 0, INT_MIN - bits, bits)


def _cparams(sem):
    return pltpu.CompilerParams(dimension_semantics=sem, vmem_limit_bytes=VMEM_LIMIT)


def _silu(x):
    return x * jax.nn.sigmoid(x)


def _mod_kernel(c_ref, w_ref, b_ref, o_ref):
    s = _silu(c_ref[...]).astype(BF16)
    o_ref[...] = jnp.dot(s, w_ref[...].astype(BF16), preferred_element_type=F32) + b_ref[...]


def _mod_call(c_all, w_ada, b_ada):
    nb, d = c_all.shape
    n = w_ada.shape[1]
    tn = 512
    return pl.pallas_call(
        _mod_kernel,
        grid=(n // tn,),
        in_specs=[pl.BlockSpec((nb, d), lambda j: (0, 0)),
                  pl.BlockSpec((d, tn), lambda j: (0, j)),
                  pl.BlockSpec((1, tn), lambda j: (0, j))],
        out_specs=pl.BlockSpec((nb, tn), lambda j: (0, j)),
        out_shape=jax.ShapeDtypeStruct((nb, n), F32),
        compiler_params=_cparams(("arbitrary",)),
        name="mod",
    )(c_all, w_ada, b_ada.reshape(1, n))


def _rmsnorm_mod(x, g, sc, sh):
    ms = jnp.mean(x * x, axis=-1, keepdims=True)
    return (x * lax.rsqrt(ms + EPS) * g) * (1.0 + sc) + sh


def _inproj_kernel(x_ref, sc_ref, sh_ref, g_ref, w_ref, cos_ref, sa_ref, sb_ref, glu_ref, q_ref, qi_ref, *kv_refs,
                   kv_transposed):
    h = _rmsnorm_mod(x_ref[...], g_ref[...], sc_ref[...], sh_ref[...])
    z = jnp.dot(h.astype(BF16), w_ref[...], preferred_element_type=F32)
    glu_ref[...] = z[:, COL_U:COL_U + CONV_DIM] * jax.nn.sigmoid(z[:, COL_UG:COL_UG + CONV_DIM])
    cos, sa, sb = cos_ref[...], sa_ref[...], sb_ref[...]

    def rope(xs):
        return (xs * cos + pltpu.roll(xs, ROT_DIM // 2, axis=1) * sa
                + pltpu.roll(xs, LANES - ROT_DIM // 2, axis=1) * sb)

    for j in range(ATTN_DIM // LANES):
        c0 = COL_Q + LANES * j
        q_ref[:, LANES * j:LANES * (j + 1)] = (rope(z[:, c0:c0 + LANES]) * Q_SCALE).astype(BF16)
    qi_ref[...] = z[:, COL_QI:COL_QI + N_IDX_HEADS * IDX_DIM].astype(BF16)
    k = rope(z[:, COL_K:COL_K + KV_DIM])
    v = z[:, COL_V:COL_V + KV_DIM]
    ki_slab = z[:, COL_KI:COL_KI + LANES]
    wi_slab = z[:, COL_WI:COL_WI + LANES]
    if kv_transposed:
        kbf_ref, kibf_ref, kt_ref, vt_ref, vtbf_ref, kit_ref, wit_ref = kv_refs
        kbf_ref[...] = k.astype(BF16)
        kibf_ref[...] = ki_slab[:, 0:IDX_DIM].astype(BF16)
        kt_ref[...] = k.T
        vt = v.T
        vt_ref[...] = vt
        vtbf_ref[...] = vt.astype(BF16)
        kit_ref[...] = ki_slab.T[0:IDX_DIM, :]
        wit_ref[...] = wi_slab.T[0:N_IDX_HEADS, :]
    else:
        k_ref, v_ref, ki_ref, wi_ref = kv_refs
        k_ref[...] = k
        v_ref[...] = v
        ki_ref[...] = ki_slab[:, 0:IDX_DIM]
        wi_ref[...] = wi_slab[:, 0:N_IDX_HEADS]


def _inproj_call(x2, sc, sh, g1, w_in_p, tabs, tm, name):
    r, d = x2.shape
    nt = r // tm
    kv_t = sc.ndim == 3
    if kv_t:
        nb = sc.shape[0]
        tpb = nt // nb
        t_len = r // nb
        mod_spec = pl.BlockSpec((None, 1, d), lambda i: (i // tpb, 0, 0))
    else:
        mod_spec = pl.BlockSpec((tm, d), lambda i: (i, 0))
    ntab = tabs[0].shape[0] // tm
    tab_spec = pl.BlockSpec((tm, LANES), lambda i: (i % ntab, 0))

    def rows(width):
        return pl.BlockSpec((tm, width), lambda i: (i, 0))

    def rows_shape(width, dtype):
        return jax.ShapeDtypeStruct((r, width), dtype)

    def cols(width):
        return pl.BlockSpec((None, width, tm), lambda i: (i // tpb, 0, i % tpb))

    def cols_shape(width, dtype):
        return jax.ShapeDtypeStruct((nb, width, t_len), dtype)

    out_specs = [rows(CONV_DIM), rows(ATTN_DIM), rows(N_IDX_HEADS * IDX_DIM)]
    out_shapes = [rows_shape(CONV_DIM, F32), rows_shape(ATTN_DIM, BF16), rows_shape(N_IDX_HEADS * IDX_DIM, BF16)]
    if kv_t:
        out_specs += [rows(KV_DIM), rows(IDX_DIM), cols(KV_DIM), cols(KV_DIM), cols(KV_DIM), cols(IDX_DIM),
                      cols(N_IDX_HEADS)]
        out_shapes += [rows_shape(KV_DIM, BF16), rows_shape(IDX_DIM, BF16), cols_shape(KV_DIM, F32),
                       cols_shape(KV_DIM, F32), cols_shape(KV_DIM, BF16), cols_shape(IDX_DIM, F32),
                       cols_shape(N_IDX_HEADS, F32)]
    else:
        out_specs += [rows(KV_DIM), rows(KV_DIM), rows(IDX_DIM), rows(N_IDX_HEADS)]
        out_shapes += [rows_shape(KV_DIM, F32), rows_shape(KV_DIM, F32), rows_shape(IDX_DIM, F32),
                       rows_shape(N_IDX_HEADS, F32)]
    return pl.pallas_call(
        functools.partial(_inproj_kernel, kv_transposed=kv_t),
        grid=(nt,),
        in_specs=[rows(d), mod_spec, mod_spec,
                  pl.BlockSpec((1, d), lambda i: (0, 0)),
                  pl.BlockSpec((d, IN_PAD), lambda i: (0, 0)),
                  tab_spec, tab_spec, tab_spec],
        out_specs=tuple(out_specs),
        out_shape=tuple(out_shapes),
        compiler_params=_cparams(("arbitrary",)),
        name=name,
    )(x2, sc, sh, g1, w_in_p, *tabs)


CONV_HIST = CONV_WIDTH - 1
CONV_PAD = 32
CONV_RB = 64


def _conv_kernel(glu_ref, hist_ref, w_ref, b_ref, lg_ref, lb_ref, o_ref, ext_ref, y_ref, *, tt):
    t = pl.program_id(1)

    @pl.when(t == 0)
    def _():
        ext_ref[0:CONV_PAD - CONV_HIST, :] = jnp.zeros((CONV_PAD - CONV_HIST, CONV_DIM), F32)
        ext_ref[CONV_PAD - CONV_HIST:CONV_PAD, :] = hist_ref[...]

    @pl.when(t > 0)
    def _():
        ext_ref[0:CONV_PAD, :] = ext_ref[tt:tt + CONV_PAD, :]

    ext_ref[CONV_PAD:CONV_PAD + tt, :] = glu_ref[...]

    off = CONV_PAD - CONV_HIST
    rb = min(CONV_RB, tt)
    for r0 in range(0, tt, rb):
        for c0 in range(0, CONV_DIM, LANES):
            acc = None
            for b in range(SUBLANES):
                span = (CONV_WIDTH - 1 - b) // SUBLANES * SUBLANES
                mis = (off + b) % SUBLANES
                base = off + b - mis + r0
                cover = -(-(mis + rb + span) // SUBLANES) * SUBLANES
                win = ext_ref[base:base + cover, c0:c0 + LANES]
                if mis:
                    win = pltpu.roll(win, cover - mis, axis=0)
                for j in range(b, CONV_WIDTH, SUBLANES):
                    term = win[j - b:j - b + rb, :] * w_ref[j:j + 1, c0:c0 + LANES]
                    acc = term if acc is None else acc + term
            y_ref[r0:r0 + rb, c0:c0 + LANES] = acc + b_ref[:, c0:c0 + LANES]

    y = y_ref[...]
    mu = jnp.mean(y, axis=-1, keepdims=True)
    yc = y - mu
    var = jnp.mean(yc * yc, axis=-1, keepdims=True)
    o_ref[...] = _silu(yc * lax.rsqrt(var + EPS) * lg_ref[...] + lb_ref[...]).astype(BF16)


def _conv_call(glu3, hist, dw_w, dw_b, ln_g, ln_b, tt, name):
    b, t, c = glu3.shape
    vec = pl.BlockSpec((1, c), lambda i, j: (0, 0))
    return pl.pallas_call(
        functools.partial(_conv_kernel, tt=tt),
        grid=(b, t // tt),
        in_specs=[pl.BlockSpec((None, tt, c), lambda i, j: (i, j, 0)),
                  pl.BlockSpec((None, CONV_HIST, c), lambda i, j: (i, 0, 0)),
                  pl.BlockSpec((CONV_WIDTH, c), lambda i, j: (0, 0)),
                  vec, vec, vec],
        out_specs=pl.BlockSpec((None, tt, c), lambda i, j: (i, j, 0)),
        out_shape=jax.ShapeDtypeStruct((b, t, c), BF16),
        scratch_shapes=[pltpu.VMEM((tt + CONV_PAD, c), F32), pltpu.VMEM((tt, c), F32)],
        compiler_params=_cparams(("arbitrary", "arbitrary")),
        name=name,
    )(glu3, hist, dw_w, dw_b.reshape(1, c), ln_g.reshape(1, c), ln_b.reshape(1, c))


CNT_ROWS = 64


def _pad_rows(x, n):
    return jnp.concatenate([x, jnp.zeros((n - x.shape[0], x.shape[1]), x.dtype)], axis=0)


def _attn_kernel(q_ref, qi_ref, wi_ref, kn_ref, vn_ref, kin_ref, kct_ref, vct_ref, kict_ref, o_ref, key_ref, bias_ref,
                 *, nb, tq, past, k_sel):
    rows = nb * tq
    n_keys = past + LANES
    nt = (((1,), (1,)), ((), ()))
    col = lax.broadcasted_iota(I32, (rows, n_keys), 1)
    adm = col < past + tq
    new_ok = lax.broadcasted_iota(I32, (tq, LANES), 1) < tq

    for bb in range(nb):
        kic = kict_ref[bb].astype(BF16)
        kin = _pad_rows(kin_ref[bb].astype(BF16), LANES)
        wi = wi_ref[bb]
        sc_c = jnp.zeros((tq, past), F32)
        sc_n = jnp.zeros((tq, LANES), F32)
        for h in range(N_IDX_HEADS):
            qih = qi_ref[bb, :, IDX_DIM * h:IDX_DIM * (h + 1)]
            sc_c = sc_c + wi[:, h:h + 1] * jnp.maximum(jnp.dot(qih, kic, preferred_element_type=F32), 0.0)
            sc_n = sc_n + wi[:, h:h + 1] * jnp.maximum(
                lax.dot_general(qih, kin, nt, preferred_element_type=F32), 0.0)
        key_ref[bb * tq:(bb + 1) * tq, 0:past] = _order_key(sc_c)
        key_ref[bb * tq:(bb + 1) * tq, past:n_keys] = _order_key(jnp.where(new_ok, sc_n, -jnp.inf))

    rg = min(CNT_ROWS, rows)

    def count_ge(cands):
        outs = [[] for _ in cands]
        for r0 in range(0, rows, rg):
            cs = [jnp.broadcast_to(c[r0:r0 + rg, :], (rg, LANES)) for c in cands]
            accs = [jnp.zeros((rg, LANES), F32) for _ in cands]
            for c0 in range(0, n_keys, LANES):
                kv = key_ref[r0:r0 + rg, c0:c0 + LANES]
                accs = [a + jnp.where(kv >= c, 1.0, 0.0) for a, c in zip(accs, cs)]
            for o, a in zip(outs, accs):
                o.append(jnp.sum(a, axis=1, keepdims=True))
        return [o[0] if len(o) == 1 else jnp.concatenate(o, axis=0) for o in outs]

    bpp = 2 if rows <= 64 else 1

    def bits_step(it, r):
        sh = 32 - bpp * (it + 1)
        step = jnp.left_shift(jnp.int32(1), sh)
        counts = count_ge([r + m * step for m in range(1, 1 << bpp)])
        inc = sum(jnp.where(n >= k_sel, 1, 0) for n in counts)
        return r + jnp.left_shift(inc, sh)

    thr = lax.fori_loop(0, 32 // bpp, bits_step, jnp.full((rows, 1), INT_MIN, I32))

    keys = key_ref[...]
    ge = keys >= thr
    cnt_ge = jnp.sum(jnp.where(ge, 1.0, 0.0), axis=1, keepdims=True)
    bias_ref[...] = jnp.where(ge & adm, 0.0, -jnp.inf)
    split_tie = (cnt_ge > k_sel) & (thr != KEY_NEG_INF)
    any_split = jnp.max(jnp.where(split_tie, 1.0, 0.0)) > 0.0

    @pl.when(any_split)
    def _():
        gt = keys > thr
        eq = keys == thr
        need = k_sel - jnp.sum(jnp.where(gt, 1.0, 0.0), axis=1, keepdims=True)

        def idx_step(it, last):
            cand = last + jnp.left_shift(jnp.int32(1), IDX_BITS - 1 - it)
            below = jnp.sum(jnp.where(eq & (col < cand), 1.0, 0.0), axis=1, keepdims=True)
            return jnp.where(below < need, cand, last)

        last = lax.fori_loop(0, IDX_BITS, idx_step, jnp.zeros((rows, 1), I32))
        sel = (gt | (eq & (col <= last))) & adm
        bias_ref[...] = jnp.where(sel, 0.0, -jnp.inf)

    rep = N_HEADS // N_KV_HEADS
    for bb in range(nb):
        bias = bias_ref[bb * tq:(bb + 1) * tq, :]
        for g in range(N_KV_HEADS):
            qg = jnp.concatenate([q_ref[bb, :, HEAD_DIM * (rep * g + r):HEAD_DIM * (rep * g + r + 1)]
                                  for r in range(rep)], axis=0)
            kn = _pad_rows(kn_ref[bb, :, HEAD_DIM * g:HEAD_DIM * (g + 1)].astype(BF16), LANES)
            vn = _pad_rows(vn_ref[bb, :, HEAD_DIM * g:HEAD_DIM * (g + 1)].astype(BF16), LANES)
            logits = jnp.concatenate(
                [jnp.dot(qg, kct_ref[bb, g].astype(BF16), preferred_element_type=F32),
                 lax.dot_general(qg, kn, nt, preferred_element_type=F32)], axis=1)
            ps, ss = [], []
            for r in range(rep):
                lg = logits[r * tq:(r + 1) * tq, :] + bias
                p = jnp.exp2(lg - jnp.max(lg, axis=1, keepdims=True))
                ss.append(jnp.sum(p, axis=1, keepdims=True))
                ps.append(p.astype(BF16))
            p_all = jnp.concatenate(ps, axis=0)
            o = (lax.dot_general(p_all[:, 0:past], vct_ref[bb, g].astype(BF16), nt, preferred_element_type=F32)
                 + jnp.dot(p_all[:, past:n_keys], vn, preferred_element_type=F32))
            for r in range(rep):
                h = rep * g + r
                o_ref[bb, :, HEAD_DIM * h:HEAD_DIM * (h + 1)] = (o[r * tq:(r + 1) * tq, :] / ss[r]).astype(BF16)


def _attn_call(q3, qi3, wi3, k_new, v_new, ki_new, kct, vct, kict, nb, k_sel, name):
    b, tq, _ = q3.shape
    past = kct.shape[-1]
    n_keys = past + LANES
    assert past % LANES == 0 and tq <= LANES and n_keys <= (1 << IDX_BITS) and b % nb == 0

    def rows(width):
        return pl.BlockSpec((nb, tq, width), lambda i: (i, 0, 0))

    kern = functools.partial(_attn_kernel, nb=nb, tq=tq, past=past, k_sel=float(k_sel))
    return pl.pallas_call(
        kern,
        grid=(b // nb,),
        in_specs=[rows(ATTN_DIM), rows(N_IDX_HEADS * IDX_DIM), rows(N_IDX_HEADS),
                  rows(KV_DIM), rows(KV_DIM), rows(IDX_DIM),
                  pl.BlockSpec((nb, N_KV_HEADS, HEAD_DIM, past), lambda i: (i, 0, 0, 0)),
                  pl.BlockSpec((nb, N_KV_HEADS, HEAD_DIM, past), lambda i: (i, 0, 0, 0)),
                  pl.BlockSpec((nb, IDX_DIM, past), lambda i: (i, 0, 0))],
        out_specs=rows(ATTN_DIM),
        out_shape=jax.ShapeDtypeStruct((b, tq, ATTN_DIM), BF16),
        scratch_shapes=[pltpu.VMEM((nb * tq, n_keys), I32), pltpu.VMEM((nb * tq, n_keys), F32)],
        compiler_params=_cparams(("arbitrary",)),
        name=name,
    )(q3, qi3, wi3, k_new, v_new, ki_new, kct, vct, kict)


VT_ROWS = HEAD_DIM + 16
SCORE_KEYS = 256
CNT_KEYS = 64
SUB_KEYS = 32


def _attn_t_kernel(q_ref, qi_ref, wit_ref, k_ref, vt_ref, ki_ref, o_ref, key_ref, bias_ref, hi_ref, lo_ref,
                   d_ref, lg_ref, p_ref, *, tq, n_keys, q0, t_new, past, k_sel):
    j = pl.program_id(1)
    qpos = lax.broadcasted_iota(I32, (1, tq), 1) + (q0 + j * tq)
    limit = past + jnp.minimum((qpos // CHUNK + 1) * CHUNK, t_new)
    nt = (((1,), (1,)), ((), ()))
    sb = SUB_KEYS
    kpos0 = lax.broadcasted_iota(I32, (sb, tq), 0)

    qi_stack = jnp.concatenate([qi_ref[:, IDX_DIM * h:IDX_DIM * (h + 1)] for h in range(N_IDX_HEADS)], axis=0)
    wit = wit_ref[...]
    kc = min(SCORE_KEYS, n_keys)
    for c0 in range(0, n_keys, kc):
        d_ref[...] = lax.dot_general(ki_ref[c0:c0 + kc, :], qi_stack, nt, preferred_element_type=F32)
        for s0 in range(0, kc, sb):
            sc = wit[0:1, :] * jnp.maximum(d_ref[s0:s0 + sb, 0:tq], 0.0)
            for h in range(1, N_IDX_HEADS):
                sc = sc + wit[h:h + 1, :] * jnp.maximum(d_ref[s0:s0 + sb, h * tq:(h + 1) * tq], 0.0)
            keys = _order_key(jnp.where(kpos0 + (c0 + s0) < limit, sc, -jnp.inf))
            key_ref[c0 + s0:c0 + s0 + sb, :] = keys
            hi_ref[c0 + s0:c0 + s0 + sb, :] = (keys >> 16).astype(I16)
            lo_ref[c0 + s0:c0 + s0 + sb, :] = (keys ^ 0x8000).astype(I16)

    sl = min(CNT_KEYS, n_keys)

    def search16(ref):
        def bit_step(it, r):
            cand = r + jnp.left_shift(jnp.int32(1), 15 - it)
            cb = jnp.broadcast_to((cand - 32768).astype(I16), (sl, tq))
            acc = jnp.zeros((sl, tq), I16)
            for c0 in range(0, n_keys, sl):
                acc = acc + jnp.where(ref[c0:c0 + sl, :] >= cb, I16(1), I16(0))
            cnt = jnp.sum(acc.astype(I32), axis=0, keepdims=True)
            return jnp.where(cnt >= k_sel, cand, r)
        return lax.fori_loop(0, 16, bit_step, jnp.zeros((1, tq), I32))

    if n_keys > k_sel:
        r_hi = search16(hi_ref)
        h16 = jnp.broadcast_to((r_hi - 32768).astype(I16), (sb, tq))
        for s0 in range(0, n_keys, sb):
            hi = hi_ref[s0:s0 + sb, :]
            lo_ref[s0:s0 + sb, :] = jnp.where(hi == h16, lo_ref[s0:s0 + sb, :],
                                              jnp.where(hi > h16, I16(32767), I16(-32768)))
        thr = ((r_hi - 32768) << 16) | search16(lo_ref)
    else:
        thr = jnp.full((1, tq), KEY_NEG_INF, I32)

    cnt = jnp.zeros((sb, tq), F32)
    for s0 in range(0, n_keys, sb):
        ge = key_ref[s0:s0 + sb, :] >= thr
        cnt = cnt + jnp.where(ge, 1.0, 0.0)
        bias_ref[s0:s0 + sb, :] = jnp.where(ge & (kpos0 + s0 < limit), 0.0, -jnp.inf)
    cnt_ge = jnp.sum(cnt, axis=0, keepdims=True)
    split_tie = (cnt_ge > k_sel) & (thr != KEY_NEG_INF)
    any_split = jnp.max(jnp.where(split_tie, 1.0, 0.0)) > 0.0

    @pl.when(any_split)
    def _():
        keys = key_ref[...]
        kpos = lax.broadcasted_iota(I32, (n_keys, tq), 0)
        gt = keys > thr
        eq = keys == thr
        need = k_sel - jnp.sum(jnp.where(gt, 1.0, 0.0), axis=0, keepdims=True)

        def idx_step(it, last):
            cand = last + jnp.left_shift(jnp.int32(1), IDX_BITS - 1 - it)
            below = jnp.sum(jnp.where(eq & (kpos < cand), 1.0, 0.0), axis=0, keepdims=True)
            return jnp.where(below < need, cand, last)

        last = lax.fori_loop(0, IDX_BITS, idx_step, jnp.zeros((1, tq), I32))
        sel = (gt | (eq & (kpos <= last))) & (kpos < limit)
        bias_ref[...] = jnp.where(sel, 0.0, -jnp.inf)

    rep = N_HEADS // N_KV_HEADS
    for g in range(N_KV_HEADS):
        qg = jnp.concatenate([q_ref[:, HEAD_DIM * (rep * g + r):HEAD_DIM * (rep * g + r + 1)] for r in range(rep)],
                             axis=0)
        lg_ref[...] = lax.dot_general(k_ref[:, HEAD_DIM * g:HEAD_DIM * (g + 1)], qg, nt, preferred_element_type=F32)
        for r in range(rep):
            cols = slice(r * tq, (r + 1) * tq)
            mx = jnp.full((sb, tq), -jnp.inf, F32)
            for s0 in range(0, n_keys, sb):
                mx = jnp.maximum(mx, lg_ref[s0:s0 + sb, cols] + bias_ref[s0:s0 + sb, :])
            mb = jnp.broadcast_to(jnp.max(mx, axis=0, keepdims=True), (sb, tq))
            for s0 in range(0, n_keys, sb):
                p_ref[s0:s0 + sb, cols] = jnp.exp2(lg_ref[s0:s0 + sb, cols] + bias_ref[s0:s0 + sb, :] - mb).astype(BF16)
        vt_ext = jnp.concatenate([vt_ref[HEAD_DIM * g:HEAD_DIM * (g + 1), :],
                                  jnp.ones((VT_ROWS - HEAD_DIM, n_keys), BF16)], axis=0)
        o = jnp.dot(vt_ext, p_ref[...], preferred_element_type=F32)
        o = o[0:HEAD_DIM, :] / o[HEAD_DIM:HEAD_DIM + 1, :]
        for r in range(rep):
            h = rep * g + r
            o_ref[HEAD_DIM * h:HEAD_DIM * (h + 1), :] = o[:, r * tq:(r + 1) * tq].astype(BF16)


def _attn_t_call(q3, qi3, wit3, k_all, vt_all, ki_all, tq, q_blk0, n_q_blk, n_keys, t_new, past, k_sel, name):
    b = q3.shape[0]
    assert n_keys % LANES == 0 and n_keys <= (1 << IDX_BITS) and n_keys <= k_all.shape[1] and tq % LANES == 0
    kern = functools.partial(_attn_t_kernel, tq=tq, n_keys=n_keys, q0=q_blk0 * tq, t_new=t_new, past=past,
                             k_sel=float(k_sel))
    return pl.pallas_call(
        kern,
        grid=(b, n_q_blk),
        in_specs=[pl.BlockSpec((None, tq, ATTN_DIM), lambda i, j: (i, q_blk0 + j, 0)),
                  pl.BlockSpec((None, tq, N_IDX_HEADS * IDX_DIM), lambda i, j: (i, q_blk0 + j, 0)),
                  pl.BlockSpec((None, N_IDX_HEADS, tq), lambda i, j: (i, 0, q_blk0 + j)),
                  pl.BlockSpec((None, n_keys, KV_DIM), lambda i, j: (i, 0, 0)),
                  pl.BlockSpec((None, KV_DIM, n_keys), lambda i, j: (i, 0, 0)),
                  pl.BlockSpec((None, n_keys, IDX_DIM), lambda i, j: (i, 0, 0))],
        out_specs=pl.BlockSpec((None, ATTN_DIM, tq), lambda i, j: (i, 0, j)),
        out_shape=jax.ShapeDtypeStruct((b, ATTN_DIM, n_q_blk * tq), BF16),
        scratch_shapes=[pltpu.VMEM((n_keys, tq), I32), pltpu.VMEM((n_keys, tq), F32),
                        pltpu.VMEM((n_keys, tq), I16), pltpu.VMEM((n_keys, tq), I16),
                        pltpu.VMEM((min(SCORE_KEYS, n_keys), N_IDX_HEADS * tq), F32),
                        pltpu.VMEM((n_keys, N_HEADS // N_KV_HEADS * tq), F32),
                        pltpu.VMEM((n_keys, N_HEADS // N_KV_HEADS * tq), BF16)],
        compiler_params=_cparams(("arbitrary", "arbitrary")),
        name=name,
    )(q3, qi3, wit3, k_all, vt_all, ki_all)


FF_CHUNKS = ((0, D_FF),)


def _post_kernel(x_ref, conv_ref, attn_ref, gt1_ref, sc2_ref, sh2_ref, gt2_ref, g2_ref, gf_ref,
                 wout_ref, wup_ref, wdn_ref, fw_ref, fb_ref, hist_ref,
                 y_ref, newffn_ref, *scratch, tm, t_len, tiles_per_batch, attn_transposed):
    nseg = max(1, tm // t_len)
    seg = tm // nseg
    i = pl.program_id(0)
    if nseg == 1:
        carry_ref = scratch[0]

        @pl.when(i % tiles_per_batch == 0)
        def _():
            carry_ref[...] = hist_ref[0]

    w_attn = wout_ref[CONV_DIM:CONV_DIM + ATTN_DIM, :]
    if attn_transposed:
        mix_attn = lax.dot_general(attn_ref[...], w_attn, (((0,), (0,)), ((), ())), preferred_element_type=F32)
    else:
        mix_attn = jnp.dot(attn_ref[...], w_attn, preferred_element_type=F32)
    mix = jnp.dot(conv_ref[...], wout_ref[0:CONV_DIM, :], preferred_element_type=F32) + mix_attn
    x1 = x_ref[...] + gt1_ref[...] * mix
    h2 = _rmsnorm_mod(x1, g2_ref[...], sc2_ref[...], sh2_ref[...]).astype(BF16)

    row = lax.broadcasted_iota(I32, (SUBLANES, 1), 0)

    def causal3(u, col0, width):
        w0 = fw_ref[0:1, col0:col0 + width]
        w1 = fw_ref[1:2, col0:col0 + width]
        w2 = fw_ref[2:3, col0:col0 + width]
        outs = []
        for s in range(nseg):
            us = u[s * seg:(s + 1) * seg, :]
            if nseg == 1:
                h0 = carry_ref[0:1, col0:col0 + width]
                h1 = carry_ref[1:2, col0:col0 + width]
            else:
                h0 = hist_ref[s, 0:1, col0:col0 + width]
                h1 = hist_ref[s, 1:2, col0:col0 + width]
            p1 = pltpu.roll(us, 1, axis=0)
            p2 = pltpu.roll(us, 2, axis=0)
            p1 = jnp.concatenate([jnp.where(row == 0, h1, p1[0:SUBLANES, :]), p1[SUBLANES:, :]], axis=0)
            p2 = jnp.concatenate([jnp.where(row == 0, h0, jnp.where(row == 1, h1, p2[0:SUBLANES, :])),
                                  p2[SUBLANES:, :]], axis=0)
            outs.append(us * w2 + p1 * w1 + p2 * w0 + fb_ref[:, col0:col0 + width])
            if nseg == 1:
                carry_ref[:, col0:col0 + width] = us[seg - 2:seg, :]
            else:
                newffn_ref[s, :, col0:col0 + width] = us[seg - 2:seg, :]
        return outs[0] if nseg == 1 else jnp.concatenate(outs, axis=0)

    acc = jnp.zeros((tm, D_MODEL), F32)
    for c0, cw in FF_CHUNKS:
        ua = jnp.dot(h2, wup_ref[:, c0:c0 + cw], preferred_element_type=F32)
        ug = jnp.dot(h2, wup_ref[:, D_FF + c0:D_FF + c0 + cw], preferred_element_type=F32)
        a = causal3(ua, c0, cw)
        g = causal3(ug, D_FF + c0, cw)
        acc = acc + jnp.dot((a * _silu(g)).astype(BF16), wdn_ref[c0:c0 + cw, :], preferred_element_type=F32)

    if nseg == 1:
        @pl.when(i % tiles_per_batch == tiles_per_batch - 1)
        def _():
            newffn_ref[0] = carry_ref[...]

    x2 = x1 + gt2_ref[...] * acc
    ms = jnp.mean(x2 * x2, axis=-1, keepdims=True)
    y_ref[...] = x2 * lax.rsqrt(ms + EPS) * gf_ref[...]


def _post_call(x2d, conv2d, attn, gt1, sc2, sh2, gt2, g2, gf, wout, wup, wdn, fw, fb, hist, tm, t_len, name):
    r, d = x2d.shape
    nt = r // tm
    nb = hist.shape[0]
    nseg = max(1, tm // t_len)
    tpb = max(1, t_len // tm)
    assert nt * nseg == nb * tpb and t_len >= FFN_CONV_WIDTH - 1
    if gt1.ndim == 3:
        mod_spec = pl.BlockSpec((None, 1, d), lambda i: (i // tpb, 0, 0))
    else:
        mod_spec = pl.BlockSpec((tm, d), lambda i: (i, 0))

    def rows(width):
        return pl.BlockSpec((tm, width), lambda i: (i, 0))

    def const(shape):
        return pl.BlockSpec(shape, lambda i: (0,) * len(shape), pipeline_mode=pl.Buffered(1))

    state_spec = pl.BlockSpec((nseg, FFN_CONV_WIDTH - 1, 2 * D_FF), lambda i: (i * nseg // tpb, 0, 0))
    attn_t = attn.ndim == 3
    if attn_t:
        assert nseg == 1
        attn_spec = pl.BlockSpec((None, ATTN_DIM, tm), lambda i: (i // tpb, 0, i % tpb))
    else:
        attn_spec = rows(ATTN_DIM)
    kern = functools.partial(_post_kernel, tm=tm, t_len=t_len, tiles_per_batch=tpb, attn_transposed=attn_t)
    scratch = [pltpu.VMEM((FFN_CONV_WIDTH - 1, 2 * D_FF), F32)] if nseg == 1 else []
    return pl.pallas_call(
        kern,
        grid=(nt,),
        in_specs=[rows(d), rows(CONV_DIM), attn_spec, mod_spec, mod_spec, mod_spec, mod_spec,
                  const((1, d)), const((1, d)),
                  const((CONV_DIM + ATTN_DIM, d)), const((d, 2 * D_FF)), const((D_FF, d)),
                  const((FFN_CONV_WIDTH, 2 * D_FF)), const((1, 2 * D_FF)),
                  state_spec],
        out_specs=(rows(d), state_spec),
        out_shape=(jax.ShapeDtypeStruct((r, d), F32),
                   jax.ShapeDtypeStruct((nb, FFN_CONV_WIDTH - 1, 2 * D_FF), F32)),
        scratch_shapes=scratch,
        compiler_params=_cparams(("arbitrary",)),
        name=name,
    )(x2d, conv2d, attn, gt1, sc2, sh2, gt2, g2, gf, wout, wup, wdn, fw, fb, hist)


def _rope_tables(pos):
    half = ROT_DIM // 2
    inv = 1.0 / (ROPE_THETA ** (jnp.arange(0, ROT_DIM, 2, dtype=F32) / ROT_DIM))
    ang = pos.astype(F32)[:, None] * inv[None, :]
    cos, sin = jnp.cos(ang), jnp.sin(ang)
    t = pos.shape[0]
    rest1 = jnp.ones((t, HEAD_DIM - ROT_DIM), F32)
    rest0 = jnp.zeros((t, HEAD_DIM - ROT_DIM), F32)
    z = jnp.zeros((t, half), F32)
    c64 = jnp.concatenate([cos, cos, rest1], axis=1)
    a64 = jnp.concatenate([z, sin, rest0], axis=1)
    b64 = jnp.concatenate([-sin, z, rest0], axis=1)
    return tuple(jnp.concatenate([m, m], axis=1) for m in (c64, a64, b64))


def _pad_w_in(w_in):
    d = w_in.shape[0]
    return jnp.concatenate(
        [w_in[:, :COL_KI + IDX_DIM], jnp.zeros((d, LANES - IDX_DIM), F32),
         w_in[:, COL_KI + IDX_DIM:], jnp.zeros((d, LANES - N_IDX_HEADS), F32)], axis=1).astype(BF16)


def _layer_group(x, mods, pos, conv_hist, ffn_hist, past, w, *, per_row_mod, tm_in, tt_conv, tq, tm_post, tag, nb_attn=1):
    (g1, w_in_p, dw_w, dw_b, ln_g, ln_b, w_out, g2, w_up, fdw_w, fdw_b, w_down, gf) = w
    b, t, d = x.shape
    r = b * t
    x2d = x.reshape(r, d)
    if per_row_mod:
        sh1, sc1, gt1, sh2, sc2, gt2 = [jnp.repeat(m, t, axis=0) for m in mods]
        tabs = tuple(jnp.tile(m, (b, 1)) for m in _rope_tables(pos))
    else:
        sh1, sc1, gt1, sh2, sc2, gt2 = [m[:, None, :] for m in mods]
        tabs = _rope_tables(pos)

    outs = _inproj_call(x2d, sc1, sh1, g1, w_in_p, tabs, tm_in, "inproj_" + tag)
    glu, q, qi = outs[:3]
    glu3 = glu.reshape(b, t, CONV_DIM)
    q3, qi3 = q.reshape(b, t, ATTN_DIM), qi.reshape(b, t, -1)

    conv_out = _conv_call(glu3, conv_hist, dw_w, dw_b, ln_g, ln_b, tt_conv, "conv_" + tag)
    new_conv = glu3[:, t - CONV_HIST:, :]

    if past is None:
        k_bf, ki_bf, kt, vt, vt_bf, kit, wit = outs[3:]
        k_sel = min(TOPK_MAX, t // 4)
        pieces = [_attn_t_call(q3, qi3, wit, k_bf.reshape(b, t, KV_DIM), vt_bf, ki_bf.reshape(b, t, IDX_DIM),
                               tq, i, 1, (i + 1) * tq, t, 0, k_sel, "attn_%s%d" % (tag, i)) for i in range(t // tq)]
        attn_out = jnp.concatenate(pieces, axis=2)
        new_k = jnp.transpose(kt.reshape(b, N_KV_HEADS, HEAD_DIM, t), (0, 3, 1, 2))
        new_v = jnp.transpose(vt.reshape(b, N_KV_HEADS, HEAD_DIM, t), (0, 3, 1, 2))
        new_ki = jnp.transpose(kit, (0, 2, 1))
    else:
        k, v, ki, wi = outs[3:]
        ck, cv, cki = past
        k3, v3, ki3 = k.reshape(b, t, KV_DIM), v.reshape(b, t, KV_DIM), ki.reshape(b, t, IDX_DIM)
        k_sel = min(TOPK_MAX, (ck.shape[1] + t) // 4)
        attn_out = _attn_call(q3, qi3, wi.reshape(b, t, -1), k3, v3, ki3,
                              jnp.transpose(ck, (0, 2, 3, 1)), jnp.transpose(cv, (0, 2, 3, 1)),
                              jnp.transpose(cki, (0, 2, 1)), nb_attn, k_sel, "attn_" + tag).reshape(r, ATTN_DIM)
        new_k, new_v, new_ki = k3.reshape(b, t, N_KV_HEADS, HEAD_DIM), v3.reshape(b, t, N_KV_HEADS, HEAD_DIM), ki3

    y, new_ffn = _post_call(x2d, conv_out.reshape(r, CONV_DIM), attn_out,
                            gt1, sc2, sh2, gt2, g2, gf, w_out, w_up, w_down, fdw_w, fdw_b, ffn_hist,
                            tm_post, t, "post_" + tag)
    return (y.reshape(b, t, d), new_k, new_v, new_ki, new_conv, new_ffn)


def kernel(x_prompt, x_sample, cache_k, cache_v, cache_kidx, state_conv, state_ffn_conv, c_prompt, c_sample,
           w_ada, b_ada, norm1_g, w_in, conv_dw_w, conv_dw_b, conv_ln_g, conv_ln_b, w_out, norm2_g,
           w_up, ffn_dw_w, ffn_dw_b, w_down, final_norm_g):
    depth = w_ada.shape[0]
    assert depth == 1, "the final norm is fused into the single layer's last kernel"
    bp, sp, d = x_prompt.shape
    bs, ts, _ = x_sample.shape
    past_len = cache_k.shape[2]
    l = 0
    mod = _mod_call(jnp.concatenate([c_prompt, c_sample], axis=0), w_ada[l], b_ada[l])
    mods_p = jnp.split(mod[:bp], 6, axis=-1)
    mods_s = jnp.split(mod[bp:], 6, axis=-1)
    w = (norm1_g[l].reshape(1, d), _pad_w_in(w_in[l]), conv_dw_w[l], conv_dw_b[l], conv_ln_g[l], conv_ln_b[l],
         w_out[l].astype(BF16), norm2_g[l].reshape(1, d), w_up[l].astype(BF16), ffn_dw_w[l],
         ffn_dw_b[l].reshape(1, -1), w_down[l].astype(BF16), final_norm_g.reshape(1, d))

    conv0 = jnp.zeros((bp, CONV_HIST, CONV_DIM), F32)
    ffn0 = jnp.zeros((bp, FFN_CONV_WIDTH - 1, 2 * D_FF), F32)
    out_p = _layer_group(x_prompt, mods_p, jnp.arange(sp, dtype=I32), conv0, ffn0, None, w,
                         per_row_mod=False, tm_in=1024, tt_conv=128, tq=256, tm_post=512, tag="p")
    out_s = _layer_group(x_sample, mods_s, past_len + jnp.arange(ts, dtype=I32), state_conv[l], state_ffn_conv[l],
                         (cache_k[l], cache_v[l], cache_kidx[l]), w,
                         per_row_mod=True, tm_in=bs * ts, tt_conv=ts, tq=ts, tm_post=bs * ts, tag="s", nb_attn=2)
    y_p, k_p, v_p, ki_p, conv_p, ffn_p = out_p
    y_s, k_s, v_s, ki_s, conv_s, ffn_s = out_s
    st = lambda a: a[None]
    return (y_p, y_s, st(k_p), st(v_p), st(ki_p), st(conv_p), st(ffn_p),
            st(k_s), st(v_s), st(ki_s), st(conv_s), st(ffn_s))
```

```python
import functools

import jax
import jax.numpy as jnp
from jax import lax
from jax.experimental import pallas as pl
from jax.experimental.pallas import tpu as pltpu

F32 = jnp.float32
BF16 = jnp.bfloat16
I32 = jnp.int32
I16 = jnp.int16

D_MODEL = 1024
CHUNK = 64
CONV_DIM = 512
CONV_WIDTH = 31
N_HEADS = 8
HEAD_DIM = 64
N_KV_HEADS = 2
ATTN_DIM = N_HEADS * HEAD_DIM
KV_DIM = N_KV_HEADS * HEAD_DIM
ROT_DIM = HEAD_DIM // 4
ROPE_THETA = 500000.0
N_IDX_HEADS = 8
IDX_DIM = 32
TOPK_MAX = 256
D_FF = 2816
FFN_CONV_WIDTH = 3
EPS = 1e-6

LANES = 128
SUBLANES = 8
VMEM_LIMIT = 52 * 1024 * 1024

COL_U, COL_UG, COL_Q, COL_K, COL_V, COL_QI = 0, 512, 1024, 1536, 1664, 1792
COL_KI = 2048
COL_WI = COL_KI + LANES
IN_PAD = COL_WI + LANES

Q_SCALE = HEAD_DIM ** -0.5 * 1.4426950408889634

INT_MIN = -2147483648
KEY_NEG_INF = -2139095040
IDX_BITS = 13


def _order_key(x):
    bits = pltpu.bitcast(x, I32)
    return jnp.where(bits < 0, INT_MIN - bits, bits)


def _cparams(sem):
    return pltpu.CompilerParams(dimension_semantics=sem, vmem_limit_bytes=VMEM_LIMIT)


def _silu(x):
    return x * jax.nn.sigmoid(x)


def _mod_kernel(c_ref, w_ref, b_ref, o_ref):
    s = _silu(c_ref[...]).astype(BF16)
    o_ref[...] = jnp.dot(s, w_ref[...].astype(BF16), preferred_element_type=F32) + b_ref[...]


def _mod_call(c_all, w_ada, b_ada):
    nb, d = c_all.shape
    n = w_ada.shape[1]
    tn = 512
    return pl.pallas_call(
        _mod_kernel,
        grid=(n // tn,),
        in_specs=[pl.BlockSpec((nb, d), lambda j: (0, 0)),
                  pl.BlockSpec((d, tn), lambda j: (0, j)),
                  pl.BlockSpec((1, tn), lambda j: (0, j))],
        out_specs=pl.BlockSpec((nb, tn), lambda j: (0, j)),
        out_shape=jax.ShapeDtypeStruct((nb, n), F32),
        compiler_params=_cparams(("arbitrary",)),
        name="mod",
    )(c_all, w_ada, b_ada.reshape(1, n))


def _rmsnorm_mod(x, g, sc, sh):
    ms = jnp.mean(x * x, axis=-1, keepdims=True)
    return (x * lax.rsqrt(ms + EPS) * g) * (1.0 + sc) + sh


def _inproj_kernel(x_ref, sc_ref, sh_ref, g_ref, w_ref, cos_ref, sa_ref, sb_ref, glu_ref, q_ref, qi_ref, *kv_refs,
                   kv_transposed):
    h = _rmsnorm_mod(x_ref[...], g_ref[...], sc_ref[...], sh_ref[...])
    z = jnp.dot(h.astype(BF16), w_ref[...], preferred_element_type=F32)
    glu_ref[...] = z[:, COL_U:COL_U + CONV_DIM] * jax.nn.sigmoid(z[:, COL_UG:COL_UG + CONV_DIM])
    cos, sa, sb = cos_ref[...], sa_ref[...], sb_ref[...]

    def rope(xs):
        return (xs * cos + pltpu.roll(xs, ROT_DIM // 2, axis=1) * sa
                + pltpu.roll(xs, LANES - ROT_DIM // 2, axis=1) * sb)

    for j in range(ATTN_DIM // LANES):
        c0 = COL_Q + LANES * j
        q_ref[:, LANES * j:LANES * (j + 1)] = (rope(z[:, c0:c0 + LANES]) * Q_SCALE).astype(BF16)
    qi_ref[...] = z[:, COL_QI:COL_QI + N_IDX_HEADS * IDX_DIM].astype(BF16)
    k = rope(z[:, COL_K:COL_K + KV_DIM])
    v = z[:, COL_V:COL_V + KV_DIM]
    ki_slab = z[:, COL_KI:COL_KI + LANES]
    wi_slab = z[:, COL_WI:COL_WI + LANES]
    if kv_transposed:
        kbf_ref, kibf_ref, kt_ref, vt_ref, vtbf_ref, kit_ref, wit_ref = kv_refs
        kbf_ref[...] = k.astype(BF16)
        kibf_ref[...] = ki_slab[:, 0:IDX_DIM].astype(BF16)
        kt_ref[...] = k.T
        vt = v.T
        vt_ref[...] = vt
        vtbf_ref[...] = vt.astype(BF16)
        kit_ref[...] = ki_slab.T[0:IDX_DIM, :]
        wit_ref[...] = wi_slab.T[0:N_IDX_HEADS, :]
    else:
        k_ref, v_ref, ki_ref, wi_ref = kv_refs
        k_ref[...] = k
        v_ref[...] = v
        ki_ref[...] = ki_slab[:, 0:IDX_DIM]
        wi_ref[...] = wi_slab[:, 0:N_IDX_HEADS]


def _inproj_call(x2, sc, sh, g1, w_in_p, tabs, tm, name):
    r, d = x2.shape
    nt = r // tm
    kv_t = sc.ndim == 3
    if kv_t:
        nb = sc.shape[0]
        tpb = nt // nb
        t_len = r // nb
        mod_spec = pl.BlockSpec((None, 1, d), lambda i: (i // tpb, 0, 0))
    else:
        mod_spec = pl.BlockSpec((tm, d), lambda i: (i, 0))
    ntab = tabs[0].shape[0] // tm
    tab_spec = pl.BlockSpec((tm, LANES), lambda i: (i % ntab, 0))

    def rows(width):
        return pl.BlockSpec((tm, width), lambda i: (i, 0))

    def rows_shape(width, dtype):
        return jax.ShapeDtypeStruct((r, width), dtype)

    def cols(width):
        return pl.BlockSpec((None, width, tm), lambda i: (i // tpb, 0, i % tpb))

    def cols_shape(width, dtype):
        return jax.ShapeDtypeStruct((nb, width, t_len), dtype)

    out_specs = [rows(CONV_DIM), rows(ATTN_DIM), rows(N_IDX_HEADS * IDX_DIM)]
    out_shapes = [rows_shape(CONV_DIM, F32), rows_shape(ATTN_DIM, BF16), rows_shape(N_IDX_HEADS * IDX_DIM, BF16)]
    if kv_t:
        out_specs += [rows(KV_DIM), rows(IDX_DIM), cols(KV_DIM), cols(KV_DIM), cols(KV_DIM), cols(IDX_DIM),
                      cols(N_IDX_HEADS)]
        out_shapes += [rows_shape(KV_DIM, BF16), rows_shape(IDX_DIM, BF16), cols_shape(KV_DIM, F32),
                       cols_shape(KV_DIM, F32), cols_shape(KV_DIM, BF16), cols_shape(IDX_DIM, F32),
                       cols_shape(N_IDX_HEADS, F32)]
    else:
        out_specs += [rows(KV_DIM), rows(KV_DIM), rows(IDX_DIM), rows(N_IDX_HEADS)]
        out_shapes += [rows_shape(KV_DIM, F32), rows_shape(KV_DIM, F32), rows_shape(IDX_DIM, F32),
                       rows_shape(N_IDX_HEADS, F32)]
    return pl.pallas_call(
        functools.partial(_inproj_kernel, kv_transposed=kv_t),
        grid=(nt,),
        in_specs=[rows(d), mod_spec, mod_spec,
                  pl.BlockSpec((1, d), lambda i: (0, 0)),
                  pl.BlockSpec((d, IN_PAD), lambda i: (0, 0)),
                  tab_spec, tab_spec, tab_spec],
        out_specs=tuple(out_specs),
        out_shape=tuple(out_shapes),
        compiler_params=_cparams(("arbitrary",)),
        name=name,
    )(x2, sc, sh, g1, w_in_p, *tabs)


CONV_HIST = CONV_WIDTH - 1
CONV_PAD = 32
CONV_RB = 64


def _conv_kernel(glu_ref, hist_ref, w_ref, b_ref, lg_ref, lb_ref, o_ref, ext_ref, y_ref, *, tt):
    t = pl.program_id(1)

    @pl.when(t == 0)
    def _():
        ext_ref[0:CONV_PAD - CONV_HIST, :] = jnp.zeros((CONV_PAD - CONV_HIST, CONV_DIM), F32)
        ext_ref[CONV_PAD - CONV_HIST:CONV_PAD, :] = hist_ref[...]

    @pl.when(t > 0)
    def _():
        ext_ref[0:CONV_PAD, :] = ext_ref[tt:tt + CONV_PAD, :]

    ext_ref[CONV_PAD:CONV_PAD + tt, :] = glu_ref[...]

    off = CONV_PAD - CONV_HIST
    rb = min(CONV_RB, tt)
    for r0 in range(0, tt, rb):
        for c0 in range(0, CONV_DIM, LANES):
            acc = None
            for b in range(SUBLANES):
                span = (CONV_WIDTH - 1 - b) // SUBLANES * SUBLANES
                mis = (off + b) % SUBLANES
                base = off + b - mis + r0
                cover = -(-(mis + rb + span) // SUBLANES) * SUBLANES
                win = ext_ref[base:base + cover, c0:c0 + LANES]
                if mis:
                    win = pltpu.roll(win, cover - mis, axis=0)
                for j in range(b, CONV_WIDTH, SUBLANES):
                    term = win[j - b:j - b + rb, :] * w_ref[j:j + 1, c0:c0 + LANES]
                    acc = term if acc is None else acc + term
            y_ref[r0:r0 + rb, c0:c0 + LANES] = acc + b_ref[:, c0:c0 + LANES]

    y = y_ref[...]
    mu = jnp.mean(y, axis=-1, keepdims=True)
    yc = y - mu
    var = jnp.mean(yc * yc, axis=-1, keepdims=True)
    o_ref[...] = _silu(yc * lax.rsqrt(var + EPS) * lg_ref[...] + lb_ref[...]).astype(BF16)


def _conv_call(glu3, hist, dw_w, dw_b, ln_g, ln_b, tt, name):
    b, t, c = glu3.shape
    vec = pl.BlockSpec((1, c), lambda i, j: (0, 0))
    return pl.pallas_call(
        functools.partial(_conv_kernel, tt=tt),
        grid=(b, t // tt),
        in_specs=[pl.BlockSpec((None, tt, c), lambda i, j: (i, j, 0)),
                  pl.BlockSpec((None, CONV_HIST, c), lambda i, j: (i, 0, 0)),
                  pl.BlockSpec((CONV_WIDTH, c), lambda i, j: (0, 0)),
                  vec, vec, vec],
        out_specs=pl.BlockSpec((None, tt, c), lambda i, j: (i, j, 0)),
        out_shape=jax.ShapeDtypeStruct((b, t, c), BF16),
        scratch_shapes=[pltpu.VMEM((tt + CONV_PAD, c), F32), pltpu.VMEM((tt, c), F32)],
        compiler_params=_cparams(("arbitrary", "arbitrary")),
        name=name,
    )(glu3, hist, dw_w, dw_b.reshape(1, c), ln_g.reshape(1, c), ln_b.reshape(1, c))


CNT_ROWS = 64


def _pad_rows(x, n):
    return jnp.concatenate([x, jnp.zeros((n - x.shape[0], x.shape[1]), x.dtype)], axis=0)


def _attn_kernel(q_ref, qi_ref, wi_ref, kn_ref, vn_ref, kin_ref, kct_ref, vct_ref, kict_ref, o_ref, key_ref, bias_ref,
                 *, nb, tq, past, k_sel):
    rows = nb * tq
    n_keys = past + LANES
    nt = (((1,), (1,)), ((), ()))
    col = lax.broadcasted_iota(I32, (rows, n_keys), 1)
    adm = col < past + tq
    new_ok = lax.broadcasted_iota(I32, (tq, LANES), 1) < tq

    for bb in range(nb):
        kic = kict_ref[bb].astype(BF16)
        kin = _pad_rows(kin_ref[bb].astype(BF16), LANES)
        wi = wi_ref[bb]
        sc_c = jnp.zeros((tq, past), F32)
        sc_n = jnp.zeros((tq, LANES), F32)
        for h in range(N_IDX_HEADS):
            qih = qi_ref[bb, :, IDX_DIM * h:IDX_DIM * (h + 1)]
            sc_c = sc_c + wi[:, h:h + 1] * jnp.maximum(jnp.dot(qih, kic, preferred_element_type=F32), 0.0)
            sc_n = sc_n + wi[:, h:h + 1] * jnp.maximum(
                lax.dot_general(qih, kin, nt, preferred_element_type=F32), 0.0)
        key_ref[bb * tq:(bb + 1) * tq, 0:past] = _order_key(sc_c)
        key_ref[bb * tq:(bb + 1) * tq, past:n_keys] = _order_key(jnp.where(new_ok, sc_n, -jnp.inf))

    rg = min(CNT_ROWS, rows)

    def count_ge(cands):
        outs = [[] for _ in cands]
        for r0 in range(0, rows, rg):
            cs = [jnp.broadcast_to(c[r0:r0 + rg, :], (rg, LANES)) for c in cands]
            accs = [jnp.zeros((rg, LANES), F32) for _ in cands]
            for c0 in range(0, n_keys, LANES):
                kv = key_ref[r0:r0 + rg, c0:c0 + LANES]
                accs = [a + jnp.where(kv >= c, 1.0, 0.0) for a, c in zip(accs, cs)]
            for o, a in zip(outs, accs):
                o.append(jnp.sum(a, axis=1, keepdims=True))
        return [o[0] if len(o) == 1 else jnp.concatenate(o, axis=0) for o in outs]

    bpp = 2 if rows <= 64 else 1

    def bits_step(it, r):
        sh = 32 - bpp * (it + 1)
        step = jnp.left_shift(jnp.int32(1), sh)
        counts = count_ge([r + m * step for m in range(1, 1 << bpp)])
        inc = sum(jnp.where(n >= k_sel, 1, 0) for n in counts)
        return r + jnp.left_shift(inc, sh)

    thr = lax.fori_loop(0, 32 // bpp, bits_step, jnp.full((rows, 1), INT_MIN, I32))

    keys = key_ref[...]
    ge = keys >= thr
    cnt_ge = jnp.sum(jnp.where(ge, 1.0, 0.0), axis=1, keepdims=True)
    bias_ref[...] = jnp.where(ge & adm, 0.0, -jnp.inf)
    split_tie = (cnt_ge > k_sel) & (thr != KEY_NEG_INF)
    any_split = jnp.max(jnp.where(split_tie, 1.0, 0.0)) > 0.0

    @pl.when(any_split)
    def _():
        gt = keys > thr
        eq = keys == thr
        need = k_sel - jnp.sum(jnp.where(gt, 1.0, 0.0), axis=1, keepdims=True)

        def idx_step(it, last):
            cand = last + jnp.left_shift(jnp.int32(1), IDX_BITS - 1 - it)
            below = jnp.sum(jnp.where(eq & (col < cand), 1.0, 0.0), axis=1, keepdims=True)
            return jnp.where(below < need, cand, last)

        last = lax.fori_loop(0, IDX_BITS, idx_step, jnp.zeros((rows, 1), I32))
        sel = (gt | (eq & (col <= last))) & adm
        bias_ref[...] = jnp.where(sel, 0.0, -jnp.inf)

    rep = N_HEADS // N_KV_HEADS
    for bb in range(nb):
        bias = bias_ref[bb * tq:(bb + 1) * tq, :]
        for g in range(N_KV_HEADS):
            qg = jnp.concatenate([q_ref[bb, :, HEAD_DIM * (rep * g + r):HEAD_DIM * (rep * g + r + 1)]
                                  for r in range(rep)], axis=0)
            kn = _pad_rows(kn_ref[bb, :, HEAD_DIM * g:HEAD_DIM * (g + 1)].astype(BF16), LANES)
            vn = _pad_rows(vn_ref[bb, :, HEAD_DIM * g:HEAD_DIM * (g + 1)].astype(BF16), LANES)
            logits = jnp.concatenate(
                [jnp.dot(qg, kct_ref[bb, g].astype(BF16), preferred_element_type=F32),
                 lax.dot_general(qg, kn, nt, preferred_element_type=F32)], axis=1)
            ps, ss = [], []
            for r in range(rep):
                lg = logits[r * tq:(r + 1) * tq, :] + bias
                p = jnp.exp2(lg - jnp.max(lg, axis=1, keepdims=True))
                ss.append(jnp.sum(p, axis=1, keepdims=True))
                ps.append(p.astype(BF16))
            p_all = jnp.concatenate(ps, axis=0)
            o = (lax.dot_general(p_all[:, 0:past], vct_ref[bb, g].astype(BF16), nt, preferred_element_type=F32)
                 + jnp.dot(p_all[:, past:n_keys], vn, preferred_element_type=F32))
            for r in range(rep):
                h = rep * g + r
                o_ref[bb, :, HEAD_DIM * h:HEAD_DIM * (h + 1)] = (o[r * tq:(r + 1) * tq, :] / ss[r]).astype(BF16)


def _attn_call(q3, qi3, wi3, k_new, v_new, ki_new, kct, vct, kict, nb, k_sel, name):
    b, tq, _ = q3.shape
    past = kct.shape[-1]
    n_keys = past + LANES
    assert past % LANES == 0 and tq <= LANES and n_keys <= (1 << IDX_BITS) and b % nb == 0

    def rows(width):
        return pl.BlockSpec((nb, tq, width), lambda i: (i, 0, 0))

    kern = functools.partial(_attn_kernel, nb=nb, tq=tq, past=past, k_sel=float(k_sel))
    return pl.pallas_call(
        kern,
        grid=(b // nb,),
        in_specs=[rows(ATTN_DIM), rows(N_IDX_HEADS * IDX_DIM), rows(N_IDX_HEADS),
                  rows(KV_DIM), rows(KV_DIM), rows(IDX_DIM),
                  pl.BlockSpec((nb, N_KV_HEADS, HEAD_DIM, past), lambda i: (i, 0, 0, 0)),
                  pl.BlockSpec((nb, N_KV_HEADS, HEAD_DIM, past), lambda i: (i, 0, 0, 0)),
                  pl.BlockSpec((nb, IDX_DIM, past), lambda i: (i, 0, 0))],
        out_specs=rows(ATTN_DIM),
        out_shape=jax.ShapeDtypeStruct((b, tq, ATTN_DIM), BF16),
        scratch_shapes=[pltpu.VMEM((nb * tq, n_keys), I32), pltpu.VMEM((nb * tq, n_keys), F32)],
        compiler_params=_cparams(("arbitrary",)),
        name=name,
    )(q3, qi3, wi3, k_new, v_new, ki_new, kct, vct, kict)


VT_ROWS = HEAD_DIM + 16
SCORE_KEYS = 256
CNT_KEYS = 64
SUB_KEYS = 32


def _attn_t_kernel(q_ref, qi_ref, wit_ref, k_ref, vt_ref, ki_ref, o_ref, key_ref, bias_ref, hi_ref, lo_ref,
                   d_ref, lg_ref, p_ref, *, tq, n_keys, q0, t_new, past, k_sel):
    j = pl.program_id(1)
    qpos = lax.broadcasted_iota(I32, (1, tq), 1) + (q0 + j * tq)
    limit = past + jnp.minimum((qpos // CHUNK + 1) * CHUNK, t_new)
    nt = (((1,), (1,)), ((), ()))
    sb = SUB_KEYS
    kpos0 = lax.broadcasted_iota(I32, (sb, tq), 0)

    qi_stack = jnp.concatenate([qi_ref[:, IDX_DIM * h:IDX_DIM * (h + 1)] for h in range(N_IDX_HEADS)], axis=0)
    wit = wit_ref[...]
    kc = min(SCORE_KEYS, n_keys)
    for c0 in range(0, n_keys, kc):
        d_ref[...] = lax.dot_general(ki_ref[c0:c0 + kc, :], qi_stack, nt, preferred_element_type=F32)
        for s0 in range(0, kc, sb):
            sc = wit[0:1, :] * jnp.maximum(d_ref[s0:s0 + sb, 0:tq], 0.0)
            for h in range(1, N_IDX_HEADS):
                sc = sc + wit[h:h + 1, :] * jnp.maximum(d_ref[s0:s0 + sb, h * tq:(h + 1) * tq], 0.0)
            keys = _order_key(jnp.where(kpos0 + (c0 + s0) < limit, sc, -jnp.inf))
            key_ref[c0 + s0:c0 + s0 + sb, :] = keys
            hi_ref[c0 + s0:c0 + s0 + sb, :] = (keys >> 16).astype(I16)
            lo_ref[c0 + s0:c0 + s0 + sb, :] = (keys ^ 0x8000).astype(I16)

    sl = min(CNT_KEYS, n_keys)

    def search16(ref):
        def bit_step(it, r):
            cand = r + jnp.left_shift(jnp.int32(1), 15 - it)
            cb = jnp.broadcast_to((cand - 32768).astype(I16), (sl, tq))
            acc = jnp.zeros((sl, tq), I16)
            for c0 in range(0, n_keys, sl):
                acc = acc + jnp.where(ref[c0:c0 + sl, :] >= cb, I16(1), I16(0))
            cnt = jnp.sum(acc.astype(I32), axis=0, keepdims=True)
            return jnp.where(cnt >= k_sel, cand, r)
        return lax.fori_loop(0, 16, bit_step, jnp.zeros((1, tq), I32))

    if n_keys > k_sel:
        r_hi = search16(hi_ref)
        h16 = jnp.broadcast_to((r_hi - 32768).astype(I16), (sb, tq))
        for s0 in range(0, n_keys, sb):
            hi = hi_ref[s0:s0 + sb, :]
            lo_ref[s0:s0 + sb, :] = jnp.where(hi == h16, lo_ref[s0:s0 + sb, :],
                                              jnp.where(hi > h16, I16(32767), I16(-32768)))
        thr = ((r_hi - 32768) << 16) | search16(lo_ref)
    else:
        thr = jnp.full((1, tq), KEY_NEG_INF, I32)

    cnt = jnp.zeros((sb, tq), F32)
    for s0 in range(0, n_keys, sb):
        ge = key_ref[s0:s0 + sb, :] >= thr
        cnt = cnt + jnp.where(ge, 1.0, 0.0)
        bias_ref[s0:s0 + sb, :] = jnp.where(ge & (kpos0 + s0 < limit), 0.0, -jnp.inf)
    cnt_ge = jnp.sum(cnt, axis=0, keepdims=True)
    split_tie = (cnt_ge > k_sel) & (thr != KEY_NEG_INF)
    any_split = jnp.max(jnp.where(split_tie, 1.0, 0.0)) > 0.0

    @pl.when(any_split)
    def _():
        keys = key_ref[...]
        kpos = lax.broadcasted_iota(I32, (n_keys, tq), 0)
        gt = keys > thr
        eq = keys == thr
        need = k_sel - jnp.sum(jnp.where(gt, 1.0, 0.0), axis=0, keepdims=True)

        def idx_step(it, last):
            cand = last + jnp.left_shift(jnp.int32(1), IDX_BITS - 1 - it)
            below = jnp.sum(jnp.where(eq & (kpos < cand), 1.0, 0.0), axis=0, keepdims=True)
            return jnp.where(below < need, cand, last)

        last = lax.fori_loop(0, IDX_BITS, idx_step, jnp.zeros((1, tq), I32))
        sel = (gt | (eq & (kpos <= last))) & (kpos < limit)
        bias_ref[...] = jnp.where(sel, 0.0, -jnp.inf)

    rep = N_HEADS // N_KV_HEADS
    for g in range(N_KV_HEADS):
        qg = jnp.concatenate([q_ref[:, HEAD_DIM * (rep * g + r):HEAD_DIM * (rep * g + r + 1)] for r in range(rep)],
                             axis=0)
        lg_ref[...] = lax.dot_general(k_ref[:, HEAD_DIM * g:HEAD_DIM * (g + 1)], qg, nt, preferred_element_type=F32)
        for r in range(rep):
            cols = slice(r * tq, (r + 1) * tq)
            mx = jnp.full((sb, tq), -jnp.inf, F32)
            for s0 in range(0, n_keys, sb):
                mx = jnp.maximum(mx, lg_ref[s0:s0 + sb, cols] + bias_ref[s0:s0 + sb, :])
            mb = jnp.broadcast_to(jnp.max(mx, axis=0, keepdims=True), (sb, tq))
            for s0 in range(0, n_keys, sb):
                p_ref[s0:s0 + sb, cols] = jnp.exp2(lg_ref[s0:s0 + sb, cols] + bias_ref[s0:s0 + sb, :] - mb).astype(BF16)
        vt_ext = jnp.concatenate([vt_ref[HEAD_DIM * g:HEAD_DIM * (g + 1), :],
                                  jnp.ones((VT_ROWS - HEAD_DIM, n_keys), BF16)], axis=0)
        o = jnp.dot(vt_ext, p_ref[...], preferred_element_type=F32)
        o = o[0:HEAD_DIM, :] / o[HEAD_DIM:HEAD_DIM + 1, :]
        for r in range(rep):
            h = rep * g + r
            o_ref[HEAD_DIM * h:HEAD_DIM * (h + 1), :] = o[:, r * tq:(r + 1) * tq].astype(BF16)


def _attn_t_call(q3, qi3, wit3, k_all, vt_all, ki_all, tq, q_blk0, n_q_blk, n_keys, t_new, past, k_sel, name):
    b = q3.shape[0]
    assert n_keys % LANES == 0 and n_keys <= (1 << IDX_BITS) and n_keys <= k_all.shape[1] and tq % LANES == 0
    kern = functools.partial(_attn_t_kernel, tq=tq, n_keys=n_keys, q0=q_blk0 * tq, t_new=t_new, past=past,
                             k_sel=float(k_sel))
    return pl.pallas_call(
        kern,
        grid=(b, n_q_blk),
        in_specs=[pl.BlockSpec((None, tq, ATTN_DIM), lambda i, j: (i, q_blk0 + j, 0)),
                  pl.BlockSpec((None, tq, N_IDX_HEADS * IDX_DIM), lambda i, j: (i, q_blk0 + j, 0)),
                  pl.BlockSpec((None, N_IDX_HEADS, tq), lambda i, j: (i, 0, q_blk0 + j)),
                  pl.BlockSpec((None, n_keys, KV_DIM), lambda i, j: (i, 0, 0)),
                  pl.BlockSpec((None, KV_DIM, n_keys), lambda i, j: (i, 0, 0)),
                  pl.BlockSpec((None, n_keys, IDX_DIM), lambda i, j: (i, 0, 0))],
        out_specs=pl.BlockSpec((None, ATTN_DIM, tq), lambda i, j: (i, 0, j)),
        out_shape=jax.ShapeDtypeStruct((b, ATTN_DIM, n_q_blk * tq), BF16),
        scratch_shapes=[pltpu.VMEM((n_keys, tq), I32), pltpu.VMEM((n_keys, tq), F32),
                        pltpu.VMEM((n_keys, tq), I16), pltpu.VMEM((n_keys, tq), I16),
                        pltpu.VMEM((min(SCORE_KEYS, n_keys), N_IDX_HEADS * tq), F32),
                        pltpu.VMEM((n_keys, N_HEADS // N_KV_HEADS * tq), F32),
                        pltpu.VMEM((n_keys, N_HEADS // N_KV_HEADS * tq), BF16)],
        compiler_params=_cparams(("arbitrary", "arbitrary")),
        name=name,
    )(q3, qi3, wit3, k_all, vt_all, ki_all)


FF_CHUNKS = ((0, D_FF),)


def _post_kernel(x_ref, conv_ref, attn_ref, gt1_ref, sc2_ref, sh2_ref, gt2_ref, g2_ref, gf_ref,
                 wout_ref, wup_ref, wdn_ref, fw_ref, fb_ref, hist_ref,
                 y_ref, newffn_ref, *scratch, tm, t_len, tiles_per_batch, attn_transposed):
    nseg = max(1, tm // t_len)
    seg = tm // nseg
    i = pl.program_id(0)
    if nseg == 1:
        carry_ref = scratch[0]

        @pl.when(i % tiles_per_batch == 0)
        def _():
            carry_ref[...] = hist_ref[0]

    w_attn = wout_ref[CONV_DIM:CONV_DIM + ATTN_DIM, :]
    if attn_transposed:
        mix_attn = lax.dot_general(attn_ref[...], w_attn, (((0,), (0,)), ((), ())), preferred_element_type=F32)
    else:
        mix_attn = jnp.dot(attn_ref[...], w_attn, preferred_element_type=F32)
    mix = jnp.dot(conv_ref[...], wout_ref[0:CONV_DIM, :], preferred_element_type=F32) + mix_attn
    x1 = x_ref[...] + gt1_ref[...] * mix
    h2 = _rmsnorm_mod(x1, g2_ref[...], sc2_ref[...], sh2_ref[...]).astype(BF16)

    row = lax.broadcasted_iota(I32, (SUBLANES, 1), 0)

    def causal3(u, col0, width):
        w0 = fw_ref[0:1, col0:col0 + width]
        w1 = fw_ref[1:2, col0:col0 + width]
        w2 = fw_ref[2:3, col0:col0 + width]
        outs = []
        for s in range(nseg):
            us = u[s * seg:(s + 1) * seg, :]
            if nseg == 1:
                h0 = carry_ref[0:1, col0:col0 + width]
                h1 = carry_ref[1:2, col0:col0 + width]
            else:
                h0 = hist_ref[s, 0:1, col0:col0 + width]
                h1 = hist_ref[s, 1:2, col0:col0 + width]
            p1 = pltpu.roll(us, 1, axis=0)
            p2 = pltpu.roll(us, 2, axis=0)
            p1 = jnp.concatenate([jnp.where(row == 0, h1, p1[0:SUBLANES, :]), p1[SUBLANES:, :]], axis=0)
            p2 = jnp.concatenate([jnp.where(row == 0, h0, jnp.where(row == 1, h1, p2[0:SUBLANES, :])),
                                  p2[SUBLANES:, :]], axis=0)
            outs.append(us * w2 + p1 * w1 + p2 * w0 + fb_ref[:, col0:col0 + width])
            if nseg == 1:
                carry_ref[:, col0:col0 + width] = us[seg - 2:seg, :]
            else:
                newffn_ref[s, :, col0:col0 + width] = us[seg - 2:seg, :]
        return outs[0] if nseg == 1 else jnp.concatenate(outs, axis=0)

    acc = jnp.zeros((tm, D_MODEL), F32)
    for c0, cw in FF_CHUNKS:
        ua = jnp.dot(h2, wup_ref[:, c0:c0 + cw], preferred_element_type=F32)
        ug = jnp.dot(h2, wup_ref[:, D_FF + c0:D_FF + c0 + cw], preferred_element_type=F32)
        a = causal3(ua, c0, cw)
        g = causal3(ug, D_FF + c0, cw)
        acc = acc + jnp.dot((a * _silu(g)).astype(BF16), wdn_ref[c0:c0 + cw, :], preferred_element_type=F32)

    if nseg == 1:
        @pl.when(i % tiles_per_batch == tiles_per_batch - 1)
        def _():
            newffn_ref[0] = carry_ref[...]

    x2 = x1 + gt2_ref[...] * acc
    ms = jnp.mean(x2 * x2, axis=-1, keepdims=True)
    y_ref[...] = x2 * lax.rsqrt(ms + EPS) * gf_ref[...]


def _post_call(x2d, conv2d, attn, gt1, sc2, sh2, gt2, g2, gf, wout, wup, wdn, fw, fb, hist, tm, t_len, name):
    r, d = x2d.shape
    nt = r // tm
    nb = hist.shape[0]
    nseg = max(1, tm // t_len)
    tpb = max(1, t_len // tm)
    assert nt * nseg == nb * tpb and t_len >= FFN_CONV_WIDTH - 1
    if gt1.ndim == 3:
        mod_spec = pl.BlockSpec((None, 1, d), lambda i: (i // tpb, 0, 0))
    else:
        mod_spec = pl.BlockSpec((tm, d), lambda i: (i, 0))

    def rows(width):
        return pl.BlockSpec((tm, width), lambda i: (i, 0))

    def const(shape):
        return pl.BlockSpec(shape, lambda i: (0,) * len(shape), pipeline_mode=pl.Buffered(1))

    state_spec = pl.BlockSpec((nseg, FFN_CONV_WIDTH - 1, 2 * D_FF), lambda i: (i * nseg // tpb, 0, 0))
    attn_t = attn.ndim == 3
    if attn_t:
        assert nseg == 1
        attn_spec = pl.BlockSpec((None, ATTN_DIM, tm), lambda i: (i // tpb, 0, i % tpb))
    else:
        attn_spec = rows(ATTN_DIM)
    kern = functools.partial(_post_kernel, tm=tm, t_len=t_len, tiles_per_batch=tpb, attn_transposed=attn_t)
    scratch = [pltpu.VMEM((FFN_CONV_WIDTH - 1, 2 * D_FF), F32)] if nseg == 1 else []
    return pl.pallas_call(
        kern,
        grid=(nt,),
        in_specs=[rows(d), rows(CONV_DIM), attn_spec, mod_spec, mod_spec, mod_spec, mod_spec,
                  const((1, d)), const((1, d)),
                  const((CONV_DIM + ATTN_DIM, d)), const((d, 2 * D_FF)), const((D_FF, d)),
                  const((FFN_CONV_WIDTH, 2 * D_FF)), const((1, 2 * D_FF)),
                  state_spec],
        out_specs=(rows(d), state_spec),
        out_shape=(jax.ShapeDtypeStruct((r, d), F32),
                   jax.ShapeDtypeStruct((nb, FFN_CONV_WIDTH - 1, 2 * D_FF), F32)),
        scratch_shapes=scratch,
        compiler_params=_cparams(("arbitrary",)),
        name=name,
    )(x2d, conv2d, attn, gt1, sc2, sh2, gt2, g2, gf, wout, wup, wdn, fw, fb, hist)


def _rope_tables(pos):
    half = ROT_DIM // 2
    inv = 1.0 / (ROPE_THETA ** (jnp.arange(0, ROT_DIM, 2, dtype=F32) / ROT_DIM))
    ang = pos.astype(F32)[:, None] * inv[None, :]
    cos, sin = jnp.cos(ang), jnp.sin(ang)
    t = pos.shape[0]
    rest1 = jnp.ones((t, HEAD_DIM - ROT_DIM), F32)
    rest0 = jnp.zeros((t, HEAD_DIM - ROT_DIM), F32)
    z = jnp.zeros((t, half), F32)
    c64 = jnp.concatenate([cos, cos, rest1], axis=1)
    a64 = jnp.concatenate([z, sin, rest0], axis=1)
    b64 = jnp.concatenate([-sin, z, rest0], axis=1)
    return tuple(jnp.concatenate([m, m], axis=1) for m in (c64, a64, b64))


def _pad_w_in(w_in):
    d = w_in.shape[0]
    return jnp.concatenate(
        [w_in[:, :COL_KI + IDX_DIM], jnp.zeros((d, LANES - IDX_DIM), F32),
         w_in[:, COL_KI + IDX_DIM:], jnp.zeros((d, LANES - N_IDX_HEADS), F32)], axis=1).astype(BF16)


def _group_tiles(b, t, cached):
    if cached:
        return dict(tm_in=b * t, tt_conv=t, tq=t, tm_post=b * t, nb_attn=4)
    return dict(tm_in=1024, tt_conv=128, tq=256, tm_post=512, nb_attn=1)


def _layer_group(x, mods, pos, conv_hist, ffn_hist, past, w, *, per_row_mod, tm_in, tt_conv, tq, tm_post, tag, nb_attn=1):
    (g1, w_in_p, dw_w, dw_b, ln_g, ln_b, w_out, g2, w_up, fdw_w, fdw_b, w_down, gf) = w
    b, t, d = x.shape
    r = b * t
    x2d = x.reshape(r, d)
    if per_row_mod:
        sh1, sc1, gt1, sh2, sc2, gt2 = [jnp.repeat(m, t, axis=0) for m in mods]
        tabs = tuple(jnp.tile(m, (b, 1)) for m in _rope_tables(pos))
    else:
        sh1, sc1, gt1, sh2, sc2, gt2 = [m[:, None, :] for m in mods]
        tabs = _rope_tables(pos)

    outs = _inproj_call(x2d, sc1, sh1, g1, w_in_p, tabs, tm_in, "inproj_" + tag)
    glu, q, qi = outs[:3]
    glu3 = glu.reshape(b, t, CONV_DIM)
    q3, qi3 = q.reshape(b, t, ATTN_DIM), qi.reshape(b, t, -1)

    conv_out = _conv_call(glu3, conv_hist, dw_w, dw_b, ln_g, ln_b, tt_conv, "conv_" + tag)
    new_conv = glu3[:, t - CONV_HIST:, :]

    if past is None:
        k_bf, ki_bf, kt, vt, vt_bf, kit, wit = outs[3:]
        k_sel = min(TOPK_MAX, t // 4)
        pieces = [_attn_t_call(q3, qi3, wit, k_bf.reshape(b, t, KV_DIM), vt_bf, ki_bf.reshape(b, t, IDX_DIM),
                               tq, i, 1, (i + 1) * tq, t, 0, k_sel, "attn_%s%d" % (tag, i)) for i in range(t // tq)]
        attn_out = jnp.concatenate(pieces, axis=2)
        new_k = jnp.transpose(kt.reshape(b, N_KV_HEADS, HEAD_DIM, t), (0, 3, 1, 2))
        new_v = jnp.transpose(vt.reshape(b, N_KV_HEADS, HEAD_DIM, t), (0, 3, 1, 2))
        new_ki = jnp.transpose(kit, (0, 2, 1))
    else:
        k, v, ki, wi = outs[3:]
        ck, cv, cki = past
        k3, v3, ki3 = k.reshape(b, t, KV_DIM), v.reshape(b, t, KV_DIM), ki.reshape(b, t, IDX_DIM)
        k_sel = min(TOPK_MAX, (ck.shape[1] + t) // 4)
        attn_out = _attn_call(q3, qi3, wi.reshape(b, t, -1), k3, v3, ki3,
                              jnp.transpose(ck, (0, 2, 3, 1)), jnp.transpose(cv, (0, 2, 3, 1)),
                              jnp.transpose(cki, (0, 2, 1)), nb_attn, k_sel, "attn_" + tag).reshape(r, ATTN_DIM)
        new_k, new_v, new_ki = k3.reshape(b, t, N_KV_HEADS, HEAD_DIM), v3.reshape(b, t, N_KV_HEADS, HEAD_DIM), ki3

    y, new_ffn = _post_call(x2d, conv_out.reshape(r, CONV_DIM), attn_out,
                            gt1, sc2, sh2, gt2, g2, gf, w_out, w_up, w_down, fdw_w, fdw_b, ffn_hist,
                            tm_post, t, "post_" + tag)
    return (y.reshape(b, t, d), new_k, new_v, new_ki, new_conv, new_ffn)


def kernel(x_prompt, x_sample, cache_k, cache_v, cache_kidx, state_conv, state_ffn_conv, c_prompt, c_sample,
           w_ada, b_ada, norm1_g, w_in, conv_dw_w, conv_dw_b, conv_ln_g, conv_ln_b, w_out, norm2_g,
           w_up, ffn_dw_w, ffn_dw_b, w_down, final_norm_g):
    depth = w_ada.shape[0]
    assert depth == 1, "the final norm is fused into the single layer's last kernel"
    bp, sp, d = x_prompt.shape
    bs, ts, _ = x_sample.shape
    past_len = cache_k.shape[2]
    l = 0
    mod = _mod_call(jnp.concatenate([c_prompt, c_sample], axis=0), w_ada[l], b_ada[l])
    mods_p = jnp.split(mod[:bp], 6, axis=-1)
    mods_s = jnp.split(mod[bp:], 6, axis=-1)
    w = (norm1_g[l].reshape(1, d), _pad_w_in(w_in[l]), conv_dw_w[l], conv_dw_b[l], conv_ln_g[l], conv_ln_b[l],
         w_out[l].astype(BF16), norm2_g[l].reshape(1, d), w_up[l].astype(BF16), ffn_dw_w[l],
         ffn_dw_b[l].reshape(1, -1), w_down[l].astype(BF16), final_norm_g.reshape(1, d))

    conv0 = jnp.zeros((bp, CONV_HIST, CONV_DIM), F32)
    ffn0 = jnp.zeros((bp, FFN_CONV_WIDTH - 1, 2 * D_FF), F32)
    out_p = _layer_group(x_prompt, mods_p, jnp.arange(sp, dtype=I32), conv0, ffn0, None, w,
                         per_row_mod=False, tag="p", **_group_tiles(bp, sp, cached=False))
    out_s = _layer_group(x_sample, mods_s, past_len + jnp.arange(ts, dtype=I32), state_conv[l], state_ffn_conv[l],
                         (cache_k[l], cache_v[l], cache_kidx[l]), w,
                         per_row_mod=True, tag="s", **_group_tiles(bs, ts, cached=True))
    y_p, k_p, v_p, ki_p, conv_p, ffn_p = out_p
    y_s, k_s, v_s, ki_s, conv_s, ffn_s = out_s
    st = lambda a: a[None]
    return (y_p, y_s, st(k_p), st(v_p), st(ki_p), st(conv_p), st(ffn_p),
            st(k_s), st(v_s), st(ki_s), st(conv_s), st(ffn_s))
```

```python
import functools

import jax
import jax.numpy as jnp
from jax import lax
from jax.experimental import pallas as pl
from jax.experimental.pallas import tpu as pltpu

F32 = jnp.float32
BF16 = jnp.bfloat16
I32 = jnp.int32
I16 = jnp.int16

D_MODEL = 1024
CHUNK = 64
CONV_DIM = 512
CONV_WIDTH = 31
N_HEADS = 8
HEAD_DIM = 64
N_KV_HEADS = 2
ATTN_DIM = N_HEADS * HEAD_DIM
KV_DIM = N_KV_HEADS * HEAD_DIM
ROT_DIM = HEAD_DIM // 4
ROPE_THETA = 500000.0
N_IDX_HEADS = 8
IDX_DIM = 32
TOPK_MAX = 256
D_FF = 2816
FFN_CONV_WIDTH = 3
EPS = 1e-6

LANES = 128
SUBLANES = 8
VMEM_LIMIT = 52 * 1024 * 1024

COL_U, COL_UG, COL_Q, COL_K, COL_V, COL_QI = 0, 512, 1024, 1536, 1664, 1792
COL_KI = 2048
COL_WI = COL_KI + LANES
IN_PAD = COL_WI + LANES

Q_SCALE = HEAD_DIM ** -0.5 * 1.4426950408889634

INT_MIN = -2147483648
KEY_NEG_INF = -2139095040
IDX_BITS = 13


def _order_key(x):
    bits = pltpu.bitcast(x, I32)
    return jnp.where(bits < 0, INT_MIN - bits, bits)


def _cparams(sem):
    return pltpu.CompilerParams(dimension_semantics=sem, vmem_limit_bytes=VMEM_LIMIT)


def _silu(x):
    return x * jax.nn.sigmoid(x)


def _mod_kernel(c_ref, w_ref, b_ref, o_ref):
    s = _silu(c_ref[...]).astype(BF16)
    o_ref[...] = jnp.dot(s, w_ref[...].astype(BF16), preferred_element_type=F32) + b_ref[...]


def _mod_call(c_all, w_ada, b_ada):
    nb, d = c_all.shape
    n = w_ada.shape[1]
    tn = 512
    return pl.pallas_call(
        _mod_kernel,
        grid=(n // tn,),
        in_specs=[pl.BlockSpec((nb, d), lambda j: (0, 0)),
                  pl.BlockSpec((d, tn), lambda j: (0, j)),
                  pl.BlockSpec((1, tn), lambda j: (0, j))],
        out_specs=pl.BlockSpec((nb, tn), lambda j: (0, j)),
        out_shape=jax.ShapeDtypeStruct((nb, n), F32),
        compiler_params=_cparams(("arbitrary",)),
        name="mod",
    )(c_all, w_ada, b_ada.reshape(1, n))


def _rmsnorm_mod(x, g, sc, sh):
    ms = jnp.mean(x * x, axis=-1, keepdims=True)
    return (x * lax.rsqrt(ms + EPS) * g) * (1.0 + sc) + sh


def _inproj_kernel(x_ref, sc_ref, sh_ref, g_ref, w_ref, cos_ref, sa_ref, sb_ref, glu_ref, q_ref, qi_ref, *kv_refs,
                   kv_transposed):
    h = _rmsnorm_mod(x_ref[...], g_ref[...], sc_ref[...], sh_ref[...])
    z = jnp.dot(h.astype(BF16), w_ref[...], preferred_element_type=F32)
    glu_ref[...] = z[:, COL_U:COL_U + CONV_DIM] * jax.nn.sigmoid(z[:, COL_UG:COL_UG + CONV_DIM])
    cos, sa, sb = cos_ref[...], sa_ref[...], sb_ref[...]

    def rope(xs):
        return (xs * cos + pltpu.roll(xs, ROT_DIM // 2, axis=1) * sa
                + pltpu.roll(xs, LANES - ROT_DIM // 2, axis=1) * sb)

    for j in range(ATTN_DIM // LANES):
        c0 = COL_Q + LANES * j
        q_ref[:, LANES * j:LANES * (j + 1)] = (rope(z[:, c0:c0 + LANES]) * Q_SCALE).astype(BF16)
    qi_ref[...] = z[:, COL_QI:COL_QI + N_IDX_HEADS * IDX_DIM].astype(BF16)
    k = rope(z[:, COL_K:COL_K + KV_DIM])
    v = z[:, COL_V:COL_V + KV_DIM]
    ki_slab = z[:, COL_KI:COL_KI + LANES]
    wi_slab = z[:, COL_WI:COL_WI + LANES]
    if kv_transposed:
        kbf_ref, kibf_ref, kt_ref, vt_ref, vtbf_ref, kit_ref, wit_ref = kv_refs
        kbf_ref[...] = k.astype(BF16)
        kibf_ref[...] = ki_slab[:, 0:IDX_DIM].astype(BF16)
        kt_ref[...] = k.T
        vt = v.T
        vt_ref[...] = vt
        vtbf_ref[...] = vt.astype(BF16)
        kit_ref[...] = ki_slab.T[0:IDX_DIM, :]
        wit_ref[...] = wi_slab.T[0:N_IDX_HEADS, :]
    else:
        k_ref, v_ref, ki_ref, wi_ref = kv_refs
        k_ref[...] = k
        v_ref[...] = v
        ki_ref[...] = ki_slab[:, 0:IDX_DIM]
        wi_ref[...] = wi_slab[:, 0:N_IDX_HEADS]


def _inproj_call(x2, sc, sh, g1, w_in_p, tabs, tm, name):
    r, d = x2.shape
    nt = r // tm
    kv_t = sc.ndim == 3
    if kv_t:
        nb = sc.shape[0]
        tpb = nt // nb
        t_len = r // nb
        mod_spec = pl.BlockSpec((None, 1, d), lambda i: (i // tpb, 0, 0))
    else:
        mod_spec = pl.BlockSpec((tm, d), lambda i: (i, 0))
    ntab = tabs[0].shape[0] // tm
    tab_spec = pl.BlockSpec((tm, LANES), lambda i: (i % ntab, 0))

    def rows(width):
        return pl.BlockSpec((tm, width), lambda i: (i, 0))

    def rows_shape(width, dtype):
        return jax.ShapeDtypeStruct((r, width), dtype)

    def cols(width):
        return pl.BlockSpec((None, width, tm), lambda i: (i // tpb, 0, i % tpb))

    def cols_shape(width, dtype):
        return jax.ShapeDtypeStruct((nb, width, t_len), dtype)

    out_specs = [rows(CONV_DIM), rows(ATTN_DIM), rows(N_IDX_HEADS * IDX_DIM)]
    out_shapes = [rows_shape(CONV_DIM, F32), rows_shape(ATTN_DIM, BF16), rows_shape(N_IDX_HEADS * IDX_DIM, BF16)]
    if kv_t:
        out_specs += [rows(KV_DIM), rows(IDX_DIM), cols(KV_DIM), cols(KV_DIM), cols(KV_DIM), cols(IDX_DIM),
                      cols(N_IDX_HEADS)]
        out_shapes += [rows_shape(KV_DIM, BF16), rows_shape(IDX_DIM, BF16), cols_shape(KV_DIM, F32),
                       cols_shape(KV_DIM, F32), cols_shape(KV_DIM, BF16), cols_shape(IDX_DIM, F32),
                       cols_shape(N_IDX_HEADS, F32)]
    else:
        out_specs += [rows(KV_DIM), rows(KV_DIM), rows(IDX_DIM), rows(N_IDX_HEADS)]
        out_shapes += [rows_shape(KV_DIM, F32), rows_shape(KV_DIM, F32), rows_shape(IDX_DIM, F32),
                       rows_shape(N_IDX_HEADS, F32)]
    return pl.pallas_call(
        functools.partial(_inproj_kernel, kv_transposed=kv_t),
        grid=(nt,),
        in_specs=[rows(d), mod_spec, mod_spec,
                  pl.BlockSpec((1, d), lambda i: (0, 0)),
                  pl.BlockSpec((d, IN_PAD), lambda i: (0, 0)),
                  tab_spec, tab_spec, tab_spec],
        out_specs=tuple(out_specs),
        out_shape=tuple(out_shapes),
        compiler_params=_cparams(("arbitrary",)),
        name=name,
    )(x2, sc, sh, g1, w_in_p, *tabs)


CONV_HIST = CONV_WIDTH - 1
CONV_PAD = 32
CONV_RB = 64


def _conv_kernel(glu_ref, hist_ref, w_ref, b_ref, lg_ref, lb_ref, o_ref, ext_ref, y_ref, *, tt):
    t = pl.program_id(1)

    @pl.when(t == 0)
    def _():
        ext_ref[0:CONV_PAD - CONV_HIST, :] = jnp.zeros((CONV_PAD - CONV_HIST, CONV_DIM), F32)
        ext_ref[CONV_PAD - CONV_HIST:CONV_PAD, :] = hist_ref[...]

    @pl.when(t > 0)
    def _():
        ext_ref[0:CONV_PAD, :] = ext_ref[tt:tt + CONV_PAD, :]

    ext_ref[CONV_PAD:CONV_PAD + tt, :] = glu_ref[...]

    off = CONV_PAD - CONV_HIST
    rb = min(CONV_RB, tt)
    for r0 in range(0, tt, rb):
        for c0 in range(0, CONV_DIM, LANES):
            acc = None
            for b in range(SUBLANES):
                span = (CONV_WIDTH - 1 - b) // SUBLANES * SUBLANES
                mis = (off + b) % SUBLANES
                base = off + b - mis + r0
                cover = -(-(mis + rb + span) // SUBLANES) * SUBLANES
                win = ext_ref[base:base + cover, c0:c0 + LANES]
                if mis:
                    win = pltpu.roll(win, cover - mis, axis=0)
                for j in range(b, CONV_WIDTH, SUBLANES):
                    term = win[j - b:j - b + rb, :] * w_ref[j:j + 1, c0:c0 + LANES]
                    acc = term if acc is None else acc + term
            y_ref[r0:r0 + rb, c0:c0 + LANES] = acc + b_ref[:, c0:c0 + LANES]

    y = y_ref[...]
    mu = jnp.mean(y, axis=-1, keepdims=True)
    yc = y - mu
    var = jnp.mean(yc * yc, axis=-1, keepdims=True)
    o_ref[...] = _silu(yc * lax.rsqrt(var + EPS) * lg_ref[...] + lb_ref[...]).astype(BF16)


def _conv_call(glu3, hist, dw_w, dw_b, ln_g, ln_b, tt, name):
    b, t, c = glu3.shape
    vec = pl.BlockSpec((1, c), lambda i, j: (0, 0))
    return pl.pallas_call(
        functools.partial(_conv_kernel, tt=tt),
        grid=(b, t // tt),
        in_specs=[pl.BlockSpec((None, tt, c), lambda i, j: (i, j, 0)),
                  pl.BlockSpec((None, CONV_HIST, c), lambda i, j: (i, 0, 0)),
                  pl.BlockSpec((CONV_WIDTH, c), lambda i, j: (0, 0)),
                  vec, vec, vec],
        out_specs=pl.BlockSpec((None, tt, c), lambda i, j: (i, j, 0)),
        out_shape=jax.ShapeDtypeStruct((b, t, c), BF16),
        scratch_shapes=[pltpu.VMEM((tt + CONV_PAD, c), F32), pltpu.VMEM((tt, c), F32)],
        compiler_params=_cparams(("arbitrary", "arbitrary")),
        name=name,
    )(glu3, hist, dw_w, dw_b.reshape(1, c), ln_g.reshape(1, c), ln_b.reshape(1, c))


CNT_ROWS = 64


def _pad_rows(x, n):
    return jnp.concatenate([x, jnp.zeros((n - x.shape[0], x.shape[1]), x.dtype)], axis=0)


def _attn_kernel(q_ref, qi_ref, wi_ref, kn_ref, vn_ref, kin_ref, kct_ref, vct_ref, kict_ref, o_ref, key_ref, bias_ref,
                 *, nb, tq, past, k_sel):
    rows = nb * tq
    n_keys = past + LANES
    nt = (((1,), (1,)), ((), ()))
    col = lax.broadcasted_iota(I32, (rows, n_keys), 1)
    adm = col < past + tq
    new_ok = lax.broadcasted_iota(I32, (tq, LANES), 1) < tq

    for bb in range(nb):
        kic = kict_ref[bb].astype(BF16)
        kin = _pad_rows(kin_ref[bb].astype(BF16), LANES)
        wi = wi_ref[bb]
        sc_c = jnp.zeros((tq, past), F32)
        sc_n = jnp.zeros((tq, LANES), F32)
        for h in range(N_IDX_HEADS):
            qih = qi_ref[bb, :, IDX_DIM * h:IDX_DIM * (h + 1)]
            sc_c = sc_c + wi[:, h:h + 1] * jnp.maximum(jnp.dot(qih, kic, preferred_element_type=F32), 0.0)
            sc_n = sc_n + wi[:, h:h + 1] * jnp.maximum(
                lax.dot_general(qih, kin, nt, preferred_element_type=F32), 0.0)
        key_ref[bb * tq:(bb + 1) * tq, 0:past] = _order_key(sc_c)
        key_ref[bb * tq:(bb + 1) * tq, past:n_keys] = _order_key(jnp.where(new_ok, sc_n, -jnp.inf))

    rg = min(CNT_ROWS, rows)

    def count_ge(cands):
        outs = [[] for _ in cands]
        for r0 in range(0, rows, rg):
            cs = [jnp.broadcast_to(c[r0:r0 + rg, :], (rg, LANES)) for c in cands]
            accs = [jnp.zeros((rg, LANES), F32) for _ in cands]
            for c0 in range(0, n_keys, LANES):
                kv = key_ref[r0:r0 + rg, c0:c0 + LANES]
                accs = [a + jnp.where(kv >= c, 1.0, 0.0) for a, c in zip(accs, cs)]
            for o, a in zip(outs, accs):
                o.append(jnp.sum(a, axis=1, keepdims=True))
        return [o[0] if len(o) == 1 else jnp.concatenate(o, axis=0) for o in outs]

    bpp = 2 if rows <= 64 else 1

    def bits_step(it, r):
        sh = 32 - bpp * (it + 1)
        step = jnp.left_shift(jnp.int32(1), sh)
        counts = count_ge([r + m * step for m in range(1, 1 << bpp)])
        inc = sum(jnp.where(n >= k_sel, 1, 0) for n in counts)
        return r + jnp.left_shift(inc, sh)

    thr = lax.fori_loop(0, 32 // bpp, bits_step, jnp.full((rows, 1), INT_MIN, I32))

    keys = key_ref[...]
    ge = keys >= thr
    cnt_ge = jnp.sum(jnp.where(ge, 1.0, 0.0), axis=1, keepdims=True)
    bias_ref[...] = jnp.where(ge & adm, 0.0, -jnp.inf)
    split_tie = (cnt_ge > k_sel) & (thr != KEY_NEG_INF)
    any_split = jnp.max(jnp.where(split_tie, 1.0, 0.0)) > 0.0

    @pl.when(any_split)
    def _():
        gt = keys > thr
        eq = keys == thr
        need = k_sel - jnp.sum(jnp.where(gt, 1.0, 0.0), axis=1, keepdims=True)

        def idx_step(it, last):
            cand = last + jnp.left_shift(jnp.int32(1), IDX_BITS - 1 - it)
            below = jnp.sum(jnp.where(eq & (col < cand), 1.0, 0.0), axis=1, keepdims=True)
            return jnp.where(below < need, cand, last)

        last = lax.fori_loop(0, IDX_BITS, idx_step, jnp.zeros((rows, 1), I32))
        sel = (gt | (eq & (col <= last))) & adm
        bias_ref[...] = jnp.where(sel, 0.0, -jnp.inf)

    rep = N_HEADS // N_KV_HEADS
    for bb in range(nb):
        bias = bias_ref[bb * tq:(bb + 1) * tq, :]
        for g in range(N_KV_HEADS):
            qg = jnp.concatenate([q_ref[bb, :, HEAD_DIM * (rep * g + r):HEAD_DIM * (rep * g + r + 1)]
                                  for r in range(rep)], axis=0)
            kn = _pad_rows(kn_ref[bb, :, HEAD_DIM * g:HEAD_DIM * (g + 1)].astype(BF16), LANES)
            vn = _pad_rows(vn_ref[bb, :, HEAD_DIM * g:HEAD_DIM * (g + 1)].astype(BF16), LANES)
            logits = jnp.concatenate(
                [jnp.dot(qg, kct_ref[bb, g].astype(BF16), preferred_element_type=F32),
                 lax.dot_general(qg, kn, nt, preferred_element_type=F32)], axis=1)
            ps, ss = [], []
            for r in range(rep):
                lg = logits[r * tq:(r + 1) * tq, :] + bias
                p = jnp.exp2(lg - jnp.max(lg, axis=1, keepdims=True))
                ss.append(jnp.sum(p, axis=1, keepdims=True))
                ps.append(p.astype(BF16))
            p_all = jnp.concatenate(ps, axis=0)
            o = (lax.dot_general(p_all[:, 0:past], vct_ref[bb, g].astype(BF16), nt, preferred_element_type=F32)
                 + jnp.dot(p_all[:, past:n_keys], vn, preferred_element_type=F32))
            for r in range(rep):
                h = rep * g + r
                o_ref[bb, :, HEAD_DIM * h:HEAD_DIM * (h + 1)] = (o[r * tq:(r + 1) * tq, :] / ss[r]).astype(BF16)


def _attn_call(q3, qi3, wi3, k_new, v_new, ki_new, kct, vct, kict, nb, k_sel, name):
    b, tq, _ = q3.shape
    past = kct.shape[-1]
    n_keys = past + LANES
    assert past % LANES == 0 and tq <= LANES and n_keys <= (1 << IDX_BITS) and b % nb == 0

    def rows(width):
        return pl.BlockSpec((nb, tq, width), lambda i: (i, 0, 0))

    kern = functools.partial(_attn_kernel, nb=nb, tq=tq, past=past, k_sel=float(k_sel))
    return pl.pallas_call(
        kern,
        grid=(b // nb,),
        in_specs=[rows(ATTN_DIM), rows(N_IDX_HEADS * IDX_DIM), rows(N_IDX_HEADS),
                  rows(KV_DIM), rows(KV_DIM), rows(IDX_DIM),
                  pl.BlockSpec((nb, N_KV_HEADS, HEAD_DIM, past), lambda i: (i, 0, 0, 0)),
                  pl.BlockSpec((nb, N_KV_HEADS, HEAD_DIM, past), lambda i: (i, 0, 0, 0)),
                  pl.BlockSpec((nb, IDX_DIM, past), lambda i: (i, 0, 0))],
        out_specs=rows(ATTN_DIM),
        out_shape=jax.ShapeDtypeStruct((b, tq, ATTN_DIM), BF16),
        scratch_shapes=[pltpu.VMEM((nb * tq, n_keys), I32), pltpu.VMEM((nb * tq, n_keys), F32)],
        compiler_params=_cparams(("arbitrary",)),
        name=name,
    )(q3, qi3, wi3, k_new, v_new, ki_new, kct, vct, kict)


VT_ROWS = HEAD_DIM + 16
SCORE_KEYS = 256
CNT_KEYS = 64
SUB_KEYS = 32


def _attn_t_kernel(q_ref, qi_ref, wit_ref, k_ref, vt_ref, ki_ref, o_ref, key_ref, bias_ref, hi_ref, lo_ref,
                   d_ref, lg_ref, p_ref, *, tq, n_keys, q0, t_new, past, k_sel):
    j = pl.program_id(1)
    qpos = lax.broadcasted_iota(I32, (1, tq), 1) + (q0 + j * tq)
    limit = past + jnp.minimum((qpos // CHUNK + 1) * CHUNK, t_new)
    nt = (((1,), (1,)), ((), ()))
    sb = SUB_KEYS
    kpos0 = lax.broadcasted_iota(I32, (sb, tq), 0)
    all_adm = past + min((q0 // CHUNK + 1) * CHUNK, t_new)

    qi_stack = jnp.concatenate([qi_ref[:, IDX_DIM * h:IDX_DIM * (h + 1)] for h in range(N_IDX_HEADS)], axis=0)
    wit = wit_ref[...]
    kc = min(SCORE_KEYS, n_keys)
    for c0 in range(0, n_keys, kc):
        d_ref[...] = lax.dot_general(ki_ref[c0:c0 + kc, :], qi_stack, nt, preferred_element_type=F32)
        for s0 in range(0, kc, sb):
            sc = wit[0:1, :] * jnp.maximum(d_ref[s0:s0 + sb, 0:tq], 0.0)
            for h in range(1, N_IDX_HEADS):
                sc = sc + wit[h:h + 1, :] * jnp.maximum(d_ref[s0:s0 + sb, h * tq:(h + 1) * tq], 0.0)
            if c0 + s0 + sb > all_adm:
                sc = jnp.where(kpos0 + (c0 + s0) < limit, sc, -jnp.inf)
            keys = _order_key(sc)
            key_ref[c0 + s0:c0 + s0 + sb, :] = keys
            hi_ref[c0 + s0:c0 + s0 + sb, :] = (keys >> 16).astype(I16)
            lo_ref[c0 + s0:c0 + s0 + sb, :] = (keys ^ 0x8000).astype(I16)

    sl = min(CNT_KEYS, n_keys)

    def search16(ref):
        def bit_step(it, r):
            cand = r + jnp.left_shift(jnp.int32(1), 15 - it)
            cb = jnp.broadcast_to((cand - 32768).astype(I16), (sl, tq))
            acc = jnp.zeros((sl, tq), I16)
            for c0 in range(0, n_keys, sl):
                acc = acc + jnp.where(ref[c0:c0 + sl, :] >= cb, I16(1), I16(0))
            cnt = jnp.sum(acc.astype(I32), axis=0, keepdims=True)
            return jnp.where(cnt >= k_sel, cand, r)
        return lax.fori_loop(0, 16, bit_step, jnp.zeros((1, tq), I32))

    if n_keys > k_sel:
        r_hi = search16(hi_ref)
        h16 = jnp.broadcast_to((r_hi - 32768).astype(I16), (sb, tq))
        for s0 in range(0, n_keys, sb):
            hi = hi_ref[s0:s0 + sb, :]
            lo_ref[s0:s0 + sb, :] = jnp.where(hi == h16, lo_ref[s0:s0 + sb, :],
                                              jnp.where(hi > h16, I16(32767), I16(-32768)))
        thr = ((r_hi - 32768) << 16) | search16(lo_ref)
    else:
        thr = jnp.full((1, tq), KEY_NEG_INF, I32)

    cnt = jnp.zeros((sb, tq), F32)
    for s0 in range(0, n_keys, sb):
        ge = key_ref[s0:s0 + sb, :] >= thr
        cnt = cnt + jnp.where(ge, 1.0, 0.0)
        if s0 + sb > all_adm:
            ge = ge & (kpos0 + s0 < limit)
        bias_ref[s0:s0 + sb, :] = jnp.where(ge, 0.0, -jnp.inf)
    cnt_ge = jnp.sum(cnt, axis=0, keepdims=True)
    split_tie = (cnt_ge > k_sel) & (thr != KEY_NEG_INF)
    any_split = jnp.max(jnp.where(split_tie, 1.0, 0.0)) > 0.0

    @pl.when(any_split)
    def _():
        keys = key_ref[...]
        kpos = lax.broadcasted_iota(I32, (n_keys, tq), 0)
        gt = keys > thr
        eq = keys == thr
        need = k_sel - jnp.sum(jnp.where(gt, 1.0, 0.0), axis=0, keepdims=True)

        def idx_step(it, last):
            cand = last + jnp.left_shift(jnp.int32(1), IDX_BITS - 1 - it)
            below = jnp.sum(jnp.where(eq & (kpos < cand), 1.0, 0.0), axis=0, keepdims=True)
            return jnp.where(below < need, cand, last)

        last = lax.fori_loop(0, IDX_BITS, idx_step, jnp.zeros((1, tq), I32))
        sel = (gt | (eq & (kpos <= last))) & (kpos < limit)
        bias_ref[...] = jnp.where(sel, 0.0, -jnp.inf)

    rep = N_HEADS // N_KV_HEADS
    for g in range(N_KV_HEADS):
        qg = jnp.concatenate([q_ref[:, HEAD_DIM * (rep * g + r):HEAD_DIM * (rep * g + r + 1)] for r in range(rep)],
                             axis=0)
        lg_ref[...] = lax.dot_general(k_ref[:, HEAD_DIM * g:HEAD_DIM * (g + 1)], qg, nt, preferred_element_type=F32)
        for r in range(rep):
            cols = slice(r * tq, (r + 1) * tq)
            mx = jnp.full((sb, tq), -jnp.inf, F32)
            for s0 in range(0, n_keys, sb):
                mx = jnp.maximum(mx, lg_ref[s0:s0 + sb, cols] + bias_ref[s0:s0 + sb, :])
            mb = jnp.broadcast_to(jnp.max(mx, axis=0, keepdims=True), (sb, tq))
            for s0 in range(0, n_keys, sb):
                p_ref[s0:s0 + sb, cols] = jnp.exp2(lg_ref[s0:s0 + sb, cols] + bias_ref[s0:s0 + sb, :] - mb).astype(BF16)
        vt_ext = jnp.concatenate([vt_ref[HEAD_DIM * g:HEAD_DIM * (g + 1), :],
                                  jnp.ones((VT_ROWS - HEAD_DIM, n_keys), BF16)], axis=0)
        o = jnp.dot(vt_ext, p_ref[...], preferred_element_type=F32)
        o = o[0:HEAD_DIM, :] / o[HEAD_DIM:HEAD_DIM + 1, :]
        for r in range(rep):
            h = rep * g + r
            o_ref[HEAD_DIM * h:HEAD_DIM * (h + 1), :] = o[:, r * tq:(r + 1) * tq].astype(BF16)


def _attn_t_call(q3, qi3, wit3, k_all, vt_all, ki_all, tq, q_blk0, n_q_blk, n_keys, t_new, past, k_sel, name):
    b = q3.shape[0]
    assert n_keys % LANES == 0 and n_keys <= (1 << IDX_BITS) and n_keys <= k_all.shape[1] and tq % LANES == 0
    kern = functools.partial(_attn_t_kernel, tq=tq, n_keys=n_keys, q0=q_blk0 * tq, t_new=t_new, past=past,
                             k_sel=float(k_sel))
    return pl.pallas_call(
        kern,
        grid=(b, n_q_blk),
        in_specs=[pl.BlockSpec((None, tq, ATTN_DIM), lambda i, j: (i, q_blk0 + j, 0)),
                  pl.BlockSpec((None, tq, N_IDX_HEADS * IDX_DIM), lambda i, j: (i, q_blk0 + j, 0)),
                  pl.BlockSpec((None, N_IDX_HEADS, tq), lambda i, j: (i, 0, q_blk0 + j)),
                  pl.BlockSpec((None, n_keys, KV_DIM), lambda i, j: (i, 0, 0)),
                  pl.BlockSpec((None, KV_DIM, n_keys), lambda i, j: (i, 0, 0)),
                  pl.BlockSpec((None, n_keys, IDX_DIM), lambda i, j: (i, 0, 0))],
        out_specs=pl.BlockSpec((None, ATTN_DIM, tq), lambda i, j: (i, 0, j)),
        out_shape=jax.ShapeDtypeStruct((b, ATTN_DIM, n_q_blk * tq), BF16),
        scratch_shapes=[pltpu.VMEM((n_keys, tq), I32), pltpu.VMEM((n_keys, tq), F32),
                        pltpu.VMEM((n_keys, tq), I16), pltpu.VMEM((n_keys, tq), I16),
                        pltpu.VMEM((min(SCORE_KEYS, n_keys), N_IDX_HEADS * tq), F32),
                        pltpu.VMEM((n_keys, N_HEADS // N_KV_HEADS * tq), F32),
                        pltpu.VMEM((n_keys, N_HEADS // N_KV_HEADS * tq), BF16)],
        compiler_params=_cparams(("arbitrary", "arbitrary")),
        name=name,
    )(q3, qi3, wit3, k_all, vt_all, ki_all)


FF_CHUNKS = ((0, D_FF),)


def _post_kernel(x_ref, conv_ref, attn_ref, gt1_ref, sc2_ref, sh2_ref, gt2_ref, g2_ref, gf_ref,
                 wout_ref, wup_ref, wdn_ref, fw_ref, fb_ref, hist_ref,
                 y_ref, newffn_ref, *scratch, tm, t_len, tiles_per_batch, attn_transposed):
    nseg = max(1, tm // t_len)
    seg = tm // nseg
    i = pl.program_id(0)
    if nseg == 1:
        carry_ref = scratch[0]

        @pl.when(i % tiles_per_batch == 0)
        def _():
            carry_ref[...] = hist_ref[0]

    w_attn = wout_ref[CONV_DIM:CONV_DIM + ATTN_DIM, :]
    if attn_transposed:
        mix_attn = lax.dot_general(attn_ref[...], w_attn, (((0,), (0,)), ((), ())), preferred_element_type=F32)
    else:
        mix_attn = jnp.dot(attn_ref[...], w_attn, preferred_element_type=F32)
    mix = jnp.dot(conv_ref[...], wout_ref[0:CONV_DIM, :], preferred_element_type=F32) + mix_attn
    x1 = x_ref[...] + gt1_ref[...] * mix
    h2 = _rmsnorm_mod(x1, g2_ref[...], sc2_ref[...], sh2_ref[...]).astype(BF16)

    row = lax.broadcasted_iota(I32, (SUBLANES, 1), 0)

    def causal3(u, col0, width):
        w0 = fw_ref[0:1, col0:col0 + width]
        w1 = fw_ref[1:2, col0:col0 + width]
        w2 = fw_ref[2:3, col0:col0 + width]
        outs = []
        for s in range(nseg):
            us = u[s * seg:(s + 1) * seg, :]
            if nseg == 1:
                h0 = carry_ref[0:1, col0:col0 + width]
                h1 = carry_ref[1:2, col0:col0 + width]
            else:
                h0 = hist_ref[s, 0:1, col0:col0 + width]
                h1 = hist_ref[s, 1:2, col0:col0 + width]
            p1 = pltpu.roll(us, 1, axis=0)
            p2 = pltpu.roll(us, 2, axis=0)
            p1 = jnp.concatenate([jnp.where(row == 0, h1, p1[0:SUBLANES, :]), p1[SUBLANES:, :]], axis=0)
            p2 = jnp.concatenate([jnp.where(row == 0, h0, jnp.where(row == 1, h1, p2[0:SUBLANES, :])),
                                  p2[SUBLANES:, :]], axis=0)
            outs.append(us * w2 + p1 * w1 + p2 * w0 + fb_ref[:, col0:col0 + width])
            if nseg == 1:
                carry_ref[:, col0:col0 + width] = us[seg - 2:seg, :]
            else:
                newffn_ref[s, :, col0:col0 + width] = us[seg - 2:seg, :]
        return outs[0] if nseg == 1 else jnp.concatenate(outs, axis=0)

    acc = jnp.zeros((tm, D_MODEL), F32)
    for c0, cw in FF_CHUNKS:
        ua = jnp.dot(h2, wup_ref[:, c0:c0 + cw], preferred_element_type=F32)
        ug = jnp.dot(h2, wup_ref[:, D_FF + c0:D_FF + c0 + cw], preferred_element_type=F32)
        a = causal3(ua, c0, cw)
        g = causal3(ug, D_FF + c0, cw)
        acc = acc + jnp.dot((a * _silu(g)).astype(BF16), wdn_ref[c0:c0 + cw, :], preferred_element_type=F32)

    if nseg == 1:
        @pl.when(i % tiles_per_batch == tiles_per_batch - 1)
        def _():
            newffn_ref[0] = carry_ref[...]

    x2 = x1 + gt2_ref[...] * acc
    ms = jnp.mean(x2 * x2, axis=-1, keepdims=True)
    y_ref[...] = x2 * lax.rsqrt(ms + EPS) * gf_ref[...]


def _post_call(x2d, conv2d, attn, gt1, sc2, sh2, gt2, g2, gf, wout, wup, wdn, fw, fb, hist, tm, t_len, name):
    r, d = x2d.shape
    nt = r // tm
    nb = hist.shape[0]
    nseg = max(1, tm // t_len)
    tpb = max(1, t_len // tm)
    assert nt * nseg == nb * tpb and t_len >= FFN_CONV_WIDTH - 1
    if gt1.ndim == 3:
        mod_spec = pl.BlockSpec((None, 1, d), lambda i: (i // tpb, 0, 0))
    else:
        mod_spec = pl.BlockSpec((tm, d), lambda i: (i, 0))

    def rows(width):
        return pl.BlockSpec((tm, width), lambda i: (i, 0))

    def const(shape):
        return pl.BlockSpec(shape, lambda i: (0,) * len(shape), pipeline_mode=pl.Buffered(1))

    state_spec = pl.BlockSpec((nseg, FFN_CONV_WIDTH - 1, 2 * D_FF), lambda i: (i * nseg // tpb, 0, 0))
    attn_t = attn.ndim == 3
    if attn_t:
        assert nseg == 1
        attn_spec = pl.BlockSpec((None, ATTN_DIM, tm), lambda i: (i // tpb, 0, i % tpb))
    else:
        attn_spec = rows(ATTN_DIM)
    kern = functools.partial(_post_kernel, tm=tm, t_len=t_len, tiles_per_batch=tpb, attn_transposed=attn_t)
    scratch = [pltpu.VMEM((FFN_CONV_WIDTH - 1, 2 * D_FF), F32)] if nseg == 1 else []
    return pl.pallas_call(
        kern,
        grid=(nt,),
        in_specs=[rows(d), rows(CONV_DIM), attn_spec, mod_spec, mod_spec, mod_spec, mod_spec,
                  const((1, d)), const((1, d)),
                  const((CONV_DIM + ATTN_DIM, d)), const((d, 2 * D_FF)), const((D_FF, d)),
                  const((FFN_CONV_WIDTH, 2 * D_FF)), const((1, 2 * D_FF)),
                  state_spec],
        out_specs=(rows(d), state_spec),
        out_shape=(jax.ShapeDtypeStruct((r, d), F32),
                   jax.ShapeDtypeStruct((nb, FFN_CONV_WIDTH - 1, 2 * D_FF), F32)),
        scratch_shapes=scratch,
        compiler_params=_cparams(("arbitrary",)),
        name=name,
    )(x2d, conv2d, attn, gt1, sc2, sh2, gt2, g2, gf, wout, wup, wdn, fw, fb, hist)


def _rope_tables(pos):
    half = ROT_DIM // 2
    inv = 1.0 / (ROPE_THETA ** (jnp.arange(0, ROT_DIM, 2, dtype=F32) / ROT_DIM))
    ang = pos.astype(F32)[:, None] * inv[None, :]
    cos, sin = jnp.cos(ang), jnp.sin(ang)
    t = pos.shape[0]
    rest1 = jnp.ones((t, HEAD_DIM - ROT_DIM), F32)
    rest0 = jnp.zeros((t, HEAD_DIM - ROT_DIM), F32)
    z = jnp.zeros((t, half), F32)
    c64 = jnp.concatenate([cos, cos, rest1], axis=1)
    a64 = jnp.concatenate([z, sin, rest0], axis=1)
    b64 = jnp.concatenate([-sin, z, rest0], axis=1)
    return tuple(jnp.concatenate([m, m], axis=1) for m in (c64, a64, b64))


def _pad_w_in(w_in):
    d = w_in.shape[0]
    return jnp.concatenate(
        [w_in[:, :COL_KI + IDX_DIM], jnp.zeros((d, LANES - IDX_DIM), F32),
         w_in[:, COL_KI + IDX_DIM:], jnp.zeros((d, LANES - N_IDX_HEADS), F32)], axis=1).astype(BF16)


def _group_tiles(b, t, cached):
    if cached:
        return dict(tm_in=b * t, tt_conv=t, tq=t, tm_post=b * t, nb_attn=4)
    return dict(tm_in=1024, tt_conv=512, tq=256, tm_post=512, nb_attn=1)


def _layer_group(x, mods, pos, conv_hist, ffn_hist, past, w, *, per_row_mod, tm_in, tt_conv, tq, tm_post, tag, nb_attn=1):
    (g1, w_in_p, dw_w, dw_b, ln_g, ln_b, w_out, g2, w_up, fdw_w, fdw_b, w_down, gf) = w
    b, t, d = x.shape
    r = b * t
    x2d = x.reshape(r, d)
    if per_row_mod:
        sh1, sc1, gt1, sh2, sc2, gt2 = [jnp.repeat(m, t, axis=0) for m in mods]
        tabs = tuple(jnp.tile(m, (b, 1)) for m in _rope_tables(pos))
    else:
        sh1, sc1, gt1, sh2, sc2, gt2 = [m[:, None, :] for m in mods]
        tabs = _rope_tables(pos)

    outs = _inproj_call(x2d, sc1, sh1, g1, w_in_p, tabs, tm_in, "inproj_" + tag)
    glu, q, qi = outs[:3]
    glu3 = glu.reshape(b, t, CONV_DIM)
    q3, qi3 = q.reshape(b, t, ATTN_DIM), qi.reshape(b, t, -1)

    conv_out = _conv_call(glu3, conv_hist, dw_w, dw_b, ln_g, ln_b, tt_conv, "conv_" + tag)
    new_conv = glu3[:, t - CONV_HIST:, :]

    if past is None:
        k_bf, ki_bf, kt, vt, vt_bf, kit, wit = outs[3:]
        k_sel = min(TOPK_MAX, t // 4)
        pieces = [_attn_t_call(q3, qi3, wit, k_bf.reshape(b, t, KV_DIM), vt_bf, ki_bf.reshape(b, t, IDX_DIM),
                               tq, i, 1, (i + 1) * tq, t, 0, k_sel, "attn_%s%d" % (tag, i)) for i in range(t // tq)]
        attn_out = jnp.concatenate(pieces, axis=2)
        new_k = jnp.transpose(kt.reshape(b, N_KV_HEADS, HEAD_DIM, t), (0, 3, 1, 2))
        new_v = jnp.transpose(vt.reshape(b, N_KV_HEADS, HEAD_DIM, t), (0, 3, 1, 2))
        new_ki = jnp.transpose(kit, (0, 2, 1))
    else:
        k, v, ki, wi = outs[3:]
        ck, cv, cki = past
        k3, v3, ki3 = k.reshape(b, t, KV_DIM), v.reshape(b, t, KV_DIM), ki.reshape(b, t, IDX_DIM)
        k_sel = min(TOPK_MAX, (ck.shape[1] + t) // 4)
        attn_out = _attn_call(q3, qi3, wi.reshape(b, t, -1), k3, v3, ki3,
                              jnp.transpose(ck, (0, 2, 3, 1)), jnp.transpose(cv, (0, 2, 3, 1)),
                              jnp.transpose(cki, (0, 2, 1)), nb_attn, k_sel, "attn_" + tag).reshape(r, ATTN_DIM)
        new_k, new_v, new_ki = k3.reshape(b, t, N_KV_HEADS, HEAD_DIM), v3.reshape(b, t, N_KV_HEADS, HEAD_DIM), ki3

    y, new_ffn = _post_call(x2d, conv_out.reshape(r, CONV_DIM), attn_out,
                            gt1, sc2, sh2, gt2, g2, gf, w_out, w_up, w_down, fdw_w, fdw_b, ffn_hist,
                            tm_post, t, "post_" + tag)
    return (y.reshape(b, t, d), new_k, new_v, new_ki, new_conv, new_ffn)


def kernel(x_prompt, x_sample, cache_k, cache_v, cache_kidx, state_conv, state_ffn_conv, c_prompt, c_sample,
           w_ada, b_ada, norm1_g, w_in, conv_dw_w, conv_dw_b, conv_ln_g, conv_ln_b, w_out, norm2_g,
           w_up, ffn_dw_w, ffn_dw_b, w_down, final_norm_g):
    depth = w_ada.shape[0]
    assert depth == 1, "the final norm is fused into the single layer's last kernel"
    bp, sp, d = x_prompt.shape
    bs, ts, _ = x_sample.shape
    past_len = cache_k.shape[2]
    l = 0
    mod = _mod_call(jnp.concatenate([c_prompt, c_sample], axis=0), w_ada[l], b_ada[l])
    mods_p = jnp.split(mod[:bp], 6, axis=-1)
    mods_s = jnp.split(mod[bp:], 6, axis=-1)
    w = (norm1_g[l].reshape(1, d), _pad_w_in(w_in[l]), conv_dw_w[l], conv_dw_b[l], conv_ln_g[l], conv_ln_b[l],
         w_out[l].astype(BF16), norm2_g[l].reshape(1, d), w_up[l].astype(BF16), ffn_dw_w[l],
         ffn_dw_b[l].reshape(1, -1), w_down[l].astype(BF16), final_norm_g.reshape(1, d))

    conv0 = jnp.zeros((bp, CONV_HIST, CONV_DIM), F32)
    ffn0 = jnp.zeros((bp, FFN_CONV_WIDTH - 1, 2 * D_FF), F32)
    out_p = _layer_group(x_prompt, mods_p, jnp.arange(sp, dtype=I32), conv0, ffn0, None, w,
                         per_row_mod=False, tag="p", **_group_tiles(bp, sp, cached=False))
    out_s = _layer_group(x_sample, mods_s, past_len + jnp.arange(ts, dtype=I32), state_conv[l], state_ffn_conv[l],
                         (cache_k[l], cache_v[l], cache_kidx[l]), w,
                         per_row_mod=True, tag="s", **_group_tiles(bs, ts, cached=True))
    y_p, k_p, v_p, ki_p, conv_p, ffn_p = out_p
    y_s, k_s, v_s, ki_s, conv_s, ffn_s = out_s
    st = lambda a: a[None]
    return (y_p, y_s, st(k_p), st(v_p), st(ki_p), st(conv_p), st(ffn_p),
            st(k_s), st(v_s), st(ki_s), st(conv_s), st(ffn_s))
```

```python
import functools

import jax
import jax.numpy as jnp
from jax import lax
from jax.experimental import pallas as pl
from jax.experimental.pallas import tpu as pltpu

F32 = jnp.float32
BF16 = jnp.bfloat16
I32 = jnp.int32
I16 = jnp.int16

D_MODEL = 1024
CHUNK = 64
CONV_DIM = 512
CONV_WIDTH = 31
N_HEADS = 8
HEAD_DIM = 64
N_KV_HEADS = 2
ATTN_DIM = N_HEADS * HEAD_DIM
KV_DIM = N_KV_HEADS * HEAD_DIM
ROT_DIM = HEAD_DIM // 4
ROPE_THETA = 500000.0
N_IDX_HEADS = 8
IDX_DIM = 32
TOPK_MAX = 256
D_FF = 2816
FFN_CONV_WIDTH = 3
EPS = 1e-6

LANES = 128
SUBLANES = 8
VMEM_LIMIT = 52 * 1024 * 1024

COL_U, COL_UG, COL_Q, COL_K, COL_V, COL_QI = 0, 512, 1024, 1536, 1664, 1792
COL_KI = 2048
COL_WI = COL_KI + LANES
IN_PAD = COL_WI + LANES

Q_SCALE = HEAD_DIM ** -0.5 * 1.4426950408889634

INT_MIN = -2147483648
KEY_NEG_INF = -2139095040
IDX_BITS = 13


def _order_key(x):
    bits = pltpu.bitcast(x, I32)
    return jnp.where(bits < 0, INT_MIN - bits, bits)


def _cparams(sem):
    return pltpu.CompilerParams(dimension_semantics=sem, vmem_limit_bytes=VMEM_LIMIT)


def _silu(x):
    return x * jax.nn.sigmoid(x)


def _mod_kernel(c_ref, w_ref, b_ref, o_ref):
    s = _silu(c_ref[...]).astype(BF16)
    o_ref[...] = jnp.dot(s, w_ref[...].astype(BF16), preferred_element_type=F32) + b_ref[...]


def _mod_call(c_all, w_ada, b_ada):
    nb, d = c_all.shape
    n = w_ada.shape[1]
    tn = 512
    return pl.pallas_call(
        _mod_kernel,
        grid=(n // tn,),
        in_specs=[pl.BlockSpec((nb, d), lambda j: (0, 0)),
                  pl.BlockSpec((d, tn), lambda j: (0, j)),
                  pl.BlockSpec((1, tn), lambda j: (0, j))],
        out_specs=pl.BlockSpec((nb, tn), lambda j: (0, j)),
        out_shape=jax.ShapeDtypeStruct((nb, n), F32),
        compiler_params=_cparams(("arbitrary",)),
        name="mod",
    )(c_all, w_ada, b_ada.reshape(1, n))


def _rmsnorm_mod(x, g, sc, sh):
    ms = jnp.mean(x * x, axis=-1, keepdims=True)
    return (x * lax.rsqrt(ms + EPS) * g) * (1.0 + sc) + sh


def _inproj_kernel(x_ref, sc_ref, sh_ref, g_ref, w_ref, cos_ref, sa_ref, sb_ref, glu_ref, q_ref, qi_ref, *kv_refs,
                   kv_transposed):
    h = _rmsnorm_mod(x_ref[...], g_ref[...], sc_ref[...], sh_ref[...])
    z = jnp.dot(h.astype(BF16), w_ref[...], preferred_element_type=F32)
    glu_ref[...] = z[:, COL_U:COL_U + CONV_DIM] * jax.nn.sigmoid(z[:, COL_UG:COL_UG + CONV_DIM])
    cos, sa, sb = cos_ref[...], sa_ref[...], sb_ref[...]

    def rope(xs):
        return (xs * cos + pltpu.roll(xs, ROT_DIM // 2, axis=1) * sa
                + pltpu.roll(xs, LANES - ROT_DIM // 2, axis=1) * sb)

    for j in range(ATTN_DIM // LANES):
        c0 = COL_Q + LANES * j
        q_ref[:, LANES * j:LANES * (j + 1)] = (rope(z[:, c0:c0 + LANES]) * Q_SCALE).astype(BF16)
    qi_ref[...] = z[:, COL_QI:COL_QI + N_IDX_HEADS * IDX_DIM].astype(BF16)
    k = rope(z[:, COL_K:COL_K + KV_DIM])
    v = z[:, COL_V:COL_V + KV_DIM]
    ki_slab = z[:, COL_KI:COL_KI + LANES]
    wi_slab = z[:, COL_WI:COL_WI + LANES]
    if kv_transposed:
        kbf_ref, kibf_ref, kt_ref, vt_ref, vtbf_ref, kit_ref, wit_ref = kv_refs
        kbf_ref[...] = k.astype(BF16)
        kibf_ref[...] = ki_slab[:, 0:IDX_DIM].astype(BF16)
        kt_ref[...] = k.T
        vt = v.T
        vt_ref[...] = vt
        vtbf_ref[...] = vt.astype(BF16)
        kit_ref[...] = ki_slab.T[0:IDX_DIM, :]
        wit_ref[...] = wi_slab.T[0:N_IDX_HEADS, :]
    else:
        k_ref, v_ref, ki_ref, wi_ref = kv_refs
        k_ref[...] = k
        v_ref[...] = v
        ki_ref[...] = ki_slab[:, 0:IDX_DIM]
        wi_ref[...] = wi_slab[:, 0:N_IDX_HEADS]


def _inproj_call(x2, sc, sh, g1, w_in_p, tabs, tm, name):
    r, d = x2.shape
    nt = r // tm
    kv_t = sc.ndim == 3
    if kv_t:
        nb = sc.shape[0]
        tpb = nt // nb
        t_len = r // nb
        mod_spec = pl.BlockSpec((None, 1, d), lambda i: (i // tpb, 0, 0))
    else:
        mod_spec = pl.BlockSpec((tm, d), lambda i: (i, 0))
    ntab = tabs[0].shape[0] // tm
    tab_spec = pl.BlockSpec((tm, LANES), lambda i: (i % ntab, 0))

    def rows(width):
        return pl.BlockSpec((tm, width), lambda i: (i, 0))

    def rows_shape(width, dtype):
        return jax.ShapeDtypeStruct((r, width), dtype)

    def cols(width):
        return pl.BlockSpec((None, width, tm), lambda i: (i // tpb, 0, i % tpb))

    def cols_shape(width, dtype):
        return jax.ShapeDtypeStruct((nb, width, t_len), dtype)

    out_specs = [rows(CONV_DIM), rows(ATTN_DIM), rows(N_IDX_HEADS * IDX_DIM)]
    out_shapes = [rows_shape(CONV_DIM, F32), rows_shape(ATTN_DIM, BF16), rows_shape(N_IDX_HEADS * IDX_DIM, BF16)]
    if kv_t:
        out_specs += [rows(KV_DIM), rows(IDX_DIM), cols(KV_DIM), cols(KV_DIM), cols(KV_DIM), cols(IDX_DIM),
                      cols(N_IDX_HEADS)]
        out_shapes += [rows_shape(KV_DIM, BF16), rows_shape(IDX_DIM, BF16), cols_shape(KV_DIM, F32),
                       cols_shape(KV_DIM, F32), cols_shape(KV_DIM, BF16), cols_shape(IDX_DIM, F32),
                       cols_shape(N_IDX_HEADS, F32)]
    else:
        out_specs += [rows(KV_DIM), rows(KV_DIM), rows(IDX_DIM), rows(N_IDX_HEADS)]
        out_shapes += [rows_shape(KV_DIM, F32), rows_shape(KV_DIM, F32), rows_shape(IDX_DIM, F32),
                       rows_shape(N_IDX_HEADS, F32)]
    return pl.pallas_call(
        functools.partial(_inproj_kernel, kv_transposed=kv_t),
        grid=(nt,),
        in_specs=[rows(d), mod_spec, mod_spec,
                  pl.BlockSpec((1, d), lambda i: (0, 0)),
                  pl.BlockSpec((d, IN_PAD), lambda i: (0, 0)),
                  tab_spec, tab_spec, tab_spec],
        out_specs=tuple(out_specs),
        out_shape=tuple(out_shapes),
        compiler_params=_cparams(("arbitrary",)),
        name=name,
    )(x2, sc, sh, g1, w_in_p, *tabs)


CONV_HIST = CONV_WIDTH - 1
CONV_PAD = 32
CONV_RB = 64


def _conv_kernel(glu_ref, hist_ref, w_ref, b_ref, lg_ref, lb_ref, o_ref, ext_ref, y_ref, *, tt):
    t = pl.program_id(1)

    @pl.when(t == 0)
    def _():
        ext_ref[0:CONV_PAD - CONV_HIST, :] = jnp.zeros((CONV_PAD - CONV_HIST, CONV_DIM), F32)
        ext_ref[CONV_PAD - CONV_HIST:CONV_PAD, :] = hist_ref[...]

    @pl.when(t > 0)
    def _():
        ext_ref[0:CONV_PAD, :] = ext_ref[tt:tt + CONV_PAD, :]

    ext_ref[CONV_PAD:CONV_PAD + tt, :] = glu_ref[...]

    off = CONV_PAD - CONV_HIST
    rb = min(CONV_RB, tt)
    for r0 in range(0, tt, rb):
        for c0 in range(0, CONV_DIM, LANES):
            acc = None
            for b in range(SUBLANES):
                span = (CONV_WIDTH - 1 - b) // SUBLANES * SUBLANES
                mis = (off + b) % SUBLANES
                base = off + b - mis + r0
                cover = -(-(mis + rb + span) // SUBLANES) * SUBLANES
                win = ext_ref[base:base + cover, c0:c0 + LANES]
                if mis:
                    win = pltpu.roll(win, cover - mis, axis=0)
                for j in range(b, CONV_WIDTH, SUBLANES):
                    term = win[j - b:j - b + rb, :] * w_ref[j:j + 1, c0:c0 + LANES]
                    acc = term if acc is None else acc + term
            y_ref[r0:r0 + rb, c0:c0 + LANES] = acc + b_ref[:, c0:c0 + LANES]

    y = y_ref[...]
    mu = jnp.mean(y, axis=-1, keepdims=True)
    yc = y - mu
    var = jnp.mean(yc * yc, axis=-1, keepdims=True)
    o_ref[...] = _silu(yc * lax.rsqrt(var + EPS) * lg_ref[...] + lb_ref[...]).astype(BF16)


def _conv_call(glu3, hist, dw_w, dw_b, ln_g, ln_b, tt, name):
    b, t, c = glu3.shape
    vec = pl.BlockSpec((1, c), lambda i, j: (0, 0))
    return pl.pallas_call(
        functools.partial(_conv_kernel, tt=tt),
        grid=(b, t // tt),
        in_specs=[pl.BlockSpec((None, tt, c), lambda i, j: (i, j, 0)),
                  pl.BlockSpec((None, CONV_HIST, c), lambda i, j: (i, 0, 0)),
                  pl.BlockSpec((CONV_WIDTH, c), lambda i, j: (0, 0)),
                  vec, vec, vec],
        out_specs=pl.BlockSpec((None, tt, c), lambda i, j: (i, j, 0)),
        out_shape=jax.ShapeDtypeStruct((b, t, c), BF16),
        scratch_shapes=[pltpu.VMEM((tt + CONV_PAD, c), F32), pltpu.VMEM((tt, c), F32)],
        compiler_params=_cparams(("arbitrary", "arbitrary")),
        name=name,
    )(glu3, hist, dw_w, dw_b.reshape(1, c), ln_g.reshape(1, c), ln_b.reshape(1, c))


CNT_ROWS = 64


def _pad_rows(x, n):
    return jnp.concatenate([x, jnp.zeros((n - x.shape[0], x.shape[1]), x.dtype)], axis=0)


def _attn_kernel(q_ref, qi_ref, wi_ref, kn_ref, vn_ref, kin_ref, kct_ref, vct_ref, kict_ref, o_ref, key_ref, bias_ref,
                 *, nb, tq, past, k_sel):
    rows = nb * tq
    n_keys = past + LANES
    nt = (((1,), (1,)), ((), ()))
    col = lax.broadcasted_iota(I32, (rows, n_keys), 1)
    adm = col < past + tq
    new_ok = lax.broadcasted_iota(I32, (tq, LANES), 1) < tq

    for bb in range(nb):
        kic = kict_ref[bb].astype(BF16)
        kin = _pad_rows(kin_ref[bb].astype(BF16), LANES)
        wi = wi_ref[bb]
        sc_c = jnp.zeros((tq, past), F32)
        sc_n = jnp.zeros((tq, LANES), F32)
        for h in range(N_IDX_HEADS):
            qih = qi_ref[bb, :, IDX_DIM * h:IDX_DIM * (h + 1)]
            sc_c = sc_c + wi[:, h:h + 1] * jnp.maximum(jnp.dot(qih, kic, preferred_element_type=F32), 0.0)
            sc_n = sc_n + wi[:, h:h + 1] * jnp.maximum(
                lax.dot_general(qih, kin, nt, preferred_element_type=F32), 0.0)
        key_ref[bb * tq:(bb + 1) * tq, 0:past] = _order_key(sc_c)
        key_ref[bb * tq:(bb + 1) * tq, past:n_keys] = _order_key(jnp.where(new_ok, sc_n, -jnp.inf))

    rg = min(CNT_ROWS, rows)

    def count_ge(cands):
        outs = [[] for _ in cands]
        for r0 in range(0, rows, rg):
            cs = [jnp.broadcast_to(c[r0:r0 + rg, :], (rg, LANES)) for c in cands]
            accs = [jnp.zeros((rg, LANES), F32) for _ in cands]
            for c0 in range(0, n_keys, LANES):
                kv = key_ref[r0:r0 + rg, c0:c0 + LANES]
                accs = [a + jnp.where(kv >= c, 1.0, 0.0) for a, c in zip(accs, cs)]
            for o, a in zip(outs, accs):
                o.append(jnp.sum(a, axis=1, keepdims=True))
        return [o[0] if len(o) == 1 else jnp.concatenate(o, axis=0) for o in outs]

    bpp = 2 if rows <= 64 else 1

    def bits_step(it, r):
        sh = 32 - bpp * (it + 1)
        step = jnp.left_shift(jnp.int32(1), sh)
        counts = count_ge([r + m * step for m in range(1, 1 << bpp)])
        inc = sum(jnp.where(n >= k_sel, 1, 0) for n in counts)
        return r + jnp.left_shift(inc, sh)

    thr = lax.fori_loop(0, 32 // bpp, bits_step, jnp.full((rows, 1), INT_MIN, I32))

    keys = key_ref[...]
    ge = keys >= thr
    cnt_ge = jnp.sum(jnp.where(ge, 1.0, 0.0), axis=1, keepdims=True)
    bias_ref[...] = jnp.where(ge & adm, 0.0, -jnp.inf)
    split_tie = (cnt_ge > k_sel) & (thr != KEY_NEG_INF)
    any_split = jnp.max(jnp.where(split_tie, 1.0, 0.0)) > 0.0

    @pl.when(any_split)
    def _():
        gt = keys > thr
        eq = keys == thr
        need = k_sel - jnp.sum(jnp.where(gt, 1.0, 0.0), axis=1, keepdims=True)

        def idx_step(it, last):
            cand = last + jnp.left_shift(jnp.int32(1), IDX_BITS - 1 - it)
            below = jnp.sum(jnp.where(eq & (col < cand), 1.0, 0.0), axis=1, keepdims=True)
            return jnp.where(below < need, cand, last)

        last = lax.fori_loop(0, IDX_BITS, idx_step, jnp.zeros((rows, 1), I32))
        sel = (gt | (eq & (col <= last))) & adm
        bias_ref[...] = jnp.where(sel, 0.0, -jnp.inf)

    rep = N_HEADS // N_KV_HEADS
    for bb in range(nb):
        bias = bias_ref[bb * tq:(bb + 1) * tq, :]
        for g in range(N_KV_HEADS):
            qg = jnp.concatenate([q_ref[bb, :, HEAD_DIM * (rep * g + r):HEAD_DIM * (rep * g + r + 1)]
                                  for r in range(rep)], axis=0)
            kn = _pad_rows(kn_ref[bb, :, HEAD_DIM * g:HEAD_DIM * (g + 1)].astype(BF16), LANES)
            vn = _pad_rows(vn_ref[bb, :, HEAD_DIM * g:HEAD_DIM * (g + 1)].astype(BF16), LANES)
            logits = jnp.concatenate(
                [jnp.dot(qg, kct_ref[bb, g].astype(BF16), preferred_element_type=F32),
                 lax.dot_general(qg, kn, nt, preferred_element_type=F32)], axis=1)
            ps, ss = [], []
            for r in range(rep):
                lg = logits[r * tq:(r + 1) * tq, :] + bias
                p = jnp.exp2(lg - jnp.max(lg, axis=1, keepdims=True))
                ss.append(jnp.sum(p, axis=1, keepdims=True))
                ps.append(p.astype(BF16))
            p_all = jnp.concatenate(ps, axis=0)
            o = (lax.dot_general(p_all[:, 0:past], vct_ref[bb, g].astype(BF16), nt, preferred_element_type=F32)
                 + jnp.dot(p_all[:, past:n_keys], vn, preferred_element_type=F32))
            for r in range(rep):
                h = rep * g + r
                o_ref[bb, :, HEAD_DIM * h:HEAD_DIM * (h + 1)] = (o[r * tq:(r + 1) * tq, :] / ss[r]).astype(BF16)


def _attn_call(q3, qi3, wi3, k_new, v_new, ki_new, kct, vct, kict, nb, k_sel, name):
    b, tq, _ = q3.shape
    past = kct.shape[-1]
    n_keys = past + LANES
    assert past % LANES == 0 and tq <= LANES and n_keys <= (1 << IDX_BITS) and b % nb == 0

    def rows(width):
        return pl.BlockSpec((nb, tq, width), lambda i: (i, 0, 0))

    kern = functools.partial(_attn_kernel, nb=nb, tq=tq, past=past, k_sel=float(k_sel))
    return pl.pallas_call(
        kern,
        grid=(b // nb,),
        in_specs=[rows(ATTN_DIM), rows(N_IDX_HEADS * IDX_DIM), rows(N_IDX_HEADS),
                  rows(KV_DIM), rows(KV_DIM), rows(IDX_DIM),
                  pl.BlockSpec((nb, N_KV_HEADS, HEAD_DIM, past), lambda i: (i, 0, 0, 0)),
                  pl.BlockSpec((nb, N_KV_HEADS, HEAD_DIM, past), lambda i: (i, 0, 0, 0)),
                  pl.BlockSpec((nb, IDX_DIM, past), lambda i: (i, 0, 0))],
        out_specs=rows(ATTN_DIM),
        out_shape=jax.ShapeDtypeStruct((b, tq, ATTN_DIM), BF16),
        scratch_shapes=[pltpu.VMEM((nb * tq, n_keys), I32), pltpu.VMEM((nb * tq, n_keys), F32)],
        compiler_params=_cparams(("arbitrary",)),
        name=name,
    )(q3, qi3, wi3, k_new, v_new, ki_new, kct, vct, kict)


VT_ROWS = HEAD_DIM + 16
SCORE_KEYS = 128
LOGIT_KEYS = 256
CNT_KEYS = 64
SUB_KEYS = 32


def _attn_t_kernel(q_ref, qi_ref, wit_ref, k_ref, vt_ref, ki_ref, o_ref, key_ref, bias_ref, hi_ref, lo_ref,
                   d_ref, lg_ref, p_ref, *, tq, n_keys, q0, t_new, past, k_sel):
    j = pl.program_id(1)
    qpos = lax.broadcasted_iota(I32, (1, tq), 1) + (q0 + j * tq)
    limit = past + jnp.minimum((qpos // CHUNK + 1) * CHUNK, t_new)
    nt = (((1,), (1,)), ((), ()))
    sb = SUB_KEYS
    kpos0 = lax.broadcasted_iota(I32, (sb, tq), 0)
    all_adm = past + min((q0 // CHUNK + 1) * CHUNK, t_new)

    qi_stack = jnp.concatenate([qi_ref[:, IDX_DIM * h:IDX_DIM * (h + 1)] for h in range(N_IDX_HEADS)], axis=0)
    wit = wit_ref[...]
    kc = min(SCORE_KEYS, n_keys)
    for c0 in range(0, n_keys, kc):
        d_ref[...] = lax.dot_general(ki_ref[c0:c0 + kc, :], qi_stack, nt, preferred_element_type=F32)
        for s0 in range(0, kc, sb):
            sc = wit[0:1, :] * jnp.maximum(d_ref[s0:s0 + sb, 0:tq], 0.0)
            for h in range(1, N_IDX_HEADS):
                sc = sc + wit[h:h + 1, :] * jnp.maximum(d_ref[s0:s0 + sb, h * tq:(h + 1) * tq], 0.0)
            if c0 + s0 + sb > all_adm:
                sc = jnp.where(kpos0 + (c0 + s0) < limit, sc, -jnp.inf)
            keys = _order_key(sc)
            key_ref[c0 + s0:c0 + s0 + sb, :] = keys
            hi_ref[c0 + s0:c0 + s0 + sb, :] = (keys >> 16).astype(I16)
            lo_ref[c0 + s0:c0 + s0 + sb, :] = (keys ^ 0x8000).astype(I16)

    sl = min(CNT_KEYS, n_keys)

    def search16(ref):
        def bit_step(it, r):
            cand = r + jnp.left_shift(jnp.int32(1), 15 - it)
            cb = jnp.broadcast_to((cand - 32768).astype(I16), (sl, tq))
            acc = jnp.zeros((sl, tq), I16)
            for c0 in range(0, n_keys, sl):
                acc = acc + jnp.where(ref[c0:c0 + sl, :] >= cb, I16(1), I16(0))
            cnt = jnp.sum(acc.astype(I32), axis=0, keepdims=True)
            return jnp.where(cnt >= k_sel, cand, r)
        return lax.fori_loop(0, 16, bit_step, jnp.zeros((1, tq), I32))

    if n_keys > k_sel:
        r_hi = search16(hi_ref)
        h16 = jnp.broadcast_to((r_hi - 32768).astype(I16), (sb, tq))
        for s0 in range(0, n_keys, sb):
            hi = hi_ref[s0:s0 + sb, :]
            lo_ref[s0:s0 + sb, :] = jnp.where(hi == h16, lo_ref[s0:s0 + sb, :],
                                              jnp.where(hi > h16, I16(32767), I16(-32768)))
        thr = ((r_hi - 32768) << 16) | search16(lo_ref)
    else:
        thr = jnp.full((1, tq), KEY_NEG_INF, I32)

    cnt = jnp.zeros((sb, tq), F32)
    for s0 in range(0, n_keys, sb):
        ge = key_ref[s0:s0 + sb, :] >= thr
        cnt = cnt + jnp.where(ge, 1.0, 0.0)
        if s0 + sb > all_adm:
            ge = ge & (kpos0 + s0 < limit)
        bias_ref[s0:s0 + sb, :] = jnp.where(ge, 0.0, -jnp.inf)
    cnt_ge = jnp.sum(cnt, axis=0, keepdims=True)
    split_tie = (cnt_ge > k_sel) & (thr != KEY_NEG_INF)
    any_split = jnp.max(jnp.where(split_tie, 1.0, 0.0)) > 0.0

    @pl.when(any_split)
    def _():
        keys = key_ref[...]
        kpos = lax.broadcasted_iota(I32, (n_keys, tq), 0)
        gt = keys > thr
        eq = keys == thr
        need = k_sel - jnp.sum(jnp.where(gt, 1.0, 0.0), axis=0, keepdims=True)

        def idx_step(it, last):
            cand = last + jnp.left_shift(jnp.int32(1), IDX_BITS - 1 - it)
            below = jnp.sum(jnp.where(eq & (kpos < cand), 1.0, 0.0), axis=0, keepdims=True)
            return jnp.where(below < need, cand, last)

        last = lax.fori_loop(0, IDX_BITS, idx_step, jnp.zeros((1, tq), I32))
        sel = (gt | (eq & (kpos <= last))) & (kpos < limit)
        bias_ref[...] = jnp.where(sel, 0.0, -jnp.inf)

    rep = N_HEADS // N_KV_HEADS
    for g in range(N_KV_HEADS):
        qg = jnp.concatenate([q_ref[:, HEAD_DIM * (rep * g + r):HEAD_DIM * (rep * g + r + 1)] for r in range(rep)],
                             axis=0)
        for c0 in range(0, n_keys, LOGIT_KEYS):
            lg_ref[c0:c0 + LOGIT_KEYS, :] = lax.dot_general(k_ref[c0:c0 + LOGIT_KEYS, HEAD_DIM * g:HEAD_DIM * (g + 1)],
                                                           qg, nt, preferred_element_type=F32)
        for r in range(rep):
            cols = slice(r * tq, (r + 1) * tq)
            mx = jnp.full((sb, tq), -jnp.inf, F32)
            for s0 in range(0, n_keys, sb):
                mx = jnp.maximum(mx, lg_ref[s0:s0 + sb, cols] + bias_ref[s0:s0 + sb, :])
            mb = jnp.broadcast_to(jnp.max(mx, axis=0, keepdims=True), (sb, tq))
            for s0 in range(0, n_keys, sb):
                p_ref[s0:s0 + sb, cols] = jnp.exp2(lg_ref[s0:s0 + sb, cols] + bias_ref[s0:s0 + sb, :] - mb).astype(BF16)
        vt_ext = jnp.concatenate([vt_ref[HEAD_DIM * g:HEAD_DIM * (g + 1), :],
                                  jnp.ones((VT_ROWS - HEAD_DIM, n_keys), BF16)], axis=0)
        o = jnp.dot(vt_ext, p_ref[...], preferred_element_type=F32)
        o = o[0:HEAD_DIM, :] / o[HEAD_DIM:HEAD_DIM + 1, :]
        for r in range(rep):
            h = rep * g + r
            o_ref[HEAD_DIM * h:HEAD_DIM * (h + 1), :] = o[:, r * tq:(r + 1) * tq].astype(BF16)


def _attn_t_call(q3, qi3, wit3, k_all, vt_all, ki_all, tq, q_blk0, n_q_blk, n_keys, t_new, past, k_sel, name):
    b = q3.shape[0]
    assert n_keys % LANES == 0 and n_keys <= (1 << IDX_BITS) and n_keys <= k_all.shape[1] and tq % LANES == 0
    kern = functools.partial(_attn_t_kernel, tq=tq, n_keys=n_keys, q0=q_blk0 * tq, t_new=t_new, past=past,
                             k_sel=float(k_sel))
    return pl.pallas_call(
        kern,
        grid=(b, n_q_blk),
        in_specs=[pl.BlockSpec((None, tq, ATTN_DIM), lambda i, j: (i, q_blk0 + j, 0)),
                  pl.BlockSpec((None, tq, N_IDX_HEADS * IDX_DIM), lambda i, j: (i, q_blk0 + j, 0)),
                  pl.BlockSpec((None, N_IDX_HEADS, tq), lambda i, j: (i, 0, q_blk0 + j)),
                  pl.BlockSpec((None, n_keys, KV_DIM), lambda i, j: (i, 0, 0)),
                  pl.BlockSpec((None, KV_DIM, n_keys), lambda i, j: (i, 0, 0)),
                  pl.BlockSpec((None, n_keys, IDX_DIM), lambda i, j: (i, 0, 0))],
        out_specs=pl.BlockSpec((None, ATTN_DIM, tq), lambda i, j: (i, 0, j)),
        out_shape=jax.ShapeDtypeStruct((b, ATTN_DIM, n_q_blk * tq), BF16),
        scratch_shapes=[pltpu.VMEM((n_keys, tq), I32), pltpu.VMEM((n_keys, tq), F32),
                        pltpu.VMEM((n_keys, tq), I16), pltpu.VMEM((n_keys, tq), I16),
                        pltpu.VMEM((min(SCORE_KEYS, n_keys), N_IDX_HEADS * tq), F32),
                        pltpu.VMEM((n_keys, N_HEADS // N_KV_HEADS * tq), F32),
                        pltpu.VMEM((n_keys, N_HEADS // N_KV_HEADS * tq), BF16)],
        compiler_params=_cparams(("arbitrary", "arbitrary")),
        name=name,
    )(q3, qi3, wit3, k_all, vt_all, ki_all)


FF_CHUNKS = ((0, D_FF),)


def _post_kernel(x_ref, conv_ref, attn_ref, gt1_ref, sc2_ref, sh2_ref, gt2_ref, g2_ref, gf_ref,
                 wout_ref, wup_ref, wdn_ref, fw_ref, fb_ref, hist_ref,
                 y_ref, newffn_ref, *scratch, tm, t_len, tiles_per_batch, attn_transposed):
    nseg = max(1, tm // t_len)
    seg = tm // nseg
    i = pl.program_id(0)
    if nseg == 1:
        carry_ref = scratch[0]

        @pl.when(i % tiles_per_batch == 0)
        def _():
            carry_ref[...] = hist_ref[0]

    w_attn = wout_ref[CONV_DIM:CONV_DIM + ATTN_DIM, :]
    if attn_transposed:
        mix_attn = lax.dot_general(attn_ref[...], w_attn, (((0,), (0,)), ((), ())), preferred_element_type=F32)
    else:
        mix_attn = jnp.dot(attn_ref[...], w_attn, preferred_element_type=F32)
    mix = jnp.dot(conv_ref[...], wout_ref[0:CONV_DIM, :], preferred_element_type=F32) + mix_attn
    x1 = x_ref[...] + gt1_ref[...] * mix
    h2 = _rmsnorm_mod(x1, g2_ref[...], sc2_ref[...], sh2_ref[...]).astype(BF16)

    row = lax.broadcasted_iota(I32, (SUBLANES, 1), 0)

    def causal3(u, col0, width):
        w0 = fw_ref[0:1, col0:col0 + width]
        w1 = fw_ref[1:2, col0:col0 + width]
        w2 = fw_ref[2:3, col0:col0 + width]
        outs = []
        for s in range(nseg):
            us = u[s * seg:(s + 1) * seg, :]
            if nseg == 1:
                h0 = carry_ref[0:1, col0:col0 + width]
                h1 = carry_ref[1:2, col0:col0 + width]
            else:
                h0 = hist_ref[s, 0:1, col0:col0 + width]
                h1 = hist_ref[s, 1:2, col0:col0 + width]
            p1 = pltpu.roll(us, 1, axis=0)
            p2 = pltpu.roll(us, 2, axis=0)
            p1 = jnp.concatenate([jnp.where(row == 0, h1, p1[0:SUBLANES, :]), p1[SUBLANES:, :]], axis=0)
            p2 = jnp.concatenate([jnp.where(row == 0, h0, jnp.where(row == 1, h1, p2[0:SUBLANES, :])),
                                  p2[SUBLANES:, :]], axis=0)
            outs.append(us * w2 + p1 * w1 + p2 * w0 + fb_ref[:, col0:col0 + width])
            if nseg == 1:
                carry_ref[:, col0:col0 + width] = us[seg - 2:seg, :]
            else:
                newffn_ref[s, :, col0:col0 + width] = us[seg - 2:seg, :]
        return outs[0] if nseg == 1 else jnp.concatenate(outs, axis=0)

    acc = jnp.zeros((tm, D_MODEL), F32)
    for c0, cw in FF_CHUNKS:
        ua = jnp.dot(h2, wup_ref[:, c0:c0 + cw], preferred_element_type=F32)
        ug = jnp.dot(h2, wup_ref[:, D_FF + c0:D_FF + c0 + cw], preferred_element_type=F32)
        a = causal3(ua, c0, cw)
        g = causal3(ug, D_FF + c0, cw)
        acc = acc + jnp.dot((a * _silu(g)).astype(BF16), wdn_ref[c0:c0 + cw, :], preferred_element_type=F32)

    if nseg == 1:
        @pl.when(i % tiles_per_batch == tiles_per_batch - 1)
        def _():
            newffn_ref[0] = carry_ref[...]

    x2 = x1 + gt2_ref[...] * acc
    ms = jnp.mean(x2 * x2, axis=-1, keepdims=True)
    y_ref[...] = x2 * lax.rsqrt(ms + EPS) * gf_ref[...]


def _post_call(x2d, conv2d, attn, gt1, sc2, sh2, gt2, g2, gf, wout, wup, wdn, fw, fb, hist, tm, t_len, name):
    r, d = x2d.shape
    nt = r // tm
    nb = hist.shape[0]
    nseg = max(1, tm // t_len)
    tpb = max(1, t_len // tm)
    assert nt * nseg == nb * tpb and t_len >= FFN_CONV_WIDTH - 1
    if gt1.ndim == 3:
        mod_spec = pl.BlockSpec((None, 1, d), lambda i: (i // tpb, 0, 0))
    else:
        mod_spec = pl.BlockSpec((tm, d), lambda i: (i, 0))

    def rows(width):
        return pl.BlockSpec((tm, width), lambda i: (i, 0))

    def const(shape):
        return pl.BlockSpec(shape, lambda i: (0,) * len(shape), pipeline_mode=pl.Buffered(1))

    state_spec = pl.BlockSpec((nseg, FFN_CONV_WIDTH - 1, 2 * D_FF), lambda i: (i * nseg // tpb, 0, 0))
    attn_t = attn.ndim == 3
    if attn_t:
        assert nseg == 1
        attn_spec = pl.BlockSpec((None, ATTN_DIM, tm), lambda i: (i // tpb, 0, i % tpb))
    else:
        attn_spec = rows(ATTN_DIM)
    kern = functools.partial(_post_kernel, tm=tm, t_len=t_len, tiles_per_batch=tpb, attn_transposed=attn_t)
    scratch = [pltpu.VMEM((FFN_CONV_WIDTH - 1, 2 * D_FF), F32)] if nseg == 1 else []
    return pl.pallas_call(
        kern,
        grid=(nt,),
        in_specs=[rows(d), rows(CONV_DIM), attn_spec, mod_spec, mod_spec, mod_spec, mod_spec,
                  const((1, d)), const((1, d)),
                  const((CONV_DIM + ATTN_DIM, d)), const((d, 2 * D_FF)), const((D_FF, d)),
                  const((FFN_CONV_WIDTH, 2 * D_FF)), const((1, 2 * D_FF)),
                  state_spec],
        out_specs=(rows(d), state_spec),
        out_shape=(jax.ShapeDtypeStruct((r, d), F32),
                   jax.ShapeDtypeStruct((nb, FFN_CONV_WIDTH - 1, 2 * D_FF), F32)),
        scratch_shapes=scratch,
        compiler_params=_cparams(("arbitrary",)),
        name=name,
    )(x2d, conv2d, attn, gt1, sc2, sh2, gt2, g2, gf, wout, wup, wdn, fw, fb, hist)


def _rope_tables(pos):
    half = ROT_DIM // 2
    inv = 1.0 / (ROPE_THETA ** (jnp.arange(0, ROT_DIM, 2, dtype=F32) / ROT_DIM))
    ang = pos.astype(F32)[:, None] * inv[None, :]
    cos, sin = jnp.cos(ang), jnp.sin(ang)
    t = pos.shape[0]
    rest1 = jnp.ones((t, HEAD_DIM - ROT_DIM), F32)
    rest0 = jnp.zeros((t, HEAD_DIM - ROT_DIM), F32)
    z = jnp.zeros((t, half), F32)
    c64 = jnp.concatenate([cos, cos, rest1], axis=1)
    a64 = jnp.concatenate([z, sin, rest0], axis=1)
    b64 = jnp.concatenate([-sin, z, rest0], axis=1)
    return tuple(jnp.concatenate([m, m], axis=1) for m in (c64, a64, b64))


def _pad_w_in(w_in):
    d = w_in.shape[0]
    return jnp.concatenate(
        [w_in[:, :COL_KI + IDX_DIM], jnp.zeros((d, LANES - IDX_DIM), F32),
         w_in[:, COL_KI + IDX_DIM:], jnp.zeros((d, LANES - N_IDX_HEADS), F32)], axis=1).astype(BF16)


def _group_tiles(b, t, cached):
    if cached:
        return dict(tm_in=b * t, tt_conv=t, tq=t, tm_post=b * t, nb_attn=4)
    return dict(tm_in=1024, tt_conv=512, tq=256, tm_post=512, nb_attn=1)


def _layer_group(x, mods, pos, conv_hist, ffn_hist, past, w, *, per_row_mod, tm_in, tt_conv, tq, tm_post, tag, nb_attn=1):
    (g1, w_in_p, dw_w, dw_b, ln_g, ln_b, w_out, g2, w_up, fdw_w, fdw_b, w_down, gf) = w
    b, t, d = x.shape
    r = b * t
    x2d = x.reshape(r, d)
    if per_row_mod:
        sh1, sc1, gt1, sh2, sc2, gt2 = [jnp.repeat(m, t, axis=0) for m in mods]
        tabs = tuple(jnp.tile(m, (b, 1)) for m in _rope_tables(pos))
    else:
        sh1, sc1, gt1, sh2, sc2, gt2 = [m[:, None, :] for m in mods]
        tabs = _rope_tables(pos)

    outs = _inproj_call(x2d, sc1, sh1, g1, w_in_p, tabs, tm_in, "inproj_" + tag)
    glu, q, qi = outs[:3]
    glu3 = glu.reshape(b, t, CONV_DIM)
    q3, qi3 = q.reshape(b, t, ATTN_DIM), qi.reshape(b, t, -1)

    conv_out = _conv_call(glu3, conv_hist, dw_w, dw_b, ln_g, ln_b, tt_conv, "conv_" + tag)
    new_conv = glu3[:, t - CONV_HIST:, :]

    if past is None:
        k_bf, ki_bf, kt, vt, vt_bf, kit, wit = outs[3:]
        k_sel = min(TOPK_MAX, t // 4)
        pieces = [_attn_t_call(q3, qi3, wit, k_bf.reshape(b, t, KV_DIM), vt_bf, ki_bf.reshape(b, t, IDX_DIM),
                               tq, i, 1, (i + 1) * tq, t, 0, k_sel, "attn_%s%d" % (tag, i)) for i in range(t // tq)]
        attn_out = jnp.concatenate(pieces, axis=2)
        new_k = jnp.transpose(kt.reshape(b, N_KV_HEADS, HEAD_DIM, t), (0, 3, 1, 2))
        new_v = jnp.transpose(vt.reshape(b, N_KV_HEADS, HEAD_DIM, t), (0, 3, 1, 2))
        new_ki = jnp.transpose(kit, (0, 2, 1))
    else:
        k, v, ki, wi = outs[3:]
        ck, cv, cki = past
        k3, v3, ki3 = k.reshape(b, t, KV_DIM), v.reshape(b, t, KV_DIM), ki.reshape(b, t, IDX_DIM)
        k_sel = min(TOPK_MAX, (ck.shape[1] + t) // 4)
        attn_out = _attn_call(q3, qi3, wi.reshape(b, t, -1), k3, v3, ki3,
                              jnp.transpose(ck, (0, 2, 3, 1)), jnp.transpose(cv, (0, 2, 3, 1)),
                              jnp.transpose(cki, (0, 2, 1)), nb_attn, k_sel, "attn_" + tag).reshape(r, ATTN_DIM)
        new_k, new_v, new_ki = k3.reshape(b, t, N_KV_HEADS, HEAD_DIM), v3.reshape(b, t, N_KV_HEADS, HEAD_DIM), ki3

    y, new_ffn = _post_call(x2d, conv_out.reshape(r, CONV_DIM), attn_out,
                            gt1, sc2, sh2, gt2, g2, gf, w_out, w_up, w_down, fdw_w, fdw_b, ffn_hist,
                            tm_post, t, "post_" + tag)
    return (y.reshape(b, t, d), new_k, new_v, new_ki, new_conv, new_ffn)


def kernel(x_prompt, x_sample, cache_k, cache_v, cache_kidx, state_conv, state_ffn_conv, c_prompt, c_sample,
           w_ada, b_ada, norm1_g, w_in, conv_dw_w, conv_dw_b, conv_ln_g, conv_ln_b, w_out, norm2_g,
           w_up, ffn_dw_w, ffn_dw_b, w_down, final_norm_g):
    depth = w_ada.shape[0]
    assert depth == 1, "the final norm is fused into the single layer's last kernel"
    bp, sp, d = x_prompt.shape
    bs, ts, _ = x_sample.shape
    past_len = cache_k.shape[2]
    l = 0
    mod = _mod_call(jnp.concatenate([c_prompt, c_sample], axis=0), w_ada[l], b_ada[l])
    mods_p = jnp.split(mod[:bp], 6, axis=-1)
    mods_s = jnp.split(mod[bp:], 6, axis=-1)
    w = (norm1_g[l].reshape(1, d), _pad_w_in(w_in[l]), conv_dw_w[l], conv_dw_b[l], conv_ln_g[l], conv_ln_b[l],
         w_out[l].astype(BF16), norm2_g[l].reshape(1, d), w_up[l].astype(BF16), ffn_dw_w[l],
         ffn_dw_b[l].reshape(1, -1), w_down[l].astype(BF16), final_norm_g.reshape(1, d))

    conv0 = jnp.zeros((bp, CONV_HIST, CONV_DIM), F32)
    ffn0 = jnp.zeros((bp, FFN_CONV_WIDTH - 1, 2 * D_FF), F32)
    out_p = _layer_group(x_prompt, mods_p, jnp.arange(sp, dtype=I32), conv0, ffn0, None, w,
                         per_row_mod=False, tag="p", **_group_tiles(bp, sp, cached=False))
    out_s = _layer_group(x_sample, mods_s, past_len + jnp.arange(ts, dtype=I32), state_conv[l], state_ffn_conv[l],
                         (cache_k[l], cache_v[l], cache_kidx[l]), w,
                         per_row_mod=True, tag="s", **_group_tiles(bs, ts, cached=True))
    y_p, k_p, v_p, ki_p, conv_p, ffn_p = out_p
    y_s, k_s, v_s, ki_s, conv_s, ffn_s = out_s
    st = lambda a: a[None]
    return (y_p, y_s, st(k_p), st(v_p), st(ki_p), st(conv_p), st(ffn_p),
            st(k_s), st(v_s), st(ki_s), st(conv_s), st(ffn_s))
```

```python
import functools

import jax
import jax.numpy as jnp
from jax import lax
from jax.experimental import pallas as pl
from jax.experimental.pallas import tpu as pltpu

F32 = jnp.float32
BF16 = jnp.bfloat16
I32 = jnp.int32
I16 = jnp.int16

D_MODEL = 1024
CHUNK = 64
CONV_DIM = 512
CONV_WIDTH = 31
N_HEADS = 8
HEAD_DIM = 64
N_KV_HEADS = 2
ATTN_DIM = N_HEADS * HEAD_DIM
KV_DIM = N_KV_HEADS * HEAD_DIM
ROT_DIM = HEAD_DIM // 4
ROPE_THETA = 500000.0
N_IDX_HEADS = 8
IDX_DIM = 32
TOPK_MAX = 256
D_FF = 2816
FFN_CONV_WIDTH = 3
EPS = 1e-6

LANES = 128
SUBLANES = 8
VMEM_LIMIT = 52 * 1024 * 1024

COL_U, COL_UG, COL_Q, COL_K, COL_V, COL_QI = 0, 512, 1024, 1536, 1664, 1792
COL_KI = 2048
COL_WI = COL_KI + LANES
IN_PAD = COL_WI + LANES

Q_SCALE = HEAD_DIM ** -0.5 * 1.4426950408889634

INT_MIN = -2147483648
KEY_NEG_INF = -2139095040
IDX_BITS = 13


def _order_key(x):
    bits = pltpu.bitcast(x, I32)
    return jnp.where(bits < 0, INT_MIN - bits, bits)


def _cparams(sem):
    return pltpu.CompilerParams(dimension_semantics=sem, vmem_limit_bytes=VMEM_LIMIT)


def _silu(x):
    return x * jax.nn.sigmoid(x)


def _mod_kernel(c_ref, w_ref, b_ref, o_ref):
    s = _silu(c_ref[...]).astype(BF16)
    o_ref[...] = jnp.dot(s, w_ref[...].astype(BF16), preferred_element_type=F32) + b_ref[...]


def _mod_call(c_all, w_ada, b_ada):
    nb, d = c_all.shape
    n = w_ada.shape[1]
    tn = 512
    return pl.pallas_call(
        _mod_kernel,
        grid=(n // tn,),
        in_specs=[pl.BlockSpec((nb, d), lambda j: (0, 0)),
                  pl.BlockSpec((d, tn), lambda j: (0, j)),
                  pl.BlockSpec((1, tn), lambda j: (0, j))],
        out_specs=pl.BlockSpec((nb, tn), lambda j: (0, j)),
        out_shape=jax.ShapeDtypeStruct((nb, n), F32),
        compiler_params=_cparams(("arbitrary",)),
        name="mod",
    )(c_all, w_ada, b_ada.reshape(1, n))


def _rmsnorm_mod(x, g, sc, sh):
    ms = jnp.mean(x * x, axis=-1, keepdims=True)
    return (x * lax.rsqrt(ms + EPS) * g) * (1.0 + sc) + sh


def _inproj_kernel(x_ref, sc_ref, sh_ref, g_ref, w_ref, cos_ref, sa_ref, sb_ref, glu_ref, q_ref, qi_ref, *kv_refs,
                   kv_transposed):
    h = _rmsnorm_mod(x_ref[...], g_ref[...], sc_ref[...], sh_ref[...])
    z = lax.dot_general(h.astype(BF16), w_ref[...], (((1,), (1,)), ((), ())),
                        preferred_element_type=F32)
    glu_ref[...] = z[:, COL_U:COL_U + CONV_DIM] * jax.nn.sigmoid(z[:, COL_UG:COL_UG + CONV_DIM])
    cos, sa, sb = cos_ref[...], sa_ref[...], sb_ref[...]

    def rope(xs):
        return (xs * cos + pltpu.roll(xs, ROT_DIM // 2, axis=1) * sa
                + pltpu.roll(xs, LANES - ROT_DIM // 2, axis=1) * sb)

    for j in range(ATTN_DIM // LANES):
        c0 = COL_Q + LANES * j
        q_ref[:, LANES * j:LANES * (j + 1)] = (rope(z[:, c0:c0 + LANES]) * Q_SCALE).astype(BF16)
    qi_ref[...] = z[:, COL_QI:COL_QI + N_IDX_HEADS * IDX_DIM].astype(BF16)
    k = rope(z[:, COL_K:COL_K + KV_DIM])
    v = z[:, COL_V:COL_V + KV_DIM]
    ki_slab = z[:, COL_KI:COL_KI + LANES]
    wi_slab = z[:, COL_WI:COL_WI + LANES]
    if kv_transposed:
        kbf_ref, kibf_ref, kt_ref, vt_ref, vtbf_ref, kit_ref, wit_ref = kv_refs
        kbf_ref[...] = k.astype(BF16)
        kibf_ref[...] = ki_slab[:, 0:IDX_DIM].astype(BF16)
        kt_ref[...] = k.T
        vt = v.T
        vt_ref[...] = vt
        vtbf_ref[...] = vt.astype(BF16)
        kit_ref[...] = ki_slab.T[0:IDX_DIM, :]
        wit_ref[...] = wi_slab.T[0:N_IDX_HEADS, :]
    else:
        k_ref, v_ref, ki_ref, wi_ref = kv_refs
        k_ref[...] = k
        v_ref[...] = v
        ki_ref[...] = ki_slab[:, 0:IDX_DIM]
        wi_ref[...] = wi_slab[:, 0:N_IDX_HEADS]


def _inproj_call(x2, sc, sh, g1, w_in_p, tabs, tm, name):
    r, d = x2.shape
    nt = r // tm
    kv_t = sc.ndim == 3
    if kv_t:
        nb = sc.shape[0]
        tpb = nt // nb
        t_len = r // nb
        mod_spec = pl.BlockSpec((None, 1, d), lambda i: (i // tpb, 0, 0))
    else:
        mod_spec = pl.BlockSpec((tm, d), lambda i: (i, 0))
    ntab = tabs[0].shape[0] // tm
    tab_spec = pl.BlockSpec((tm, LANES), lambda i: (i % ntab, 0))

    def rows(width):
        return pl.BlockSpec((tm, width), lambda i: (i, 0))

    def rows_shape(width, dtype):
        return jax.ShapeDtypeStruct((r, width), dtype)

    def cols(width):
        return pl.BlockSpec((None, width, tm), lambda i: (i // tpb, 0, i % tpb))

    def cols_shape(width, dtype):
        return jax.ShapeDtypeStruct((nb, width, t_len), dtype)

    out_specs = [rows(CONV_DIM), rows(ATTN_DIM), rows(N_IDX_HEADS * IDX_DIM)]
    out_shapes = [rows_shape(CONV_DIM, F32), rows_shape(ATTN_DIM, BF16), rows_shape(N_IDX_HEADS * IDX_DIM, BF16)]
    if kv_t:
        out_specs += [rows(KV_DIM), rows(IDX_DIM), cols(KV_DIM), cols(KV_DIM), cols(KV_DIM), cols(IDX_DIM),
                      cols(N_IDX_HEADS)]
        out_shapes += [rows_shape(KV_DIM, BF16), rows_shape(IDX_DIM, BF16), cols_shape(KV_DIM, F32),
                       cols_shape(KV_DIM, F32), cols_shape(KV_DIM, BF16), cols_shape(IDX_DIM, F32),
                       cols_shape(N_IDX_HEADS, F32)]
    else:
        out_specs += [rows(KV_DIM), rows(KV_DIM), rows(IDX_DIM), rows(N_IDX_HEADS)]
        out_shapes += [rows_shape(KV_DIM, F32), rows_shape(KV_DIM, F32), rows_shape(IDX_DIM, F32),
                       rows_shape(N_IDX_HEADS, F32)]
    return pl.pallas_call(
        functools.partial(_inproj_kernel, kv_transposed=kv_t),
        grid=(nt,),
        in_specs=[rows(d), mod_spec, mod_spec,
                  pl.BlockSpec((1, d), lambda i: (0, 0)),
                  pl.BlockSpec((IN_PAD, d), lambda i: (0, 0)),
                  tab_spec, tab_spec, tab_spec],
        out_specs=tuple(out_specs),
        out_shape=tuple(out_shapes),
        compiler_params=_cparams(("arbitrary",)),
        name=name,
    )(x2, sc, sh, g1, w_in_p, *tabs)


CONV_HIST = CONV_WIDTH - 1
CONV_PAD = 32
CONV_RB = 64


def _conv_kernel(glu_ref, hist_ref, w_ref, b_ref, lg_ref, lb_ref, o_ref, ext_ref, y_ref, *, tt):
    t = pl.program_id(1)

    @pl.when(t == 0)
    def _():
        ext_ref[0:CONV_PAD - CONV_HIST, :] = jnp.zeros((CONV_PAD - CONV_HIST, CONV_DIM), F32)
        ext_ref[CONV_PAD - CONV_HIST:CONV_PAD, :] = hist_ref[...]

    @pl.when(t > 0)
    def _():
        ext_ref[0:CONV_PAD, :] = ext_ref[tt:tt + CONV_PAD, :]

    ext_ref[CONV_PAD:CONV_PAD + tt, :] = glu_ref[...]

    off = CONV_PAD - CONV_HIST
    rb = min(CONV_RB, tt)
    for r0 in range(0, tt, rb):
        for c0 in range(0, CONV_DIM, LANES):
            acc = None
            for b in range(SUBLANES):
                span = (CONV_WIDTH - 1 - b) // SUBLANES * SUBLANES
                mis = (off + b) % SUBLANES
                base = off + b - mis + r0
                cover = -(-(mis + rb + span) // SUBLANES) * SUBLANES
                win = ext_ref[base:base + cover, c0:c0 + LANES]
                if mis:
                    win = pltpu.roll(win, cover - mis, axis=0)
                for j in range(b, CONV_WIDTH, SUBLANES):
                    term = win[j - b:j - b + rb, :] * w_ref[j:j + 1, c0:c0 + LANES]
                    acc = term if acc is None else acc + term
            y_ref[r0:r0 + rb, c0:c0 + LANES] = acc + b_ref[:, c0:c0 + LANES]

    y = y_ref[...]
    mu = jnp.mean(y, axis=-1, keepdims=True)
    yc = y - mu
    var = jnp.mean(yc * yc, axis=-1, keepdims=True)
    o_ref[...] = _silu(yc * lax.rsqrt(var + EPS) * lg_ref[...] + lb_ref[...]).astype(BF16)


def _conv_call(glu3, hist, dw_w, dw_b, ln_g, ln_b, tt, name):
    b, t, c = glu3.shape
    vec = pl.BlockSpec((1, c), lambda i, j: (0, 0))
    return pl.pallas_call(
        functools.partial(_conv_kernel, tt=tt),
        grid=(b, t // tt),
        in_specs=[pl.BlockSpec((None, tt, c), lambda i, j: (i, j, 0)),
                  pl.BlockSpec((None, CONV_HIST, c), lambda i, j: (i, 0, 0)),
                  pl.BlockSpec((CONV_WIDTH, c), lambda i, j: (0, 0)),
                  vec, vec, vec],
        out_specs=pl.BlockSpec((None, tt, c), lambda i, j: (i, j, 0)),
        out_shape=jax.ShapeDtypeStruct((b, t, c), BF16),
        scratch_shapes=[pltpu.VMEM((tt + CONV_PAD, c), F32), pltpu.VMEM((tt, c), F32)],
        compiler_params=_cparams(("arbitrary", "arbitrary")),
        name=name,
    )(glu3, hist, dw_w, dw_b.reshape(1, c), ln_g.reshape(1, c), ln_b.reshape(1, c))


CNT_ROWS = 64


def _pad_rows(x, n):
    return jnp.concatenate([x, jnp.zeros((n - x.shape[0], x.shape[1]), x.dtype)], axis=0)


def _attn_kernel(q_ref, qi_ref, wi_ref, kn_ref, vn_ref, kin_ref, kct_ref, vct_ref, kict_ref, o_ref, key_ref, bias_ref,
                 *, nb, tq, past, k_sel):
    rows = nb * tq
    n_keys = past + LANES
    nt = (((1,), (1,)), ((), ()))
    col = lax.broadcasted_iota(I32, (rows, n_keys), 1)
    adm = col < past + tq
    new_ok = lax.broadcasted_iota(I32, (tq, LANES), 1) < tq

    for bb in range(nb):
        kic = kict_ref[bb].astype(BF16)
        kin = _pad_rows(kin_ref[bb].astype(BF16), LANES)
        wi = wi_ref[bb]
        sc_c = jnp.zeros((tq, past), F32)
        sc_n = jnp.zeros((tq, LANES), F32)
        for h in range(N_IDX_HEADS):
            qih = qi_ref[bb, :, IDX_DIM * h:IDX_DIM * (h + 1)]
            sc_c = sc_c + wi[:, h:h + 1] * jnp.maximum(jnp.dot(qih, kic, preferred_element_type=F32), 0.0)
            sc_n = sc_n + wi[:, h:h + 1] * jnp.maximum(
                lax.dot_general(qih, kin, nt, preferred_element_type=F32), 0.0)
        key_ref[bb * tq:(bb + 1) * tq, 0:past] = _order_key(sc_c)
        key_ref[bb * tq:(bb + 1) * tq, past:n_keys] = _order_key(jnp.where(new_ok, sc_n, -jnp.inf))

    rg = min(CNT_ROWS, rows)

    def count_ge(cands):
        outs = [[] for _ in cands]
        for r0 in range(0, rows, rg):
            cs = [jnp.broadcast_to(c[r0:r0 + rg, :], (rg, LANES)) for c in cands]
            accs = [jnp.zeros((rg, LANES), F32) for _ in cands]
            for c0 in range(0, n_keys, LANES):
                kv = key_ref[r0:r0 + rg, c0:c0 + LANES]
                accs = [a + jnp.where(kv >= c, 1.0, 0.0) for a, c in zip(accs, cs)]
            for o, a in zip(outs, accs):
                o.append(jnp.sum(a, axis=1, keepdims=True))
        return [o[0] if len(o) == 1 else jnp.concatenate(o, axis=0) for o in outs]

    bpp = 2 if rows <= 64 else 1

    def bits_step(it, r):
        sh = 32 - bpp * (it + 1)
        step = jnp.left_shift(jnp.int32(1), sh)
        counts = count_ge([r + m * step for m in range(1, 1 << bpp)])
        inc = sum(jnp.where(n >= k_sel, 1, 0) for n in counts)
        return r + jnp.left_shift(inc, sh)

    thr = lax.fori_loop(0, 32 // bpp, bits_step, jnp.full((rows, 1), INT_MIN, I32))

    keys = key_ref[...]
    ge = keys >= thr
    cnt_ge = jnp.sum(jnp.where(ge, 1.0, 0.0), axis=1, keepdims=True)
    bias_ref[...] = jnp.where(ge & adm, 0.0, -jnp.inf)
    split_tie = (cnt_ge > k_sel) & (thr != KEY_NEG_INF)
    any_split = jnp.max(jnp.where(split_tie, 1.0, 0.0)) > 0.0

    @pl.when(any_split)
    def _():
        gt = keys > thr
        eq = keys == thr
        need = k_sel - jnp.sum(jnp.where(gt, 1.0, 0.0), axis=1, keepdims=True)

        def idx_step(it, last):
            cand = last + jnp.left_shift(jnp.int32(1), IDX_BITS - 1 - it)
            below = jnp.sum(jnp.where(eq & (col < cand), 1.0, 0.0), axis=1, keepdims=True)
            return jnp.where(below < need, cand, last)

        last = lax.fori_loop(0, IDX_BITS, idx_step, jnp.zeros((rows, 1), I32))
        sel = (gt | (eq & (col <= last))) & adm
        bias_ref[...] = jnp.where(sel, 0.0, -jnp.inf)

    rep = N_HEADS // N_KV_HEADS
    for bb in range(nb):
        bias = bias_ref[bb * tq:(bb + 1) * tq, :]
        for g in range(N_KV_HEADS):
            qg = jnp.concatenate([q_ref[bb, :, HEAD_DIM * (rep * g + r):HEAD_DIM * (rep * g + r + 1)]
                                  for r in range(rep)], axis=0)
            kn = _pad_rows(kn_ref[bb, :, HEAD_DIM * g:HEAD_DIM * (g + 1)].astype(BF16), LANES)
            vn = _pad_rows(vn_ref[bb, :, HEAD_DIM * g:HEAD_DIM * (g + 1)].astype(BF16), LANES)
            logits = jnp.concatenate(
                [jnp.dot(qg, kct_ref[bb, g].astype(BF16), preferred_element_type=F32),
                 lax.dot_general(qg, kn, nt, preferred_element_type=F32)], axis=1)
            ps, ss = [], []
            for r in range(rep):
                lg = logits[r * tq:(r + 1) * tq, :] + bias
                p = jnp.exp2(lg - jnp.max(lg, axis=1, keepdims=True))
                ss.append(jnp.sum(p, axis=1, keepdims=True))
                ps.append(p.astype(BF16))
            p_all = jnp.concatenate(ps, axis=0)
            o = (lax.dot_general(p_all[:, 0:past], vct_ref[bb, g].astype(BF16), nt, preferred_element_type=F32)
                 + jnp.dot(p_all[:, past:n_keys], vn, preferred_element_type=F32))
            for r in range(rep):
                h = rep * g + r
                o_ref[bb, :, HEAD_DIM * h:HEAD_DIM * (h + 1)] = (o[r * tq:(r + 1) * tq, :] / ss[r]).astype(BF16)


def _attn_call(q3, qi3, wi3, k_new, v_new, ki_new, kct, vct, kict, nb, k_sel, name):
    b, tq, _ = q3.shape
    past = kct.shape[-1]
    n_keys = past + LANES
    assert past % LANES == 0 and tq <= LANES and n_keys <= (1 << IDX_BITS) and b % nb == 0

    def rows(width):
        return pl.BlockSpec((nb, tq, width), lambda i: (i, 0, 0))

    kern = functools.partial(_attn_kernel, nb=nb, tq=tq, past=past, k_sel=float(k_sel))
    return pl.pallas_call(
        kern,
        grid=(b // nb,),
        in_specs=[rows(ATTN_DIM), rows(N_IDX_HEADS * IDX_DIM), rows(N_IDX_HEADS),
                  rows(KV_DIM), rows(KV_DIM), rows(IDX_DIM),
                  pl.BlockSpec((nb, N_KV_HEADS, HEAD_DIM, past), lambda i: (i, 0, 0, 0)),
                  pl.BlockSpec((nb, N_KV_HEADS, HEAD_DIM, past), lambda i: (i, 0, 0, 0)),
                  pl.BlockSpec((nb, IDX_DIM, past), lambda i: (i, 0, 0))],
        out_specs=rows(ATTN_DIM),
        out_shape=jax.ShapeDtypeStruct((b, tq, ATTN_DIM), BF16),
        scratch_shapes=[pltpu.VMEM((nb * tq, n_keys), I32), pltpu.VMEM((nb * tq, n_keys), F32)],
        compiler_params=_cparams(("arbitrary",)),
        name=name,
    )(q3, qi3, wi3, k_new, v_new, ki_new, kct, vct, kict)


VT_ROWS = HEAD_DIM + 16
SCORE_KEYS = 256
CNT_KEYS = 64
SUB_KEYS = 32


def _attn_t_kernel(q_ref, qi_ref, wit_ref, k_ref, vt_ref, ki_ref, o_ref, key_ref, bias_ref, hi_ref, lo_ref,
                   d_ref, lg_ref, p_ref, *, tq, n_keys, q0, t_new, past, k_sel):
    j = pl.program_id(1)
    qpos = lax.broadcasted_iota(I32, (1, tq), 1) + (q0 + j * tq)
    limit = past + jnp.minimum((qpos // CHUNK + 1) * CHUNK, t_new)
    nt = (((1,), (1,)), ((), ()))
    sb = SUB_KEYS
    kpos0 = lax.broadcasted_iota(I32, (sb, tq), 0)
    all_adm = past + min((q0 // CHUNK + 1) * CHUNK, t_new)

    qi_stack = jnp.concatenate([qi_ref[:, IDX_DIM * h:IDX_DIM * (h + 1)] for h in range(N_IDX_HEADS)], axis=0)
    wit = wit_ref[...]
    kc = min(SCORE_KEYS, n_keys)
    for c0 in range(0, n_keys, kc):
        d_ref[...] = lax.dot_general(ki_ref[c0:c0 + kc, :], qi_stack, nt, preferred_element_type=F32)
        for s0 in range(0, kc, sb):
            sc = wit[0:1, :] * jnp.maximum(d_ref[s0:s0 + sb, 0:tq], 0.0)
            for h in range(1, N_IDX_HEADS):
                sc = sc + wit[h:h + 1, :] * jnp.maximum(d_ref[s0:s0 + sb, h * tq:(h + 1) * tq], 0.0)
            if c0 + s0 + sb > all_adm:
                sc = jnp.where(kpos0 + (c0 + s0) < limit, sc, -jnp.inf)
            keys = _order_key(sc)
            key_ref[c0 + s0:c0 + s0 + sb, :] = keys
            hi_ref[c0 + s0:c0 + s0 + sb, :] = (keys >> 16).astype(I16)
            lo_ref[c0 + s0:c0 + s0 + sb, :] = (keys ^ 0x8000).astype(I16)

    sl = min(CNT_KEYS, n_keys)

    def search16(ref):
        def bit_step(it, r):
            cand = r + jnp.left_shift(jnp.int32(1), 15 - it)
            cb = jnp.broadcast_to((cand - 32768).astype(I16), (sl, tq))
            acc = jnp.zeros((sl, tq), I16)
            for c0 in range(0, n_keys, sl):
                acc = acc + jnp.where(ref[c0:c0 + sl, :] >= cb, I16(1), I16(0))
            cnt = jnp.sum(acc.astype(I32), axis=0, keepdims=True)
            return jnp.where(cnt >= k_sel, cand, r)
        return lax.fori_loop(0, 16, bit_step, jnp.zeros((1, tq), I32))

    if n_keys > k_sel:
        r_hi = search16(hi_ref)
        h16 = jnp.broadcast_to((r_hi - 32768).astype(I16), (sb, tq))
        for s0 in range(0, n_keys, sb):
            hi = hi_ref[s0:s0 + sb, :]
            lo_ref[s0:s0 + sb, :] = jnp.where(hi == h16, lo_ref[s0:s0 + sb, :],
                                              jnp.where(hi > h16, I16(32767), I16(-32768)))
        thr = ((r_hi - 32768) << 16) | search16(lo_ref)
    else:
        thr = jnp.full((1, tq), KEY_NEG_INF, I32)

    cnt = jnp.zeros((sb, tq), F32)
    for s0 in range(0, n_keys, sb):
        ge = key_ref[s0:s0 + sb, :] >= thr
        cnt = cnt + jnp.where(ge, 1.0, 0.0)
        if s0 + sb > all_adm:
            ge = ge & (kpos0 + s0 < limit)
        bias_ref[s0:s0 + sb, :] = jnp.where(ge, 0.0, -jnp.inf)
    cnt_ge = jnp.sum(cnt, axis=0, keepdims=True)
    split_tie = (cnt_ge > k_sel) & (thr != KEY_NEG_INF)
    any_split = jnp.max(jnp.where(split_tie, 1.0, 0.0)) > 0.0

    @pl.when(any_split)
    def _():
        keys = key_ref[...]
        kpos = lax.broadcasted_iota(I32, (n_keys, tq), 0)
        gt = keys > thr
        eq = keys == thr
        need = k_sel - jnp.sum(jnp.where(gt, 1.0, 0.0), axis=0, keepdims=True)

        def idx_step(it, last):
            cand = last + jnp.left_shift(jnp.int32(1), IDX_BITS - 1 - it)
            below = jnp.sum(jnp.where(eq & (kpos < cand), 1.0, 0.0), axis=0, keepdims=True)
            return jnp.where(below < need, cand, last)

        last = lax.fori_loop(0, IDX_BITS, idx_step, jnp.zeros((1, tq), I32))
        sel = (gt | (eq & (kpos <= last))) & (kpos < limit)
        bias_ref[...] = jnp.where(sel, 0.0, -jnp.inf)

    rep = N_HEADS // N_KV_HEADS
    for g in range(N_KV_HEADS):
        qg = jnp.concatenate([q_ref[:, HEAD_DIM * (rep * g + r):HEAD_DIM * (rep * g + r + 1)] for r in range(rep)],
                             axis=0)
        lg_ref[...] = lax.dot_general(k_ref[:, HEAD_DIM * g:HEAD_DIM * (g + 1)], qg, nt, preferred_element_type=F32)
        for r in range(rep):
            cols = slice(r * tq, (r + 1) * tq)
            mx = jnp.full((sb, tq), -jnp.inf, F32)
            for s0 in range(0, n_keys, sb):
                mx = jnp.maximum(mx, lg_ref[s0:s0 + sb, cols] + bias_ref[s0:s0 + sb, :])
            mb = jnp.broadcast_to(jnp.max(mx, axis=0, keepdims=True), (sb, tq))
            for s0 in range(0, n_keys, sb):
                p_ref[s0:s0 + sb, cols] = jnp.exp2(lg_ref[s0:s0 + sb, cols] + bias_ref[s0:s0 + sb, :] - mb).astype(BF16)
        vt_ext = jnp.concatenate([vt_ref[HEAD_DIM * g:HEAD_DIM * (g + 1), :],
                                  jnp.ones((VT_ROWS - HEAD_DIM, n_keys), BF16)], axis=0)
        o = jnp.dot(vt_ext, p_ref[...], preferred_element_type=F32)
        o = o[0:HEAD_DIM, :] / o[HEAD_DIM:HEAD_DIM + 1, :]
        for r in range(rep):
            h = rep * g + r
            o_ref[HEAD_DIM * h:HEAD_DIM * (h + 1), :] = o[:, r * tq:(r + 1) * tq].astype(BF16)


def _attn_t_call(q3, qi3, wit3, k_all, vt_all, ki_all, tq, q_blk0, n_q_blk, n_keys, t_new, past, k_sel, name):
    b = q3.shape[0]
    assert n_keys % LANES == 0 and n_keys <= (1 << IDX_BITS) and n_keys <= k_all.shape[1] and tq % LANES == 0
    kern = functools.partial(_attn_t_kernel, tq=tq, n_keys=n_keys, q0=q_blk0 * tq, t_new=t_new, past=past,
                             k_sel=float(k_sel))
    return pl.pallas_call(
        kern,
        grid=(b, n_q_blk),
        in_specs=[pl.BlockSpec((None, tq, ATTN_DIM), lambda i, j: (i, q_blk0 + j, 0)),
                  pl.BlockSpec((None, tq, N_IDX_HEADS * IDX_DIM), lambda i, j: (i, q_blk0 + j, 0)),
                  pl.BlockSpec((None, N_IDX_HEADS, tq), lambda i, j: (i, 0, q_blk0 + j)),
                  pl.BlockSpec((None, n_keys, KV_DIM), lambda i, j: (i, 0, 0)),
                  pl.BlockSpec((None, KV_DIM, n_keys), lambda i, j: (i, 0, 0)),
                  pl.BlockSpec((None, n_keys, IDX_DIM), lambda i, j: (i, 0, 0))],
        out_specs=pl.BlockSpec((None, ATTN_DIM, tq), lambda i, j: (i, 0, j)),
        out_shape=jax.ShapeDtypeStruct((b, ATTN_DIM, n_q_blk * tq), BF16),
        scratch_shapes=[pltpu.VMEM((n_keys, tq), I32), pltpu.VMEM((n_keys, tq), F32),
                        pltpu.VMEM((n_keys, tq), I16), pltpu.VMEM((n_keys, tq), I16),
                        pltpu.VMEM((min(SCORE_KEYS, n_keys), N_IDX_HEADS * tq), F32),
                        pltpu.VMEM((n_keys, N_HEADS // N_KV_HEADS * tq), F32),
                        pltpu.VMEM((n_keys, N_HEADS // N_KV_HEADS * tq), BF16)],
        compiler_params=_cparams(("arbitrary", "arbitrary")),
        name=name,
    )(q3, qi3, wit3, k_all, vt_all, ki_all)


FF_CHUNKS = ((0, D_FF),)


def _post_kernel(x_ref, conv_ref, attn_ref, gt1_ref, sc2_ref, sh2_ref, gt2_ref, g2_ref, gf_ref,
                 wout_ref, wup_ref, wdn_ref, fw_ref, fb_ref, hist_ref,
                 y_ref, newffn_ref, *scratch, tm, t_len, tiles_per_batch, attn_transposed):
    nseg = max(1, tm // t_len)
    seg = tm // nseg
    i = pl.program_id(0)
    if nseg == 1:
        carry_ref = scratch[0]

        @pl.when(i % tiles_per_batch == 0)
        def _():
            carry_ref[...] = hist_ref[0]

    w_attn = wout_ref[CONV_DIM:CONV_DIM + ATTN_DIM, :]
    if attn_transposed:
        mix_attn = lax.dot_general(attn_ref[...], w_attn, (((0,), (0,)), ((), ())), preferred_element_type=F32)
    else:
        mix_attn = jnp.dot(attn_ref[...], w_attn, preferred_element_type=F32)
    mix = jnp.dot(conv_ref[...], wout_ref[0:CONV_DIM, :], preferred_element_type=F32) + mix_attn
    x1 = x_ref[...] + gt1_ref[...] * mix
    h2 = _rmsnorm_mod(x1, g2_ref[...], sc2_ref[...], sh2_ref[...]).astype(BF16)

    row = lax.broadcasted_iota(I32, (SUBLANES, 1), 0)

    def causal3(u, col0, width):
        w0 = fw_ref[0:1, col0:col0 + width]
        w1 = fw_ref[1:2, col0:col0 + width]
        w2 = fw_ref[2:3, col0:col0 + width]
        outs = []
        for s in range(nseg):
            us = u[s * seg:(s + 1) * seg, :]
            if nseg == 1:
                h0 = carry_ref[0:1, col0:col0 + width]
                h1 = carry_ref[1:2, col0:col0 + width]
            else:
                h0 = hist_ref[s, 0:1, col0:col0 + width]
                h1 = hist_ref[s, 1:2, col0:col0 + width]
            p1 = pltpu.roll(us, 1, axis=0)
            p2 = pltpu.roll(us, 2, axis=0)
            p1 = jnp.concatenate([jnp.where(row == 0, h1, p1[0:SUBLANES, :]), p1[SUBLANES:, :]], axis=0)
            p2 = jnp.concatenate([jnp.where(row == 0, h0, jnp.where(row == 1, h1, p2[0:SUBLANES, :])),
                                  p2[SUBLANES:, :]], axis=0)
            outs.append(us * w2 + p1 * w1 + p2 * w0 + fb_ref[:, col0:col0 + width])
            if nseg == 1:
                carry_ref[:, col0:col0 + width] = us[seg - 2:seg, :]
            else:
                newffn_ref[s, :, col0:col0 + width] = us[seg - 2:seg, :]
        return outs[0] if nseg == 1 else jnp.concatenate(outs, axis=0)

    acc = jnp.zeros((tm, D_MODEL), F32)
    for c0, cw in FF_CHUNKS:
        ua = jnp.dot(h2, wup_ref[:, c0:c0 + cw], preferred_element_type=F32)
        ug = jnp.dot(h2, wup_ref[:, D_FF + c0:D_FF + c0 + cw], preferred_element_type=F32)
        a = causal3(ua, c0, cw)
        g = causal3(ug, D_FF + c0, cw)
        acc = acc + jnp.dot((a * _silu(g)).astype(BF16), wdn_ref[c0:c0 + cw, :], preferred_element_type=F32)

    if nseg == 1:
        @pl.when(i % tiles_per_batch == tiles_per_batch - 1)
        def _():
            newffn_ref[0] = carry_ref[...]

    x2 = x1 + gt2_ref[...] * acc
    ms = jnp.mean(x2 * x2, axis=-1, keepdims=True)
    y_ref[...] = x2 * lax.rsqrt(ms + EPS) * gf_ref[...]


def _post_call(x2d, conv2d, attn, gt1, sc2, sh2, gt2, g2, gf, wout, wup, wdn, fw, fb, hist, tm, t_len, name):
    r, d = x2d.shape
    nt = r // tm
    nb = hist.shape[0]
    nseg = max(1, tm // t_len)
    tpb = max(1, t_len // tm)
    assert nt * nseg == nb * tpb and t_len >= FFN_CONV_WIDTH - 1
    if gt1.ndim == 3:
        mod_spec = pl.BlockSpec((None, 1, d), lambda i: (i // tpb, 0, 0))
    else:
        mod_spec = pl.BlockSpec((tm, d), lambda i: (i, 0))

    def rows(width):
        return pl.BlockSpec((tm, width), lambda i: (i, 0))

    def const(shape):
        return pl.BlockSpec(shape, lambda i: (0,) * len(shape), pipeline_mode=pl.Buffered(1))

    state_spec = pl.BlockSpec((nseg, FFN_CONV_WIDTH - 1, 2 * D_FF), lambda i: (i * nseg // tpb, 0, 0))
    attn_t = attn.ndim == 3
    if attn_t:
        assert nseg == 1
        attn_spec = pl.BlockSpec((None, ATTN_DIM, tm), lambda i: (i // tpb, 0, i % tpb))
    else:
        attn_spec = rows(ATTN_DIM)
    kern = functools.partial(_post_kernel, tm=tm, t_len=t_len, tiles_per_batch=tpb, attn_transposed=attn_t)
    scratch = [pltpu.VMEM((FFN_CONV_WIDTH - 1, 2 * D_FF), F32)] if nseg == 1 else []
    return pl.pallas_call(
        kern,
        grid=(nt,),
        in_specs=[rows(d), rows(CONV_DIM), attn_spec, mod_spec, mod_spec, mod_spec, mod_spec,
                  const((1, d)), const((1, d)),
                  const((CONV_DIM + ATTN_DIM, d)), const((d, 2 * D_FF)), const((D_FF, d)),
                  const((FFN_CONV_WIDTH, 2 * D_FF)), const((1, 2 * D_FF)),
                  state_spec],
        out_specs=(rows(d), state_spec),
        out_shape=(jax.ShapeDtypeStruct((r, d), F32),
                   jax.ShapeDtypeStruct((nb, FFN_CONV_WIDTH - 1, 2 * D_FF), F32)),
        scratch_shapes=scratch,
        compiler_params=_cparams(("arbitrary",)),
        name=name,
    )(x2d, conv2d, attn, gt1, sc2, sh2, gt2, g2, gf, wout, wup, wdn, fw, fb, hist)


def _rope_tables(pos):
    half = ROT_DIM // 2
    inv = 1.0 / (ROPE_THETA ** (jnp.arange(0, ROT_DIM, 2, dtype=F32) / ROT_DIM))
    ang = pos.astype(F32)[:, None] * inv[None, :]
    cos, sin = jnp.cos(ang), jnp.sin(ang)
    t = pos.shape[0]
    rest1 = jnp.ones((t, HEAD_DIM - ROT_DIM), F32)
    rest0 = jnp.zeros((t, HEAD_DIM - ROT_DIM), F32)
    z = jnp.zeros((t, half), F32)
    c64 = jnp.concatenate([cos, cos, rest1], axis=1)
    a64 = jnp.concatenate([z, sin, rest0], axis=1)
    b64 = jnp.concatenate([-sin, z, rest0], axis=1)
    return tuple(jnp.concatenate([m, m], axis=1) for m in (c64, a64, b64))


def _pad_w_in(w_in):
    d = w_in.shape[0]
    w_t = jnp.transpose(w_in).astype(BF16)
    return jnp.concatenate(
        [w_t[:COL_KI + IDX_DIM], jnp.zeros((LANES - IDX_DIM, d), BF16),
         w_t[COL_KI + IDX_DIM:], jnp.zeros((LANES - N_IDX_HEADS, d), BF16)], axis=0)


def _group_tiles(b, t, cached):
    if cached:
        return dict(tm_in=b * t, tt_conv=t, tq=t, tm_post=b * t, nb_attn=4)
    return dict(tm_in=1024, tt_conv=512, tq=256, tm_post=512, nb_attn=1)


def _layer_group(x, mods, pos, conv_hist, ffn_hist, past, w, *, per_row_mod, tm_in, tt_conv, tq, tm_post, tag, nb_attn=1):
    (g1, w_in_p, dw_w, dw_b, ln_g, ln_b, w_out, g2, w_up, fdw_w, fdw_b, w_down, gf) = w
    b, t, d = x.shape
    r = b * t
    x2d = x.reshape(r, d)
    if per_row_mod:
        sh1, sc1, gt1, sh2, sc2, gt2 = [jnp.repeat(m, t, axis=0) for m in mods]
        tabs = tuple(jnp.tile(m, (b, 1)) for m in _rope_tables(pos))
    else:
        sh1, sc1, gt1, sh2, sc2, gt2 = [m[:, None, :] for m in mods]
        tabs = _rope_tables(pos)

    outs = _inproj_call(x2d, sc1, sh1, g1, w_in_p, tabs, tm_in, "inproj_" + tag)
    glu, q, qi = outs[:3]
    glu3 = glu.reshape(b, t, CONV_DIM)
    q3, qi3 = q.reshape(b, t, ATTN_DIM), qi.reshape(b, t, -1)

    conv_out = _conv_call(glu3, conv_hist, dw_w, dw_b, ln_g, ln_b, tt_conv, "conv_" + tag)
    new_conv = glu3[:, t - CONV_HIST:, :]

    if past is None:
        k_bf, ki_bf, kt, vt, vt_bf, kit, wit = outs[3:]
        k_sel = min(TOPK_MAX, t // 4)
        pieces = [_attn_t_call(q3, qi3, wit, k_bf.reshape(b, t, KV_DIM), vt_bf, ki_bf.reshape(b, t, IDX_DIM),
                               tq, i, 1, (i + 1) * tq, t, 0, k_sel, "attn_%s%d" % (tag, i)) for i in range(t // tq)]
        attn_out = jnp.concatenate(pieces, axis=2)
        new_k = jnp.transpose(kt.reshape(b, N_KV_HEADS, HEAD_DIM, t), (0, 3, 1, 2))
        new_v = jnp.transpose(vt.reshape(b, N_KV_HEADS, HEAD_DIM, t), (0, 3, 1, 2))
        new_ki = jnp.transpose(kit, (0, 2, 1))
    else:
        k, v, ki, wi = outs[3:]
        ck, cv, cki = past
        k3, v3, ki3 = k.reshape(b, t, KV_DIM), v.reshape(b, t, KV_DIM), ki.reshape(b, t, IDX_DIM)
        k_sel = min(TOPK_MAX, (ck.shape[1] + t) // 4)
        attn_out = _attn_call(q3, qi3, wi.reshape(b, t, -1), k3, v3, ki3,
                              jnp.transpose(ck, (0, 2, 3, 1)), jnp.transpose(cv, (0, 2, 3, 1)),
                              jnp.transpose(cki, (0, 2, 1)), nb_attn, k_sel, "attn_" + tag).reshape(r, ATTN_DIM)
        new_k, new_v, new_ki = k3.reshape(b, t, N_KV_HEADS, HEAD_DIM), v3.reshape(b, t, N_KV_HEADS, HEAD_DIM), ki3

    y, new_ffn = _post_call(x2d, conv_out.reshape(r, CONV_DIM), attn_out,
                            gt1, sc2, sh2, gt2, g2, gf, w_out, w_up, w_down, fdw_w, fdw_b, ffn_hist,
                            tm_post, t, "post_" + tag)
    return (y.reshape(b, t, d), new_k, new_v, new_ki, new_conv, new_ffn)


def kernel(x_prompt, x_sample, cache_k, cache_v, cache_kidx, state_conv, state_ffn_conv, c_prompt, c_sample,
           w_ada, b_ada, norm1_g, w_in, conv_dw_w, conv_dw_b, conv_ln_g, conv_ln_b, w_out, norm2_g,
           w_up, ffn_dw_w, ffn_dw_b, w_down, final_norm_g):
    depth = w_ada.shape[0]
    assert depth == 1, "the final norm is fused into the single layer's last kernel"
    bp, sp, d = x_prompt.shape
    bs, ts, _ = x_sample.shape
    past_len = cache_k.shape[2]
    l = 0
    mod = _mod_call(jnp.concatenate([c_prompt, c_sample], axis=0), w_ada[l], b_ada[l])
    mods_p = jnp.split(mod[:bp], 6, axis=-1)
    mods_s = jnp.split(mod[bp:], 6, axis=-1)
    w = (norm1_g[l].reshape(1, d), _pad_w_in(w_in[l]), conv_dw_w[l], conv_dw_b[l], conv_ln_g[l], conv_ln_b[l],
         w_out[l].astype(BF16), norm2_g[l].reshape(1, d), w_up[l].astype(BF16), ffn_dw_w[l],
         ffn_dw_b[l].reshape(1, -1), w_down[l].astype(BF16), final_norm_g.reshape(1, d))

    conv0 = jnp.zeros((bp, CONV_HIST, CONV_DIM), F32)
    ffn0 = jnp.zeros((bp, FFN_CONV_WIDTH - 1, 2 * D_FF), F32)
    out_p = _layer_group(x_prompt, mods_p, jnp.arange(sp, dtype=I32), conv0, ffn0, None, w,
                         per_row_mod=False, tag="p", **_group_tiles(bp, sp, cached=False))
    out_s = _layer_group(x_sample, mods_s, past_len + jnp.arange(ts, dtype=I32), state_conv[l], state_ffn_conv[l],
                         (cache_k[l], cache_v[l], cache_kidx[l]), w,
                         per_row_mod=True, tag="s", **_group_tiles(bs, ts, cached=True))
    y_p, k_p, v_p, ki_p, conv_p, ffn_p = out_p
    y_s, k_s, v_s, ki_s, conv_s, ffn_s = out_s
    st = lambda a: a[None]
    return (y_p, y_s, st(k_p), st(v_p), st(ki_p), st(conv_p), st(ffn_p),
            st(k_s), st(v_s), st(ki_s), st(conv_s), st(ffn_s))
```
